```python
import jax, jax.numpy as jnp
from jax import lax
import numpy as np

D_MODEL = 1024
BATCH = 8
SEQ = 4096
DEPTH = 2

N_A_LAYERS = DEPTH // 2
N_B_LAYERS = DEPTH - N_A_LAYERS
N_DENSE_LAYERS = (DEPTH + 1) // 2
N_MOE_LAYERS = DEPTH // 2

PLE_DIM = 256
POOL_WINDOWS = (2, 4, 8, 16)
N_POOL_GROUPS = len(POOL_WINDOWS)
POOL_GROUP = D_MODEL // N_POOL_GROUPS

HEAD_DIM = 64
N_HEADS = D_MODEL // HEAD_DIM
N_KV_HEADS = 4
Q_PER_KV = N_HEADS // N_KV_HEADS
WINDOW = 128
BLOCK = 128
ROPE_THETA = 10000.0

D_FF = ((8 * D_MODEL // 3 + 255) // 256) * 256
N_EXPERTS = 8
TOP_K = 2
D_FF_EXPERT = D_MODEL
EPS = 1e-6

kernel_name = "yoco_pool_swa_sink_moe_hybrid"


def rmsnorm(x, g):
    xf = x.astype(jnp.float32)
    y = xf * lax.rsqrt(jnp.mean(xf * xf, axis=-1, keepdims=True) + EPS) * g.astype(jnp.float32)
    return y.astype(x.dtype)


def rope_tables(seq):
    inv = ROPE_THETA ** (-jnp.arange(0, HEAD_DIM, 2, dtype=jnp.float32) / HEAD_DIM)
    ang = jnp.arange(seq, dtype=jnp.float32)[:, None] * inv[None, :]
    return jnp.cos(ang), jnp.sin(ang)


def apply_rope(x, cos, sin):
    xf = x.astype(jnp.float32)
    x1, x2 = jnp.split(xf, 2, axis=-1)
    c = cos[None, :, None, :]
    s = sin[None, :, None, :]
    return jnp.concatenate([x1 * c - x2 * s, x2 * c + x1 * s], axis=-1).astype(x.dtype)


def pool_mixer(u, w, scale):
    seq = u.shape[1]
    uf = u.astype(jnp.float32)
    cs = jnp.cumsum(uf, axis=1)
    pos = jnp.arange(1, seq + 1, dtype=jnp.float32)
    outs = []
    for g, win in enumerate(POOL_WINDOWS):
        c0 = g * POOL_GROUP
        ug = uf[..., c0:c0 + POOL_GROUP]
        csg = cs[..., c0:c0 + POOL_GROUP]
        lag = jnp.pad(csg[:, :-win], ((0, 0), (win, 0), (0, 0)))
        cnt = jnp.minimum(pos, float(win))[None, :, None]
        d = ((csg - lag) / cnt - ug).astype(u.dtype)
        outs.append(jnp.einsum("bsc,cd->bsd", d, w[g]))
    return jnp.concatenate(outs, axis=-1) * scale


def shared_kv(h, kv_norm, w_kv, k_norm, cos, sin):
    b, s, _ = h.shape
    kv = rmsnorm(h, kv_norm) @ w_kv
    k, v = jnp.split(kv, 2, axis=-1)
    k = k.reshape(b, s, N_KV_HEADS, HEAD_DIM)
    v = v.reshape(b, s, N_KV_HEADS, HEAD_DIM)
    k = apply_rope(rmsnorm(k, k_norm), cos, sin)
    return k, v


def sliding_sink_attention(q, k, v, sinks):
    b, s, _, _ = q.shape
    nb = s // BLOCK
    qb = q.reshape(b, nb, BLOCK, N_KV_HEADS, Q_PER_KV, HEAD_DIM).transpose(1, 0, 2, 3, 4, 5)

    def band(t):
        tb = t.reshape(b, nb, BLOCK, N_KV_HEADS, HEAD_DIM).transpose(1, 0, 2, 3, 4)
        prev = jnp.concatenate([jnp.zeros_like(tb[:1]), tb[:-1]], axis=0)
        return jnp.concatenate([prev, tb], axis=2)

    kb, vb = band(k), band(v)
    qi = jnp.arange(BLOCK)[:, None]
    kj = jnp.arange(2 * BLOCK)[None, :]
    diff = BLOCK + qi - kj
    in_window = (diff >= 0) & (diff < WINDOW)
    sink = sinks.astype(jnp.float32).reshape(N_KV_HEADS, Q_PER_KV)[None, :, :, None, None]
    scale = HEAD_DIM ** -0.5

    def block_fn(args):
        qn, kn, vn, n = args
        sc = jnp.einsum("bqhgd,bkhd->bhgqk", qn.astype(jnp.float32), kn.astype(jnp.float32)) * scale
        key_ok = (n * BLOCK - BLOCK + kj) >= 0
        mask = in_window & key_ok
        sc = jnp.where(mask[None, None, None], sc, -jnp.inf)
        m = jnp.maximum(jnp.max(sc, axis=-1, keepdims=True), sink)
        pr = jnp.exp(sc - m)
        denom = jnp.sum(pr, axis=-1, keepdims=True) + jnp.exp(sink - m)
        out = jnp.einsum("bhgqk,bkhd->bqhgd", pr / denom, vn.astype(jnp.float32))
        return out.astype(qn.dtype)

    o = lax.map(block_fn, (qb, kb, vb, jnp.arange(nb)))
    return o.transpose(1, 0, 2, 3, 4, 5).reshape(b, s, N_HEADS * HEAD_DIM)


def swiglu(u, w_gu, w_down):
    a, g = jnp.split(u @ w_gu, 2, axis=-1)
    return (jax.nn.silu(a) * g) @ w_down


def moe_swiglu(u, router_w, router_b, we_gu, we_down):
    b, s, d = u.shape
    t = u.reshape(b * s, d)
    logits = (t @ router_w).astype(jnp.float32) + router_b.astype(jnp.float32)
    top_v, top_i = lax.top_k(logits, TOP_K)
    top_w = jax.nn.softmax(top_v, axis=-1)
    gates = jnp.sum(jax.nn.one_hot(top_i, N_EXPERTS, dtype=jnp.float32) * top_w[..., None], axis=1)
    gates = gates.astype(u.dtype)
    y = jnp.zeros_like(t)
    for e in range(N_EXPERTS):
        y = y + gates[:, e:e + 1] * swiglu(t, we_gu[e], we_down[e])
    return y.reshape(b, s, d)


def setup_inputs(seed: int = 0) -> dict:
    key = jax.random.key(seed)
    ks = iter(jax.random.split(key, 32))

    def nrm(shape, scale):
        return jax.random.normal(next(ks), shape, jnp.float32) * scale

    def gain(shape):
        return 1.0 + nrm(shape, 0.02)

    D = D_MODEL
    return {
        "x": nrm((BATCH, SEQ, D), 1.0),
        "p": nrm((DEPTH, BATCH, SEQ, PLE_DIM), 1.0),
        "pool_norm": gain((N_A_LAYERS, D)),
        "pool_w": nrm((N_A_LAYERS, N_POOL_GROUPS, POOL_GROUP, POOL_GROUP), POOL_GROUP ** -0.5),
        "pool_scale": 1.0 + nrm((N_A_LAYERS, D), 0.1),
        "kv_norm": gain((D,)),
        "w_kv": nrm((D, 2 * N_KV_HEADS * HEAD_DIM), D ** -0.5),
        "k_norm": gain((HEAD_DIM,)),
        "attn_norm": gain((N_B_LAYERS, D)),
        "w_q": nrm((N_B_LAYERS, D, N_HEADS * HEAD_DIM), D ** -0.5),
        "q_norm": gain((N_B_LAYERS, HEAD_DIM)),
        "sinks": nrm((N_B_LAYERS, N_HEADS), 1.0),
        "w_o": nrm((N_B_LAYERS, N_HEADS * HEAD_DIM, D), (N_HEADS * HEAD_DIM) ** -0.5),
        "ffn_norm": gain((DEPTH, D)),
        "w_gu": nrm((N_DENSE_LAYERS, D, 2 * D_FF), D ** -0.5),
        "w_down": nrm((N_DENSE_LAYERS, D_FF, D), D_FF ** -0.5),
        "router_w": nrm((N_MOE_LAYERS, D, N_EXPERTS), D ** -0.5),
        "router_b": nrm((N_MOE_LAYERS, N_EXPERTS), 0.01),
        "we_gu": nrm((N_MOE_LAYERS, N_EXPERTS, D, 2 * D_FF_EXPERT), D ** -0.5),
        "we_down": nrm((N_MOE_LAYERS, N_EXPERTS, D_FF_EXPERT, D), D_FF_EXPERT ** -0.5),
        "ple_gate_norm": gain((DEPTH, D)),
        "ple_gate_w": nrm((DEPTH, D, D), D ** -0.5),
        "ple_w": nrm((DEPTH, PLE_DIM, D), PLE_DIM ** -0.5),
    }


def reference(x, p, pool_norm, pool_w, pool_scale, kv_norm, w_kv, k_norm, attn_norm, w_q,
              q_norm, sinks, w_o, ffn_norm, w_gu, w_down, router_w, router_b, we_gu, we_down,
              ple_gate_norm, ple_gate_w, ple_w):
    b, s, d = x.shape
    cos, sin = rope_tables(s)
    h = x
    k_sh = v_sh = None
    for i in range(DEPTH):
        if i < N_A_LAYERS:
            h = h + pool_mixer(rmsnorm(h, pool_norm[i]), pool_w[i], pool_scale[i])
        else:
            j = i - N_A_LAYERS
            if j == 0:
                k_sh, v_sh = shared_kv(h, kv_norm, w_kv, k_norm, cos, sin)
            q = (rmsnorm(h, attn_norm[j]) @ w_q[j]).reshape(b, s, N_HEADS, HEAD_DIM)
            q = apply_rope(rmsnorm(q, q_norm[j]), cos, sin)
            h = h + sliding_sink_attention(q, k_sh, v_sh, sinks[j]) @ w_o[j]
        u = rmsnorm(h, ffn_norm[i])
        if i % 2 == 0:
            h = h + swiglu(u, w_gu[i // 2], w_down[i // 2])
        else:
            h = h + moe_swiglu(u, router_w[i // 2], router_b[i // 2], we_gu[i // 2], we_down[i // 2])
        gate = jax.nn.sigmoid(rmsnorm(h, ple_gate_norm[i]) @ ple_gate_w[i])
        h = h + gate * (p[i].astype(h.dtype) @ ple_w[i])
    return h
```

```python
import functools

import jax
import jax.numpy as jnp
from jax import lax
from jax.experimental import pallas as pl
from jax.experimental.pallas import tpu as pltpu

D_MODEL = 1024
PLE_DIM = 256
POOL_WINDOWS = (2, 4, 8, 16)
POOL_GROUP = D_MODEL // len(POOL_WINDOWS)
POOL_HALO = 16
HEAD_DIM = 64
N_HEADS = 16
N_KV_HEADS = 4
KV_DIM = N_KV_HEADS * HEAD_DIM
WINDOW = 128
ROPE_THETA = 10000.0
D_FF = 2816
FF_CHUNK = 256
N_EXPERTS = 8
D_FF_EXPERT = 1024
EPS = 1e-6
LANES = 128
ROUTER_PAD = LANES

SEQ_TILE = 512
ATTN_TILE = 512
MOE_TILE = 512
VMEM_LIMIT = 56 * 1024 * 1024

_BF = jnp.bfloat16
_F32 = jnp.float32


def _rms(x, g):
    return x * lax.rsqrt(jnp.mean(x * x, axis=-1, keepdims=True) + EPS) * g


def _sigmoid(x):
    return 1.0 / (1.0 + jnp.exp(-x))


def _dot(a, b):
    return jnp.dot(a, b, preferred_element_type=_F32)


def _head_sumsq(x, hmat):
    x2 = x * x
    hi = x2.astype(_BF)
    lo = (x2 - hi.astype(_F32)).astype(_BF)
    outs = []
    for s in range(x.shape[1] // LANES):
        sl = slice(s * LANES, (s + 1) * LANES)
        outs.append(_dot(hi[:, sl], hmat) + _dot(lo[:, sl], hmat))
    return jnp.concatenate(outs, axis=1)


def _rope(x, cos, sin_signed):
    n = x.shape[1]
    reps = n // LANES
    lane = lax.broadcasted_iota(jnp.int32, (1, n), 1)
    first_half = (lane % HEAD_DIM) < (HEAD_DIM // 2)
    partner = jnp.where(first_half, pltpu.roll(x, n - HEAD_DIM // 2, 1), pltpu.roll(x, HEAD_DIM // 2, 1))
    c = jnp.concatenate([cos] * reps, axis=1)
    s = jnp.concatenate([sin_signed] * reps, axis=1)
    return x * c + partner * s


def _layer0_kernel(x_ref, p_ref, cos_ref, sin_ref, pool_norm, pool_w, pool_scale, ffn_norm, wa, wg, wd,
                   gate_norm, gate_w, ple_w, attn_norm, wq, q_gain, kv_norm, wkv, k_gain, hmat,
                   h_out, q_out, k_out, v_out, ubuf):
    si = pl.program_id(1)
    ts = x_ref.shape[0]
    x = x_ref[...]

    @pl.when(si == 0)
    def _():
        ubuf[0:POOL_HALO, :] = jnp.zeros((POOL_HALO, D_MODEL), _F32)

    u = _rms(x, pool_norm[...])
    ubuf[POOL_HALO:POOL_HALO + ts, :] = u
    pos = (si * ts + 1 + lax.broadcasted_iota(jnp.int32, (ts, 1), 0)).astype(_F32)
    mixed = []
    for g, win in enumerate(POOL_WINDOWS):
        cols = slice(g * POOL_GROUP, (g + 1) * POOL_GROUP)
        ug = u[:, cols]
        acc = ug
        for k in range(1, win):
            acc = acc + ubuf[POOL_HALO - k:POOL_HALO - k + ts, cols]
        d = acc / jnp.minimum(pos, float(win)) - ug
        mixed.append(_dot(d.astype(_BF), pool_w[g]))
    h = x + jnp.concatenate(mixed, axis=1) * pool_scale[...]
    ubuf[0:POOL_HALO, :] = ubuf[ts:ts + POOL_HALO, :]

    uf = _rms(h, ffn_norm[...]).astype(_BF)

    def ff_body(j, acc):
        a = _dot(uf, wa[j])
        g = _dot(uf, wg[j])
        act = (a * _sigmoid(a) * g).astype(_BF)
        return acc + _dot(act, wd[j])

    h = h + lax.fori_loop(0, wa.shape[0], ff_body, jnp.zeros((ts, D_MODEL), _F32))

    gate = _sigmoid(_dot(_rms(h, gate_norm[...]).astype(_BF), gate_w[...]))
    h = h + gate * _dot(p_ref[...].astype(_BF), ple_w[...])
    h_out[...] = h

    cos = cos_ref[...]
    sin = sin_ref[...]
    q = _dot(_rms(h, attn_norm[...]).astype(_BF), wq[...])
    q = q * lax.rsqrt(_head_sumsq(q, hmat[...]) * (1.0 / HEAD_DIM) + EPS) * q_gain[...]
    q_out[...] = _rope(q, cos, sin).astype(_BF)
    kv = _dot(_rms(h, kv_norm[...]).astype(_BF), wkv[...])
    k = kv[:, :KV_DIM]
    k = k * lax.rsqrt(_head_sumsq(k, hmat[...]) * (1.0 / HEAD_DIM) + EPS) * k_gain[...]
    k_out[...] = _rope(k, cos, sin).astype(_BF)
    v_out[...] = kv[:, KV_DIM:].astype(_BF)


def _attn_kernel(sinks_ref, q_ref, kp_ref, kc_ref, vp_ref, vc_ref, h_ref, wo_ref, out_ref,
                 klo, khi, vlo, vhi, o_scr):
    i = pl.program_id(1)
    tq = q_ref.shape[0]
    nblk = tq // WINDOW
    lane = lax.broadcasted_iota(jnp.int32, (1, LANES), 1)
    low = lane < HEAD_DIM

    qi = lax.broadcasted_iota(jnp.int32, (2 * WINDOW, 2 * WINDOW), 0) % WINDOW
    kj = lax.broadcasted_iota(jnp.int32, (2 * WINDOW, 2 * WINDOW), 1)
    in_window = (kj > qi) & (kj <= qi + WINDOW)
    upper_rows = lax.broadcasted_iota(jnp.int32, (2 * WINDOW, 1), 0) >= WINDOW

    for g in range(N_KV_HEADS):
        slab = slice((g // 2) * LANES, (g // 2 + 1) * LANES)
        for src_p, src_c, lo_ref, hi_ref in ((kp_ref, kc_ref, klo, khi), (vp_ref, vc_ref, vlo, vhi)):
            t = jnp.concatenate([src_p[:, slab], src_c[:, slab]], axis=0).astype(_F32)
            r = pltpu.roll(t, HEAD_DIM, 1)
            in_low, in_high = (t, r) if g % 2 == 0 else (r, t)
            lo_ref[...] = jnp.where(low, in_low, 0.0).astype(_BF)
            hi_ref[...] = jnp.where(low, 0.0, in_high).astype(_BF)

        def blk(j, carry):
            r0 = pl.multiple_of(j * WINDOW, WINDOW)
            qrows = pl.ds(r0, WINDOW)
            krows = pl.ds(r0, 2 * WINDOW)
            qg = jnp.concatenate([q_ref[qrows, (2 * g) * LANES:(2 * g + 1) * LANES],
                                  q_ref[qrows, (2 * g + 1) * LANES:(2 * g + 2) * LANES]], axis=0)
            first_valid_key = jnp.where((i > 0) | (j > 0), 0, WINDOW)
            mask = in_window & (kj >= first_valid_key)
            o = jnp.zeros((2 * WINDOW, LANES), _F32)
            for half, (k_ref, v_ref) in enumerate(((klo, vlo), (khi, vhi))):
                sink = jnp.where(upper_rows, sinks_ref[4 * g + 2 + half], sinks_ref[4 * g + half])
                s = lax.dot_general(qg, k_ref[krows, :], (((1,), (1,)), ((), ())),
                                    preferred_element_type=_F32)
                s = jnp.where(mask, s, -jnp.inf)
                m = jnp.maximum(jnp.max(s, axis=1, keepdims=True), sink)
                pr = jnp.exp(s - m)
                denom = jnp.sum(pr, axis=1, keepdims=True) + jnp.exp(sink - m)
                o = o + _dot(pr.astype(_BF), v_ref[krows, :]) / denom
            o_scr[qrows, (2 * g) * LANES:(2 * g + 1) * LANES] = o[:WINDOW].astype(_BF)
            o_scr[qrows, (2 * g + 1) * LANES:(2 * g + 2) * LANES] = o[WINDOW:].astype(_BF)
            return carry

        lax.fori_loop(0, nblk, blk, 0)

    out_ref[...] = h_ref[...] + _dot(o_scr[...], wo_ref[...])


def _moe_kernel(h_ref, p_ref, ffn_norm, rw_hi, rw_lo, rb, wa, wg, wd, gate_norm, gate_w, ple_w,
                out_ref, u_scr, gates_scr, acc_scr):
    e = pl.program_id(1)
    lane = lax.broadcasted_iota(jnp.int32, (1, ROUTER_PAD), 1)

    @pl.when(e == 0)
    def _():
        u = _rms(h_ref[...], ffn_norm[...])
        u_hi = u.astype(_BF)
        u_lo = (u - u_hi.astype(_F32)).astype(_BF)
        u_scr[...] = u_hi
        logits = _dot(u_hi, rw_hi[...]) + _dot(u_lo, rw_hi[...]) + _dot(u_hi, rw_lo[...]) + rb[...]
        logits = jnp.where(lane < N_EXPERTS, logits, -jnp.inf)
        m1 = jnp.max(logits, axis=1, keepdims=True)
        i1 = jnp.min(jnp.where(logits == m1, lane, ROUTER_PAD), axis=1, keepdims=True)
        rest = jnp.where(lane == i1, -jnp.inf, logits)
        m2 = jnp.max(rest, axis=1, keepdims=True)
        i2 = jnp.min(jnp.where(rest == m2, lane, ROUTER_PAD), axis=1, keepdims=True)
        t = jnp.exp(m2 - m1)
        w1 = 1.0 / (1.0 + t)
        gates_scr[...] = jnp.where(lane == i1, w1, 0.0) + jnp.where(lane == i2, t * w1, 0.0)
        acc_scr[...] = jnp.zeros_like(acc_scr)

    u = u_scr[...]
    a = _dot(u, wa[...])
    g = _dot(u, wg[...])
    act = (a * _sigmoid(a) * g).astype(_BF)
    gate_e = jnp.sum(jnp.where(lane == e, gates_scr[...], 0.0), axis=1, keepdims=True)
    acc_scr[...] += gate_e * _dot(act, wd[...])

    @pl.when(e == pl.num_programs(1) - 1)
    def _():
        h = h_ref[...] + acc_scr[...]
        gate = _sigmoid(_dot(_rms(h, gate_norm[...]).astype(_BF), gate_w[...]))
        out_ref[...] = h + gate * _dot(p_ref[...].astype(_BF), ple_w[...])


def _const(shape):
    return pl.BlockSpec(shape, lambda *_: (0,) * len(shape), pipeline_mode=pl.Buffered(1))


def _row(v):
    return v.reshape(1, -1).astype(_F32)


def kernel(x, p, pool_norm, pool_w, pool_scale, kv_norm, w_kv, k_norm, attn_norm, w_q, q_norm, sinks, w_o,
           ffn_norm, w_gu, w_down, router_w, router_b, we_gu, we_down, ple_gate_norm, ple_gate_w, ple_w):
    b, s, d = x.shape
    ts, tq, tm = SEQ_TILE, ATTN_TILE, MOE_TILE
    nff = D_FF // FF_CHUNK

    inv = ROPE_THETA ** (-jnp.arange(0, HEAD_DIM, 2, dtype=_F32) / HEAD_DIM)
    ang = jnp.arange(s, dtype=_F32)[:, None] * inv[None, :]
    cos_t = jnp.tile(jnp.cos(ang), (1, LANES // (HEAD_DIM // 2)))
    sin_t = jnp.tile(jnp.concatenate([-jnp.sin(ang), jnp.sin(ang)], axis=1), (1, LANES // HEAD_DIM))
    hid = jnp.arange(LANES) // HEAD_DIM
    hmat = (hid[:, None] == hid[None, :]).astype(_BF)

    wa = w_gu[0][:, :D_FF].reshape(d, nff, FF_CHUNK).transpose(1, 0, 2).astype(_BF)
    wg = w_gu[0][:, D_FF:].reshape(d, nff, FF_CHUNK).transpose(1, 0, 2).astype(_BF)
    wd = w_down[0].reshape(nff, FF_CHUNK, d).astype(_BF)
    q_gain = jnp.tile(q_norm[0].astype(_F32) * (HEAD_DIM ** -0.5), N_HEADS).reshape(1, d)
    k_gain = jnp.tile(k_norm.astype(_F32), N_KV_HEADS).reshape(1, KV_DIM)

    tile3 = lambda w: pl.BlockSpec((None, ts, w), lambda bi, si: (bi, si, 0))
    h1, q, k, v = pl.pallas_call(
        _layer0_kernel,
        grid=(b, s // ts),
        in_specs=[
            tile3(d), tile3(PLE_DIM),
            pl.BlockSpec((ts, LANES), lambda bi, si: (si, 0)),
            pl.BlockSpec((ts, LANES), lambda bi, si: (si, 0)),
            _const((1, d)), _const((len(POOL_WINDOWS), POOL_GROUP, POOL_GROUP)), _const((1, d)),
            _const((1, d)), _const((nff, d, FF_CHUNK)), _const((nff, d, FF_CHUNK)), _const((nff, FF_CHUNK, d)),
            _const((1, d)), _const((d, d)), _const((PLE_DIM, d)),
            _const((1, d)), _const((d, d)), _const((1, d)),
            _const((1, d)), _const((d, 2 * KV_DIM)), _const((1, KV_DIM)), _const((LANES, LANES)),
        ],
        out_specs=[tile3(d), tile3(d), tile3(KV_DIM), tile3(KV_DIM)],
        out_shape=[jax.ShapeDtypeStruct((b, s, d), _F32), jax.ShapeDtypeStruct((b, s, d), _BF),
                   jax.ShapeDtypeStruct((b, s, KV_DIM), _BF), jax.ShapeDtypeStruct((b, s, KV_DIM), _BF)],
        scratch_shapes=[pltpu.VMEM((POOL_HALO + ts, d), _F32)],
        compiler_params=pltpu.CompilerParams(dimension_semantics=("arbitrary", "arbitrary"),
                                             vmem_limit_bytes=VMEM_LIMIT),
        name="layer0",
    )(x, p[0], cos_t, sin_t, _row(pool_norm[0]), pool_w[0].astype(_BF), _row(pool_scale[0]),
      _row(ffn_norm[0]), wa, wg, wd, _row(ple_gate_norm[0]), ple_gate_w[0].astype(_BF), ple_w[0].astype(_BF),
      _row(attn_norm[0]), w_q[0].astype(_BF), q_gain, _row(kv_norm), w_kv.astype(_BF), k_gain, hmat)

    blocks_per_tile = tq // WINDOW
    prev = lambda bi, qi, *_: (bi, jnp.maximum(qi * blocks_per_tile - 1, 0), 0)
    cur = lambda bi, qi, *_: (bi, qi, 0)
    h2 = pl.pallas_call(
        _attn_kernel,
        grid_spec=pltpu.PrefetchScalarGridSpec(
            num_scalar_prefetch=1,
            grid=(b, s // tq),
            in_specs=[
                pl.BlockSpec((None, tq, d), cur),
                pl.BlockSpec((None, WINDOW, KV_DIM), prev), pl.BlockSpec((None, tq, KV_DIM), cur),
                pl.BlockSpec((None, WINDOW, KV_DIM), prev), pl.BlockSpec((None, tq, KV_DIM), cur),
                pl.BlockSpec((None, tq, d), cur),
                pl.BlockSpec((d, d), lambda *_: (0, 0), pipeline_mode=pl.Buffered(1)),
            ],
            out_specs=pl.BlockSpec((None, tq, d), cur),
            scratch_shapes=[pltpu.VMEM((WINDOW + tq, LANES), _BF)] * 4 + [pltpu.VMEM((tq, d), _BF)],
        ),
        out_shape=jax.ShapeDtypeStruct((b, s, d), _F32),
        compiler_params=pltpu.CompilerParams(dimension_semantics=("arbitrary", "arbitrary"),
                                             vmem_limit_bytes=VMEM_LIMIT),
        name="attn",
    )(sinks[0].astype(_F32), q, k, k, v, v, h1, w_o[0].astype(_BF))

    t = b * s
    rw = jnp.pad(router_w[0].astype(_F32), ((0, 0), (0, ROUTER_PAD - N_EXPERTS)))
    rw_hi = rw.astype(_BF)
    rw_lo = (rw - rw_hi.astype(_F32)).astype(_BF)
    rb = jnp.pad(router_b[0].astype(_F32), (0, ROUTER_PAD - N_EXPERTS)).reshape(1, ROUTER_PAD)
    wea = we_gu[0][:, :, :D_FF_EXPERT].astype(_BF)
    weg = we_gu[0][:, :, D_FF_EXPERT:].astype(_BF)
    wed = we_down[0].astype(_BF)

    tok = lambda w: pl.BlockSpec((tm, w), lambda ti, e: (ti, 0))
    expert = lambda r, c: pl.BlockSpec((None, r, c), lambda ti, e: (e, 0, 0))
    out = pl.pallas_call(
        _moe_kernel,
        grid=(t // tm, N_EXPERTS),
        in_specs=[
            tok(d), tok(PLE_DIM), _const((1, d)),
            _const((d, ROUTER_PAD)), _const((d, ROUTER_PAD)), _const((1, ROUTER_PAD)),
            expert(d, D_FF_EXPERT), expert(d, D_FF_EXPERT), expert(D_FF_EXPERT, d),
            _const((1, d)), _const((d, d)), _const((PLE_DIM, d)),
        ],
        out_specs=tok(d),
        out_shape=jax.ShapeDtypeStruct((t, d), _F32),
        scratch_shapes=[pltpu.VMEM((tm, d), _BF), pltpu.VMEM((tm, ROUTER_PAD), _F32), pltpu.VMEM((tm, d), _F32)],
        compiler_params=pltpu.CompilerParams(dimension_semantics=("arbitrary", "arbitrary"),
                                             vmem_limit_bytes=VMEM_LIMIT),
        name="moe",
    )(h2.reshape(t, d), p[1].reshape(t, PLE_DIM), _row(ffn_norm[1]), rw_hi, rw_lo, rb, wea, weg, wed,
      _row(ple_gate_norm[1]), ple_gate_w[1].astype(_BF), ple_w[1].astype(_BF))
    return out.reshape(b, s, d)
```

```python
import jax
import jax.numpy as jnp
from jax import lax
from jax.experimental import pallas as pl
from jax.experimental.pallas import tpu as pltpu

D_MODEL = 1024
PLE_DIM = 256
POOL_WINDOWS = (2, 4, 8, 16)
POOL_GROUP = D_MODEL // len(POOL_WINDOWS)
POOL_HALO = 16
HEAD_DIM = 64
N_HEADS = 16
N_KV_HEADS = 4
KV_DIM = N_KV_HEADS * HEAD_DIM
WINDOW = 128
ROPE_THETA = 10000.0
D_FF = 2816
FF_CHUNK = 256
N_EXPERTS = 8
D_FF_EXPERT = 1024
EPS = 1e-6
LANES = 128
ROUTER_PAD = LANES

SEQ_TILE = 512
ATTN_TILE = 512
ROUTE_TILE = 512
EXPERT_TILE = 512
SEG_ALIGN = 16
SEG_SIZES = tuple(ROUTE_TILE >> i for i in range((ROUTE_TILE // SEG_ALIGN).bit_length()))
SEG_ROWS = 2 * ROUTE_TILE + N_EXPERTS * SEG_ALIGN
GATE_LANES = LANES
SORTED_WIDTH = D_MODEL + GATE_LANES
VMEM_LIMIT = 56 * 1024 * 1024

_BF = jnp.bfloat16
_F32 = jnp.float32


def _rms(x, g):
    return x * lax.rsqrt(jnp.mean(x * x, axis=-1, keepdims=True) + EPS) * g


def _sigmoid(x):
    return 1.0 / (1.0 + jnp.exp(-x))


def _dot(a, b):
    return jnp.dot(a, b, preferred_element_type=_F32)


def _head_sumsq(x, hmat):
    x2 = x * x
    hi = x2.astype(_BF)
    lo = (x2 - hi.astype(_F32)).astype(_BF)
    outs = []
    for s in range(x.shape[1] // LANES):
        sl = slice(s * LANES, (s + 1) * LANES)
        outs.append(_dot(hi[:, sl], hmat) + _dot(lo[:, sl], hmat))
    return jnp.concatenate(outs, axis=1)


def _rope(x, cos, sin_signed):
    n = x.shape[1]
    reps = n // LANES
    lane = lax.broadcasted_iota(jnp.int32, (1, n), 1)
    first_half = (lane % HEAD_DIM) < (HEAD_DIM // 2)
    partner = jnp.where(first_half, pltpu.roll(x, n - HEAD_DIM // 2, 1), pltpu.roll(x, HEAD_DIM // 2, 1))
    c = jnp.concatenate([cos] * reps, axis=1)
    s = jnp.concatenate([sin_signed] * reps, axis=1)
    return x * c + partner * s


def _layer0_kernel(x_ref, p_ref, cos_ref, sin_ref, pool_norm, pool_w, pool_scale, ffn_norm, wa, wg, wd,
                   gate_norm, gate_w, ple_w, attn_norm, wq, q_gain, kv_norm, wkv, k_gain, hmat,
                   h_out, q_out, k_out, v_out, ubuf):
    si = pl.program_id(1)
    ts = x_ref.shape[0]
    x = x_ref[...]

    @pl.when(si == 0)
    def _():
        ubuf[0:POOL_HALO, :] = jnp.zeros((POOL_HALO, D_MODEL), _F32)

    u = _rms(x, pool_norm[...])
    ubuf[POOL_HALO:POOL_HALO + ts, :] = u
    pos = (si * ts + 1 + lax.broadcasted_iota(jnp.int32, (ts, 1), 0)).astype(_F32)
    mixed = []
    for g, win in enumerate(POOL_WINDOWS):
        cols = slice(g * POOL_GROUP, (g + 1) * POOL_GROUP)
        ug = u[:, cols]
        acc = ug
        for k in range(1, win):
            acc = acc + ubuf[POOL_HALO - k:POOL_HALO - k + ts, cols]
        d = acc / jnp.minimum(pos, float(win)) - ug
        mixed.append(_dot(d.astype(_BF), pool_w[g]))
    h = x + jnp.concatenate(mixed, axis=1) * pool_scale[...]
    ubuf[0:POOL_HALO, :] = ubuf[ts:ts + POOL_HALO, :]

    uf = _rms(h, ffn_norm[...]).astype(_BF)

    def ff_body(j, acc):
        a = _dot(uf, wa[j])
        g = _dot(uf, wg[j])
        act = (a * _sigmoid(a) * g).astype(_BF)
        return acc + _dot(act, wd[j])

    h = h + lax.fori_loop(0, wa.shape[0], ff_body, jnp.zeros((ts, D_MODEL), _F32))

    gate = _sigmoid(_dot(_rms(h, gate_norm[...]).astype(_BF), gate_w[...]))
    h = h + gate * _dot(p_ref[...].astype(_BF), ple_w[...])
    h_out[...] = h

    cos = cos_ref[...]
    sin = sin_ref[...]
    q = _dot(_rms(h, attn_norm[...]).astype(_BF), wq[...])
    q = q * lax.rsqrt(_head_sumsq(q, hmat[...]) * (1.0 / HEAD_DIM) + EPS) * q_gain[...]
    q_out[...] = _rope(q, cos, sin).astype(_BF)
    kv = _dot(_rms(h, kv_norm[...]).astype(_BF), wkv[...])
    k = kv[:, :KV_DIM]
    k = k * lax.rsqrt(_head_sumsq(k, hmat[...]) * (1.0 / HEAD_DIM) + EPS) * k_gain[...]
    k_out[...] = _rope(k, cos, sin).astype(_BF)
    v_out[...] = kv[:, KV_DIM:].astype(_BF)


def _attn_kernel(sinks_ref, q_ref, kp_ref, kc_ref, vp_ref, vc_ref, h_ref, wo_ref, out_ref,
                 klo, khi, vlo, vhi, o_scr):
    i = pl.program_id(1)
    tq = q_ref.shape[0]
    nblk = tq // WINDOW
    lane = lax.broadcasted_iota(jnp.int32, (1, LANES), 1)
    low = lane < HEAD_DIM

    qi = lax.broadcasted_iota(jnp.int32, (2 * WINDOW, 2 * WINDOW), 0) % WINDOW
    kj = lax.broadcasted_iota(jnp.int32, (2 * WINDOW, 2 * WINDOW), 1)
    in_window = (kj > qi) & (kj <= qi + WINDOW)
    upper_rows = lax.broadcasted_iota(jnp.int32, (2 * WINDOW, 1), 0) >= WINDOW

    for g in range(N_KV_HEADS):
        slab = slice((g // 2) * LANES, (g // 2 + 1) * LANES)
        for src_p, src_c, lo_ref, hi_ref in ((kp_ref, kc_ref, klo, khi), (vp_ref, vc_ref, vlo, vhi)):
            t = jnp.concatenate([src_p[:, slab], src_c[:, slab]], axis=0).astype(_F32)
            r = pltpu.roll(t, HEAD_DIM, 1)
            in_low, in_high = (t, r) if g % 2 == 0 else (r, t)
            lo_ref[...] = jnp.where(low, in_low, 0.0).astype(_BF)
            hi_ref[...] = jnp.where(low, 0.0, in_high).astype(_BF)

        def blk(j, carry):
            r0 = pl.multiple_of(j * WINDOW, WINDOW)
            qrows = pl.ds(r0, WINDOW)
            krows = pl.ds(r0, 2 * WINDOW)
            qg = jnp.concatenate([q_ref[qrows, (2 * g) * LANES:(2 * g + 1) * LANES],
                                  q_ref[qrows, (2 * g + 1) * LANES:(2 * g + 2) * LANES]], axis=0)
            first_valid_key = jnp.where((i > 0) | (j > 0), 0, WINDOW)
            mask = in_window & (kj >= first_valid_key)
            o = jnp.zeros((2 * WINDOW, LANES), _F32)
            for half, (k_ref, v_ref) in enumerate(((klo, vlo), (khi, vhi))):
                sink = jnp.where(upper_rows, sinks_ref[4 * g + 2 + half], sinks_ref[4 * g + half])
                s = lax.dot_general(qg, k_ref[krows, :], (((1,), (1,)), ((), ())),
                                    preferred_element_type=_F32)
                s = jnp.where(mask, s, -jnp.inf)
                m = jnp.maximum(jnp.max(s, axis=1, keepdims=True), sink)
                pr = jnp.exp(s - m)
                denom = jnp.sum(pr, axis=1, keepdims=True) + jnp.exp(sink - m)
                o = o + _dot(pr.astype(_BF), v_ref[krows, :]) / denom
            o_scr[qrows, (2 * g) * LANES:(2 * g + 1) * LANES] = o[:WINDOW].astype(_BF)
            o_scr[qrows, (2 * g + 1) * LANES:(2 * g + 2) * LANES] = o[WINDOW:].astype(_BF)
            return carry

        lax.fori_loop(0, nblk, blk, 0)

    out_ref[...] = h_ref[...] + _dot(o_scr[...], wo_ref[...])


def _segment_copies(rows, src_ref, src_base, dst_ref, dst_base, sem, sizes=SEG_SIZES):
    out = []
    for sz in sizes:
        done = rows & (-2 * sz)
        src = src_ref.at[pl.ds(pl.multiple_of(src_base + done, SEG_ALIGN), sz)]
        dst = dst_ref.at[pl.ds(pl.multiple_of(dst_base + done, SEG_ALIGN), sz)]
        out.append(((rows & sz) != 0, pltpu.make_async_copy(src, dst, sem)))
    return out


def _start_then_wait(copies):
    for op in ("start", "wait"):
        for cond, cp in copies:
            @pl.when(cond)
            def _(cp=cp, op=op):
                getattr(cp, op)()


def _route_kernel(h_ref, ffn_norm, rw_hi, rw_lo, rb, xs_hbm, meta_ref, seg_ref, tot_ref,
                  comp_scr, zero_scr, cum_ref, sem):
    k = pl.program_id(0)
    bs = h_ref.shape[0]
    cap = xs_hbm.shape[0] // N_EXPERTS
    lane = lax.broadcasted_iota(jnp.int32, (1, ROUTER_PAD), 1)

    @pl.when(k == 0)
    def _():
        for e in range(N_EXPERTS):
            cum_ref[e] = 0

    u = _rms(h_ref[...], ffn_norm[...])
    u_hi = u.astype(_BF)
    u_lo = (u - u_hi.astype(_F32)).astype(_BF)
    logits = _dot(u_hi, rw_hi[...]) + _dot(u_lo, rw_hi[...]) + _dot(u_hi, rw_lo[...]) + rb[...]
    logits = jnp.where(lane < N_EXPERTS, logits, -jnp.inf)
    m1 = jnp.max(logits, axis=1, keepdims=True)
    i1 = jnp.min(jnp.where(logits == m1, lane, ROUTER_PAD), axis=1, keepdims=True)
    rest = jnp.where(lane == i1, -jnp.inf, logits)
    m2 = jnp.max(rest, axis=1, keepdims=True)
    i2 = jnp.min(jnp.where(rest == m2, lane, ROUTER_PAD), axis=1, keepdims=True)
    t = jnp.exp(m2 - m1)
    w1 = 1.0 / (1.0 + t)
    w2 = t * w1

    a1 = lane == i1
    a2 = lane == i2
    assigned = jnp.where(a1 | a2, 1.0, 0.0)
    before = (lax.broadcasted_iota(jnp.int32, (bs, bs), 1) < lax.broadcasted_iota(jnp.int32, (bs, bs), 0))
    rank = _dot(jnp.where(before, 1.0, 0.0).astype(_BF), assigned.astype(_BF))
    count = jnp.sum(assigned, axis=0, keepdims=True).astype(jnp.int32)
    seg_off = jnp.zeros((1, ROUTER_PAD), _F32)
    copies = []
    off = jnp.int32(0)
    for e in range(N_EXPERTS):
        rows = ((count[0, e] + (SEG_ALIGN - 1)) // SEG_ALIGN) * SEG_ALIGN
        seg_off = jnp.where(lane == e, off.astype(_F32), seg_off)
        base = cum_ref[e]
        seg_ref[k * 2 * N_EXPERTS + e] = base
        seg_ref[k * 2 * N_EXPERTS + N_EXPERTS + e] = rows
        copies += _segment_copies(rows, comp_scr, off, xs_hbm, e * cap + base, sem)
        cum_ref[e] = base + rows
        off = off + rows
    dst1 = jnp.sum(jnp.where(a1, rank + seg_off, 0.0), axis=1, keepdims=True)
    dst2 = jnp.sum(jnp.where(a2, rank + seg_off, 0.0), axis=1, keepdims=True)
    meta_ref[...] = jnp.where(lane == 0, dst1, jnp.where(lane == 1, dst2, 0.0))

    pos = lax.broadcasted_iota(jnp.int32, (1, SEG_ROWS), 1).astype(_F32)
    oh1 = jnp.where(pos == dst1, 1.0, 0.0).astype(_BF)
    oh2 = jnp.where(pos == dst2, 1.0, 0.0).astype(_BF)
    tn = (((0,), (0,)), ((), ()))

    def gate_parts(w):
        hi = w.astype(_BF).astype(_F32)
        mid = (w - hi).astype(_BF).astype(_F32)
        lo = w - hi - mid
        return jnp.where(lane == 0, hi, jnp.where(lane == 1, mid, jnp.where(lane == 2, lo, 0.0))).astype(_BF)

    comp_scr[:, :D_MODEL] = lax.dot_general(oh1 + oh2, u_hi, tn, preferred_element_type=_F32).astype(_BF)
    comp_scr[:, D_MODEL:] = (lax.dot_general(oh1, gate_parts(w1), tn, preferred_element_type=_F32)
                             + lax.dot_general(oh2, gate_parts(w2), tn, preferred_element_type=_F32)).astype(_BF)
    _start_then_wait(copies)

    @pl.when(k == pl.num_programs(0) - 1)
    def _():
        zero_scr[...] = jnp.zeros_like(zero_scr)
        fills = []
        for e in range(N_EXPERTS):
            total = cum_ref[e]
            tot_ref[e] = total
            fills += _segment_copies((-total) & (EXPERT_TILE - 1), zero_scr, 0, xs_hbm, e * cap + total, sem,
                                     sizes=SEG_SIZES[1:])
        _start_then_wait(fills)


def _expert_kernel(te_ref, tb_ref, nt_ref, x_ref, wa, wg, wd, y_ref):
    @pl.when(pl.program_id(0) < nt_ref[0])
    def _():
        x = x_ref[:, :D_MODEL]
        gate = jnp.sum(x_ref[:, D_MODEL:].astype(_F32), axis=1, keepdims=True)
        a = _dot(x, wa[...])
        g = _dot(x, wg[...])
        act = (a * _sigmoid(a) * g).astype(_BF)
        y_ref[...] = (gate * _dot(act, wd[...])).astype(_BF)


def _combine_kernel(seg_ref, h_ref, p_ref, meta_ref, ys_hbm, gate_norm, gate_w, ple_w, out_ref, ycat, sem):
    k = pl.program_id(0)
    cap = ys_hbm.shape[0] // N_EXPERTS

    @pl.when(k == 0)
    def _():
        ycat[...] = jnp.zeros_like(ycat)

    copies = []
    off = jnp.int32(0)
    for e in range(N_EXPERTS):
        base = seg_ref[k * 2 * N_EXPERTS + e]
        rows = seg_ref[k * 2 * N_EXPERTS + N_EXPERTS + e]
        copies += _segment_copies(rows, ys_hbm, e * cap + base, ycat, off, sem)
        off = off + rows
    _start_then_wait(copies)

    pos = lax.broadcasted_iota(jnp.int32, (1, SEG_ROWS), 1).astype(_F32)
    meta = meta_ref[...]
    onehot = jnp.where((pos == meta[:, 0:1]) | (pos == meta[:, 1:2]), 1.0, 0.0).astype(_BF)
    h = h_ref[...] + _dot(onehot, ycat[...])
    gate = _sigmoid(_dot(_rms(h, gate_norm[...]).astype(_BF), gate_w[...]))
    out_ref[...] = h + gate * _dot(p_ref[...].astype(_BF), ple_w[...])


def _const(shape):
    return pl.BlockSpec(shape, lambda *_: (0,) * len(shape), pipeline_mode=pl.Buffered(1))


def _row(v):
    return v.reshape(1, -1).astype(_F32)


def kernel(x, p, pool_norm, pool_w, pool_scale, kv_norm, w_kv, k_norm, attn_norm, w_q, q_norm, sinks, w_o,
           ffn_norm, w_gu, w_down, router_w, router_b, we_gu, we_down, ple_gate_norm, ple_gate_w, ple_w):
    b, s, d = x.shape
    ts, tq = SEQ_TILE, ATTN_TILE
    nff = D_FF // FF_CHUNK

    inv = ROPE_THETA ** (-jnp.arange(0, HEAD_DIM, 2, dtype=_F32) / HEAD_DIM)
    ang = jnp.arange(s, dtype=_F32)[:, None] * inv[None, :]
    cos_t = jnp.tile(jnp.cos(ang), (1, LANES // (HEAD_DIM // 2)))
    sin_t = jnp.tile(jnp.concatenate([-jnp.sin(ang), jnp.sin(ang)], axis=1), (1, LANES // HEAD_DIM))
    hid = jnp.arange(LANES) // HEAD_DIM
    hmat = (hid[:, None] == hid[None, :]).astype(_BF)

    wa = w_gu[0][:, :D_FF].reshape(d, nff, FF_CHUNK).transpose(1, 0, 2).astype(_BF)
    wg = w_gu[0][:, D_FF:].reshape(d, nff, FF_CHUNK).transpose(1, 0, 2).astype(_BF)
    wd = w_down[0].reshape(nff, FF_CHUNK, d).astype(_BF)
    q_gain = jnp.tile(q_norm[0].astype(_F32) * (HEAD_DIM ** -0.5), N_HEADS).reshape(1, d)
    k_gain = jnp.tile(k_norm.astype(_F32), N_KV_HEADS).reshape(1, KV_DIM)

    tile3 = lambda w: pl.BlockSpec((None, ts, w), lambda bi, si: (bi, si, 0))
    h1, q, k, v = pl.pallas_call(
        _layer0_kernel,
        grid=(b, s // ts),
        in_specs=[
            tile3(d), tile3(PLE_DIM),
            pl.BlockSpec((ts, LANES), lambda bi, si: (si, 0)),
            pl.BlockSpec((ts, LANES), lambda bi, si: (si, 0)),
            _const((1, d)), _const((len(POOL_WINDOWS), POOL_GROUP, POOL_GROUP)), _const((1, d)),
            _const((1, d)), _const((nff, d, FF_CHUNK)), _const((nff, d, FF_CHUNK)), _const((nff, FF_CHUNK, d)),
            _const((1, d)), _const((d, d)), _const((PLE_DIM, d)),
            _const((1, d)), _const((d, d)), _const((1, d)),
            _const((1, d)), _const((d, 2 * KV_DIM)), _const((1, KV_DIM)), _const((LANES, LANES)),
        ],
        out_specs=[tile3(d), tile3(d), tile3(KV_DIM), tile3(KV_DIM)],
        out_shape=[jax.ShapeDtypeStruct((b, s, d), _F32), jax.ShapeDtypeStruct((b, s, d), _BF),
                   jax.ShapeDtypeStruct((b, s, KV_DIM), _BF), jax.ShapeDtypeStruct((b, s, KV_DIM), _BF)],
        scratch_shapes=[pltpu.VMEM((POOL_HALO + ts, d), _F32)],
        compiler_params=pltpu.CompilerParams(dimension_semantics=("arbitrary", "arbitrary"),
                                             vmem_limit_bytes=VMEM_LIMIT),
        name="layer0",
    )(x, p[0], cos_t, sin_t, _row(pool_norm[0]), pool_w[0].astype(_BF), _row(pool_scale[0]),
      _row(ffn_norm[0]), wa, wg, wd, _row(ple_gate_norm[0]), ple_gate_w[0].astype(_BF), ple_w[0].astype(_BF),
      _row(attn_norm[0]), w_q[0].astype(_BF), q_gain, _row(kv_norm), w_kv.astype(_BF), k_gain, hmat)

    blocks_per_tile = tq // WINDOW
    prev = lambda bi, qi, *_: (bi, jnp.maximum(qi * blocks_per_tile - 1, 0), 0)
    cur = lambda bi, qi, *_: (bi, qi, 0)
    h2 = pl.pallas_call(
        _attn_kernel,
        grid_spec=pltpu.PrefetchScalarGridSpec(
            num_scalar_prefetch=1,
            grid=(b, s // tq),
            in_specs=[
                pl.BlockSpec((None, tq, d), cur),
                pl.BlockSpec((None, WINDOW, KV_DIM), prev), pl.BlockSpec((None, tq, KV_DIM), cur),
                pl.BlockSpec((None, WINDOW, KV_DIM), prev), pl.BlockSpec((None, tq, KV_DIM), cur),
                pl.BlockSpec((None, tq, d), cur),
                pl.BlockSpec((d, d), lambda *_: (0, 0), pipeline_mode=pl.Buffered(1)),
            ],
            out_specs=pl.BlockSpec((None, tq, d), cur),
            scratch_shapes=[pltpu.VMEM((WINDOW + tq, LANES), _BF)] * 4 + [pltpu.VMEM((tq, d), _BF)],
        ),
        out_shape=jax.ShapeDtypeStruct((b, s, d), _F32),
        compiler_params=pltpu.CompilerParams(dimension_semantics=("arbitrary", "arbitrary"),
                                             vmem_limit_bytes=VMEM_LIMIT),
        name="attn",
    )(sinks[0].astype(_F32), q, k, k, v, v, h1, w_o[0].astype(_BF))

    t = b * s
    rw = jnp.pad(router_w[0].astype(_F32), ((0, 0), (0, ROUTER_PAD - N_EXPERTS)))
    rw_hi = rw.astype(_BF)
    rw_lo = (rw - rw_hi.astype(_F32)).astype(_BF)
    rb = jnp.pad(router_b[0].astype(_F32), (0, ROUTER_PAD - N_EXPERTS)).reshape(1, ROUTER_PAD)
    wea = we_gu[0][:, :, :D_FF_EXPERT].astype(_BF)
    weg = we_gu[0][:, :, D_FF_EXPERT:].astype(_BF)
    wed = we_down[0].astype(_BF)

    bs, te = ROUTE_TILE, EXPERT_TILE
    nblk = t // bs
    cap = t
    smem = pl.BlockSpec(memory_space=pltpu.SMEM)
    tok = lambda w: pl.BlockSpec((bs, w), lambda ki, *_: (ki, 0))
    xs, meta, seg, totals = pl.pallas_call(
        _route_kernel,
        grid=(nblk,),
        in_specs=[tok(d), _const((1, d)), _const((d, ROUTER_PAD)), _const((d, ROUTER_PAD)), _const((1, ROUTER_PAD))],
        out_specs=[pl.BlockSpec(memory_space=pl.ANY), tok(ROUTER_PAD), smem, smem],
        out_shape=[jax.ShapeDtypeStruct((N_EXPERTS * cap, SORTED_WIDTH), _BF),
                   jax.ShapeDtypeStruct((t, ROUTER_PAD), _F32),
                   jax.ShapeDtypeStruct((nblk * 2 * N_EXPERTS,), jnp.int32),
                   jax.ShapeDtypeStruct((N_EXPERTS,), jnp.int32)],
        scratch_shapes=[pltpu.VMEM((SEG_ROWS, SORTED_WIDTH), _BF), pltpu.VMEM((EXPERT_TILE, SORTED_WIDTH), _BF),
                        pltpu.SMEM((N_EXPERTS,), jnp.int32), pltpu.SemaphoreType.DMA(())],
        compiler_params=pltpu.CompilerParams(dimension_semantics=("arbitrary",), vmem_limit_bytes=VMEM_LIMIT),
        name="route",
    )(h2.reshape(t, d), _row(ffn_norm[1]), rw_hi, rw_lo, rb)

    n_steps = (2 * t + nblk * N_EXPERTS * (SEG_ALIGN - 1)) // te + N_EXPERTS
    tiles_e = (totals + te - 1) // te
    tile_end = jnp.cumsum(tiles_e)
    n_tiles = tile_end[-1]
    step = jnp.minimum(jnp.arange(n_steps, dtype=jnp.int32), n_tiles - 1)
    tile_expert = jnp.sum(step[:, None] >= tile_end[None, :], axis=1).astype(jnp.int32)
    tile_block = (tile_expert * (cap // te) + step - (tile_end - tiles_e)[tile_expert]).astype(jnp.int32)

    xrow = lambda w: pl.BlockSpec((te, w), lambda i, te_r, tb_r, nt_r: (tb_r[i], 0))
    wexp = lambda r, c: pl.BlockSpec((None, r, c), lambda i, te_r, tb_r, nt_r: (te_r[i], 0, 0))
    ys = pl.pallas_call(
        _expert_kernel,
        grid_spec=pltpu.PrefetchScalarGridSpec(
            num_scalar_prefetch=3,
            grid=(n_steps,),
            in_specs=[xrow(SORTED_WIDTH), wexp(d, D_FF_EXPERT), wexp(d, D_FF_EXPERT), wexp(D_FF_EXPERT, d)],
            out_specs=xrow(d),
        ),
        out_shape=jax.ShapeDtypeStruct((N_EXPERTS * cap, d), _BF),
        compiler_params=pltpu.CompilerParams(dimension_semantics=("arbitrary",), vmem_limit_bytes=VMEM_LIMIT),
        name="experts",
    )(tile_expert, tile_block, n_tiles.reshape(1).astype(jnp.int32), xs, wea, weg, wed)

    out = pl.pallas_call(
        _combine_kernel,
        grid_spec=pltpu.PrefetchScalarGridSpec(
            num_scalar_prefetch=1,
            grid=(nblk,),
            in_specs=[tok(d), tok(PLE_DIM), tok(ROUTER_PAD), pl.BlockSpec(memory_space=pl.ANY),
                      _const((1, d)), _const((d, d)), _const((PLE_DIM, d))],
            out_specs=tok(d),
            scratch_shapes=[pltpu.VMEM((SEG_ROWS, d), _BF), pltpu.SemaphoreType.DMA(())],
        ),
        out_shape=jax.ShapeDtypeStruct((t, d), _F32),
        compiler_params=pltpu.CompilerParams(dimension_semantics=("arbitrary",), vmem_limit_bytes=VMEM_LIMIT),
        name="combine",
    )(seg, h2.reshape(t, d), p[1].reshape(t, PLE_DIM), meta, ys, _row(ple_gate_norm[1]),
      ple_gate_w[1].astype(_BF), ple_w[1].astype(_BF))
    return out.reshape(b, s, d)
```

```python
import jax
import jax.numpy as jnp
from jax import lax
from jax.experimental import pallas as pl
from jax.experimental.pallas import tpu as pltpu

D_MODEL = 1024
PLE_DIM = 256
POOL_WINDOWS = (2, 4, 8, 16)
POOL_GROUP = D_MODEL // len(POOL_WINDOWS)
POOL_HALO = 16
HEAD_DIM = 64
N_HEADS = 16
N_KV_HEADS = 4
KV_DIM = N_KV_HEADS * HEAD_DIM
WINDOW = 128
ROPE_THETA = 10000.0
D_FF = 2816
FF_CHUNK = 256
N_EXPERTS = 8
D_FF_EXPERT = 1024
EPS = 1e-6
LANES = 128
ROUTER_PAD = LANES

SEQ_TILE = 512
ATTN_TILE = 512
ROUTE_TILE = 512
EXPERT_TILE = 512
SEG_ALIGN = 16
SEG_SIZES = tuple(ROUTE_TILE >> i for i in range((ROUTE_TILE // SEG_ALIGN).bit_length()))
SEG_ROWS = 2 * ROUTE_TILE + N_EXPERTS * SEG_ALIGN
GATE_LANES = LANES
SORTED_WIDTH = D_MODEL + GATE_LANES
VMEM_LIMIT = 56 * 1024 * 1024

_BF = jnp.bfloat16
_F32 = jnp.float32


def _rms(x, g):
    return x * lax.rsqrt(jnp.mean(x * x, axis=-1, keepdims=True) + EPS) * g


def _sigmoid(x):
    return 1.0 / (1.0 + jnp.exp(-x))


def _dot(a, b):
    return jnp.dot(a, b, preferred_element_type=_F32)


def _head_sumsq(x, hmat):
    x2 = x * x
    hi = x2.astype(_BF)
    lo = (x2 - hi.astype(_F32)).astype(_BF)
    outs = []
    for s in range(x.shape[1] // LANES):
        sl = slice(s * LANES, (s + 1) * LANES)
        outs.append(_dot(hi[:, sl], hmat) + _dot(lo[:, sl], hmat))
    return jnp.concatenate(outs, axis=1)


def _rope(x, cos, sin_signed):
    n = x.shape[1]
    reps = n // LANES
    lane = lax.broadcasted_iota(jnp.int32, (1, n), 1)
    first_half = (lane % HEAD_DIM) < (HEAD_DIM // 2)
    partner = jnp.where(first_half, pltpu.roll(x, n - HEAD_DIM // 2, 1), pltpu.roll(x, HEAD_DIM // 2, 1))
    c = jnp.concatenate([cos] * reps, axis=1)
    s = jnp.concatenate([sin_signed] * reps, axis=1)
    return x * c + partner * s


def _layer0_kernel(x_ref, p_ref, cos_ref, sin_ref, pool_norm, pool_w, pool_scale, ffn_norm, wgu, wd,
                   gate_norm, gate_w, ple_w, attn_norm, wq, q_gain, kv_norm, wkv, k_gain, hmat,
                   h_out, q_out, k_out, v_out, ubuf):
    si = pl.program_id(1)
    ts = x_ref.shape[0]
    x = x_ref[...]

    @pl.when(si == 0)
    def _():
        ubuf[0:POOL_HALO, :] = jnp.zeros((POOL_HALO, D_MODEL), _F32)

    u = _rms(x, pool_norm[...])
    ubuf[POOL_HALO:POOL_HALO + ts, :] = u
    pos = (si * ts + 1 + lax.broadcasted_iota(jnp.int32, (ts, 1), 0)).astype(_F32)
    mixed = []
    for g, win in enumerate(POOL_WINDOWS):
        cols = slice(g * POOL_GROUP, (g + 1) * POOL_GROUP)
        ug = u[:, cols]
        acc = ug
        for k in range(1, win):
            acc = acc + ubuf[POOL_HALO - k:POOL_HALO - k + ts, cols]
        d = acc / jnp.minimum(pos, float(win)) - ug
        mixed.append(_dot(d.astype(_BF), pool_w[g]))
    h = x + jnp.concatenate(mixed, axis=1) * pool_scale[...]
    ubuf[0:POOL_HALO, :] = ubuf[ts:ts + POOL_HALO, :]

    uf = _rms(h, ffn_norm[...]).astype(_BF)

    acc = jnp.zeros((ts, D_MODEL), _F32)
    for c0 in range(0, D_FF, FF_CHUNK):
        a = _dot(uf, wgu[:, c0:c0 + FF_CHUNK])
        g = _dot(uf, wgu[:, D_FF + c0:D_FF + c0 + FF_CHUNK])
        act = (a * _sigmoid(a) * g).astype(_BF)
        acc = acc + _dot(act, wd[c0:c0 + FF_CHUNK, :])
    h = h + acc

    gate = _sigmoid(_dot(_rms(h, gate_norm[...]).astype(_BF), gate_w[...]))
    h = h + gate * _dot(p_ref[...].astype(_BF), ple_w[...])
    h_out[...] = h

    cos = cos_ref[...]
    sin = sin_ref[...]
    q = _dot(_rms(h, attn_norm[...]).astype(_BF), wq[...])
    q = q * lax.rsqrt(_head_sumsq(q, hmat[...]) * (1.0 / HEAD_DIM) + EPS) * q_gain[...]
    q_out[...] = _rope(q, cos, sin).astype(_BF)
    kv = _dot(_rms(h, kv_norm[...]).astype(_BF), wkv[...])
    k = kv[:, :KV_DIM]
    k = k * lax.rsqrt(_head_sumsq(k, hmat[...]) * (1.0 / HEAD_DIM) + EPS) * k_gain[...]
    k_out[...] = _rope(k, cos, sin).astype(_BF)
    v_out[...] = kv[:, KV_DIM:].astype(_BF)


def _attn_kernel(sinks_ref, q_ref, kp_ref, kc_ref, vp_ref, vc_ref, h_ref, wo_ref, out_ref,
                 klo, khi, vlo, vhi, o_scr):
    i = pl.program_id(1)
    tq = q_ref.shape[0]
    nblk = tq // WINDOW
    lane = lax.broadcasted_iota(jnp.int32, (1, LANES), 1)
    low = lane < HEAD_DIM

    qi = lax.broadcasted_iota(jnp.int32, (2 * WINDOW, 2 * WINDOW), 0) % WINDOW
    kj = lax.broadcasted_iota(jnp.int32, (2 * WINDOW, 2 * WINDOW), 1)
    in_window = (kj > qi) & (kj <= qi + WINDOW)
    upper_rows = lax.broadcasted_iota(jnp.int32, (2 * WINDOW, 1), 0) >= WINDOW

    for g in range(N_KV_HEADS):
        slab = slice((g // 2) * LANES, (g // 2 + 1) * LANES)
        for src_p, src_c, lo_ref, hi_ref in ((kp_ref, kc_ref, klo, khi), (vp_ref, vc_ref, vlo, vhi)):
            t = jnp.concatenate([src_p[:, slab], src_c[:, slab]], axis=0).astype(_F32)
            r = pltpu.roll(t, HEAD_DIM, 1)
            in_low, in_high = (t, r) if g % 2 == 0 else (r, t)
            lo_ref[...] = jnp.where(low, in_low, 0.0).astype(_BF)
            hi_ref[...] = jnp.where(low, 0.0, in_high).astype(_BF)

        for j in range(nblk):
            qrows = pl.ds(j * WINDOW, WINDOW)
            krows = pl.ds(j * WINDOW, 2 * WINDOW)
            qg = jnp.concatenate([q_ref[qrows, (2 * g) * LANES:(2 * g + 1) * LANES],
                                  q_ref[qrows, (2 * g + 1) * LANES:(2 * g + 2) * LANES]], axis=0)
            mask = in_window & (kj >= jnp.where(i > 0, 0, WINDOW)) if j == 0 else in_window
            o = jnp.zeros((2 * WINDOW, LANES), _F32)
            for half, (k_ref, v_ref) in enumerate(((klo, vlo), (khi, vhi))):
                sink = jnp.where(upper_rows, sinks_ref[4 * g + 2 + half], sinks_ref[4 * g + half])
                s = lax.dot_general(qg, k_ref[krows, :], (((1,), (1,)), ((), ())),
                                    preferred_element_type=_F32)
                s = jnp.where(mask, s, -jnp.inf)
                m = jnp.maximum(jnp.max(s, axis=1, keepdims=True), sink)
                pr = jnp.exp(s - m)
                denom = jnp.sum(pr, axis=1, keepdims=True) + jnp.exp(sink - m)
                o = o + _dot(pr.astype(_BF), v_ref[krows, :]) / denom
            o_scr[qrows, (2 * g) * LANES:(2 * g + 1) * LANES] = o[:WINDOW].astype(_BF)
            o_scr[qrows, (2 * g + 1) * LANES:(2 * g + 2) * LANES] = o[WINDOW:].astype(_BF)

    out_ref[...] = h_ref[...] + _dot(o_scr[...], wo_ref[...])


def _segment_copies(rows, src_ref, src_base, dst_ref, dst_base, sem, sizes=SEG_SIZES):
    out = []
    for sz in sizes:
        done = rows & (-2 * sz)
        src = src_ref.at[pl.ds(pl.multiple_of(src_base + done, SEG_ALIGN), sz)]
        dst = dst_ref.at[pl.ds(pl.multiple_of(dst_base + done, SEG_ALIGN), sz)]
        out.append(((rows & sz) != 0, pltpu.make_async_copy(src, dst, sem)))
    return out


def _run(copies, op):
    for cond, cp in copies:
        @pl.when(cond)
        def _(cp=cp):
            getattr(cp, op)()


def _block_copies(seg_ref, blk, hbm_ref, buf_ref, sem, to_hbm):
    cap = hbm_ref.shape[0] // N_EXPERTS
    copies = []
    off = jnp.int32(0)
    for e in range(N_EXPERTS):
        base = e * cap + seg_ref[blk * 2 * N_EXPERTS + e]
        rows = seg_ref[blk * 2 * N_EXPERTS + N_EXPERTS + e]
        if to_hbm:
            copies += _segment_copies(rows, buf_ref, off, hbm_ref, base, sem)
        else:
            copies += _segment_copies(rows, hbm_ref, base, buf_ref, off, sem)
        off = off + rows
    return copies


def _route_kernel(h_ref, ffn_norm, rw_hi, rw_lo, rb, xs_hbm, meta_ref, seg_ref, tot_ref,
                  comp_scr, zero_scr, cum_ref, sem):
    k = pl.program_id(0)
    bs = h_ref.shape[0]
    cap = xs_hbm.shape[0] // N_EXPERTS
    lane = lax.broadcasted_iota(jnp.int32, (1, ROUTER_PAD), 1)

    @pl.when(k == 0)
    def _():
        for e in range(N_EXPERTS):
            cum_ref[e] = 0

    u = _rms(h_ref[...], ffn_norm[...])
    u_hi = u.astype(_BF)
    u_lo = (u - u_hi.astype(_F32)).astype(_BF)
    logits = _dot(u_hi, rw_hi[...]) + _dot(u_lo, rw_hi[...]) + _dot(u_hi, rw_lo[...]) + rb[...]
    logits = jnp.where(lane < N_EXPERTS, logits, -jnp.inf)
    m1 = jnp.max(logits, axis=1, keepdims=True)
    i1 = jnp.min(jnp.where(logits == m1, lane, ROUTER_PAD), axis=1, keepdims=True)
    rest = jnp.where(lane == i1, -jnp.inf, logits)
    m2 = jnp.max(rest, axis=1, keepdims=True)
    i2 = jnp.min(jnp.where(rest == m2, lane, ROUTER_PAD), axis=1, keepdims=True)
    t = jnp.exp(m2 - m1)
    w1 = 1.0 / (1.0 + t)
    w2 = t * w1

    a1 = lane == i1
    a2 = lane == i2
    assigned = jnp.where(a1 | a2, 1.0, 0.0)
    before = (lax.broadcasted_iota(jnp.int32, (bs, bs), 1) < lax.broadcasted_iota(jnp.int32, (bs, bs), 0))
    rank = _dot(jnp.where(before, 1.0, 0.0).astype(_BF), assigned.astype(_BF))
    count = jnp.sum(assigned, axis=0, keepdims=True).astype(jnp.int32)
    seg_off = jnp.zeros((1, ROUTER_PAD), _F32)
    off = jnp.int32(0)
    for e in range(N_EXPERTS):
        rows = ((count[0, e] + (SEG_ALIGN - 1)) // SEG_ALIGN) * SEG_ALIGN
        seg_off = jnp.where(lane == e, off.astype(_F32), seg_off)
        base = cum_ref[e]
        seg_ref[k * 2 * N_EXPERTS + e] = base
        seg_ref[k * 2 * N_EXPERTS + N_EXPERTS + e] = rows
        cum_ref[e] = base + rows
        off = off + rows
    dst1 = jnp.sum(jnp.where(a1, rank + seg_off, 0.0), axis=1, keepdims=True)
    dst2 = jnp.sum(jnp.where(a2, rank + seg_off, 0.0), axis=1, keepdims=True)
    meta_ref[...] = jnp.where(lane == 0, dst1, jnp.where(lane == 1, dst2, 0.0))

    pos = lax.broadcasted_iota(jnp.int32, (1, SEG_ROWS), 1).astype(_F32)
    oh1 = jnp.where(pos == dst1, 1.0, 0.0).astype(_BF)
    oh2 = jnp.where(pos == dst2, 1.0, 0.0).astype(_BF)
    tn = (((0,), (0,)), ((), ()))

    def gate_parts(w):
        hi = w.astype(_BF).astype(_F32)
        mid = (w - hi).astype(_BF).astype(_F32)
        lo = w - hi - mid
        return jnp.where(lane == 0, hi, jnp.where(lane == 1, mid, jnp.where(lane == 2, lo, 0.0))).astype(_BF)

    slot = k % 2
    comp = comp_scr.at[slot]
    comp[:, :D_MODEL] = lax.dot_general(oh1 + oh2, u_hi, tn, preferred_element_type=_F32).astype(_BF)
    comp[:, D_MODEL:] = (lax.dot_general(oh1, gate_parts(w1), tn, preferred_element_type=_F32)
                         + lax.dot_general(oh2, gate_parts(w2), tn, preferred_element_type=_F32)).astype(_BF)
    _run(_block_copies(seg_ref, k, xs_hbm, comp, sem.at[slot], to_hbm=True), "start")

    @pl.when(k > 0)
    def _():
        _run(_block_copies(seg_ref, k - 1, xs_hbm, comp_scr.at[1 - slot], sem.at[1 - slot], to_hbm=True), "wait")

    @pl.when(k == pl.num_programs(0) - 1)
    def _():
        _run(_block_copies(seg_ref, k, xs_hbm, comp, sem.at[slot], to_hbm=True), "wait")
        zero_scr[...] = jnp.zeros_like(zero_scr)
        fills = []
        for e in range(N_EXPERTS):
            total = cum_ref[e]
            tot_ref[e] = total
            fills += _segment_copies((-total) & (EXPERT_TILE - 1), zero_scr, 0, xs_hbm, e * cap + total,
                                     sem.at[slot], sizes=SEG_SIZES[1:])
        _run(fills, "start")
        _run(fills, "wait")


def _expert_kernel(te_ref, tb_ref, nt_ref, x_ref, wa, wg, wd, y_ref):
    @pl.when(pl.program_id(0) < nt_ref[0])
    def _():
        x = x_ref[:, :D_MODEL]
        gate = jnp.sum(x_ref[:, D_MODEL:].astype(_F32), axis=1, keepdims=True)
        a = _dot(x, wa[...])
        g = _dot(x, wg[...])
        act = (a * _sigmoid(a) * g).astype(_BF)
        y_ref[...] = (gate * _dot(act, wd[...])).astype(_BF)


def _combine_kernel(seg_ref, h_ref, p_ref, meta_ref, ys_hbm, gate_norm, gate_w, ple_w, out_ref, ycat, sem):
    k = pl.program_id(0)
    slot = k % 2
    fetch = lambda blk, s, op: _run(_block_copies(seg_ref, blk, ys_hbm, ycat.at[s], sem.at[s], to_hbm=False), op)

    @pl.when(k == 0)
    def _():
        ycat[...] = jnp.zeros_like(ycat)
        fetch(k, slot, "start")

    @pl.when(k + 1 < pl.num_programs(0))
    def _():
        fetch(k + 1, 1 - slot, "start")

    fetch(k, slot, "wait")

    pos = lax.broadcasted_iota(jnp.int32, (1, SEG_ROWS), 1).astype(_F32)
    meta = meta_ref[...]
    onehot = jnp.where((pos == meta[:, 0:1]) | (pos == meta[:, 1:2]), 1.0, 0.0).astype(_BF)
    h = h_ref[...] + _dot(onehot, ycat[slot])
    gate = _sigmoid(_dot(_rms(h, gate_norm[...]).astype(_BF), gate_w[...]))
    out_ref[...] = h + gate * _dot(p_ref[...].astype(_BF), ple_w[...])


def _const(shape):
    return pl.BlockSpec(shape, lambda *_: (0,) * len(shape), pipeline_mode=pl.Buffered(1))


def _row(v):
    return v.reshape(1, -1).astype(_F32)


def kernel(x, p, pool_norm, pool_w, pool_scale, kv_norm, w_kv, k_norm, attn_norm, w_q, q_norm, sinks, w_o,
           ffn_norm, w_gu, w_down, router_w, router_b, we_gu, we_down, ple_gate_norm, ple_gate_w, ple_w):
    b, s, d = x.shape
    ts, tq = SEQ_TILE, ATTN_TILE

    inv = ROPE_THETA ** (-jnp.arange(0, HEAD_DIM, 2, dtype=_F32) / HEAD_DIM)
    ang = jnp.arange(s, dtype=_F32)[:, None] * inv[None, :]
    cos_t = jnp.tile(jnp.cos(ang), (1, LANES // (HEAD_DIM // 2)))
    sin_t = jnp.tile(jnp.concatenate([-jnp.sin(ang), jnp.sin(ang)], axis=1), (1, LANES // HEAD_DIM))
    hid = jnp.arange(LANES) // HEAD_DIM
    hmat = (hid[:, None] == hid[None, :]).astype(_BF)

    q_gain = jnp.tile(q_norm[0].astype(_F32) * (HEAD_DIM ** -0.5), N_HEADS).reshape(1, d)
    k_gain = jnp.tile(k_norm.astype(_F32), N_KV_HEADS).reshape(1, KV_DIM)

    tile3 = lambda w: pl.BlockSpec((None, ts, w), lambda bi, si: (bi, si, 0))
    h1, q, k, v = pl.pallas_call(
        _layer0_kernel,
        grid=(b, s // ts),
        in_specs=[
            tile3(d), tile3(PLE_DIM),
            pl.BlockSpec((ts, LANES), lambda bi, si: (si, 0)),
            pl.BlockSpec((ts, LANES), lambda bi, si: (si, 0)),
            _const((1, d)), _const((len(POOL_WINDOWS), POOL_GROUP, POOL_GROUP)), _const((1, d)),
            _const((1, d)), _const((d, 2 * D_FF)), _const((D_FF, d)),
            _const((1, d)), _const((d, d)), _const((PLE_DIM, d)),
            _const((1, d)), _const((d, d)), _const((1, d)),
            _const((1, d)), _const((d, 2 * KV_DIM)), _const((1, KV_DIM)), _const((LANES, LANES)),
        ],
        out_specs=[tile3(d), tile3(d), tile3(KV_DIM), tile3(KV_DIM)],
        out_shape=[jax.ShapeDtypeStruct((b, s, d), _F32), jax.ShapeDtypeStruct((b, s, d), _BF),
                   jax.ShapeDtypeStruct((b, s, KV_DIM), _BF), jax.ShapeDtypeStruct((b, s, KV_DIM), _BF)],
        scratch_shapes=[pltpu.VMEM((POOL_HALO + ts, d), _F32)],
        compiler_params=pltpu.CompilerParams(dimension_semantics=("arbitrary", "arbitrary"),
                                             vmem_limit_bytes=VMEM_LIMIT),
        name="layer0",
    )(x, p[0], cos_t, sin_t, _row(pool_norm[0]), pool_w[0].astype(_BF), _row(pool_scale[0]),
      _row(ffn_norm[0]), w_gu[0].astype(_BF), w_down[0].astype(_BF), _row(ple_gate_norm[0]), ple_gate_w[0].astype(_BF), ple_w[0].astype(_BF),
      _row(attn_norm[0]), w_q[0].astype(_BF), q_gain, _row(kv_norm), w_kv.astype(_BF), k_gain, hmat)

    blocks_per_tile = tq // WINDOW
    prev = lambda bi, qi, *_: (bi, jnp.maximum(qi * blocks_per_tile - 1, 0), 0)
    cur = lambda bi, qi, *_: (bi, qi, 0)
    h2 = pl.pallas_call(
        _attn_kernel,
        grid_spec=pltpu.PrefetchScalarGridSpec(
            num_scalar_prefetch=1,
            grid=(b, s // tq),
            in_specs=[
                pl.BlockSpec((None, tq, d), cur),
                pl.BlockSpec((None, WINDOW, KV_DIM), prev), pl.BlockSpec((None, tq, KV_DIM), cur),
                pl.BlockSpec((None, WINDOW, KV_DIM), prev), pl.BlockSpec((None, tq, KV_DIM), cur),
                pl.BlockSpec((None, tq, d), cur),
                pl.BlockSpec((d, d), lambda *_: (0, 0), pipeline_mode=pl.Buffered(1)),
            ],
            out_specs=pl.BlockSpec((None, tq, d), cur),
            scratch_shapes=[pltpu.VMEM((WINDOW + tq, LANES), _BF)] * 4 + [pltpu.VMEM((tq, d), _BF)],
        ),
        out_shape=jax.ShapeDtypeStruct((b, s, d), _F32),
        compiler_params=pltpu.CompilerParams(dimension_semantics=("arbitrary", "arbitrary"),
                                             vmem_limit_bytes=VMEM_LIMIT),
        name="attn",
    )(sinks[0].astype(_F32), q, k, k, v, v, h1, w_o[0].astype(_BF))

    t = b * s
    rw = jnp.pad(router_w[0].astype(_F32), ((0, 0), (0, ROUTER_PAD - N_EXPERTS)))
    rw_hi = rw.astype(_BF)
    rw_lo = (rw - rw_hi.astype(_F32)).astype(_BF)
    rb = jnp.pad(router_b[0].astype(_F32), (0, ROUTER_PAD - N_EXPERTS)).reshape(1, ROUTER_PAD)
    wegu = we_gu[0].astype(_BF)
    wed = we_down[0].astype(_BF)

    bs, te = ROUTE_TILE, EXPERT_TILE
    nblk = t // bs
    cap = t
    smem = pl.BlockSpec(memory_space=pltpu.SMEM)
    tok = lambda w: pl.BlockSpec((bs, w), lambda ki, *_: (ki, 0))
    xs, meta, seg, totals = pl.pallas_call(
        _route_kernel,
        grid=(nblk,),
        in_specs=[tok(d), _const((1, d)), _const((d, ROUTER_PAD)), _const((d, ROUTER_PAD)), _const((1, ROUTER_PAD))],
        out_specs=[pl.BlockSpec(memory_space=pl.ANY), tok(ROUTER_PAD), smem, smem],
        out_shape=[jax.ShapeDtypeStruct((N_EXPERTS * cap, SORTED_WIDTH), _BF),
                   jax.ShapeDtypeStruct((t, ROUTER_PAD), _F32),
                   jax.ShapeDtypeStruct((nblk * 2 * N_EXPERTS,), jnp.int32),
                   jax.ShapeDtypeStruct((N_EXPERTS,), jnp.int32)],
        scratch_shapes=[pltpu.VMEM((2, SEG_ROWS, SORTED_WIDTH), _BF), pltpu.VMEM((EXPERT_TILE, SORTED_WIDTH), _BF),
                        pltpu.SMEM((N_EXPERTS,), jnp.int32), pltpu.SemaphoreType.DMA((2,))],
        compiler_params=pltpu.CompilerParams(dimension_semantics=("arbitrary",), vmem_limit_bytes=VMEM_LIMIT),
        name="route",
    )(h2.reshape(t, d), _row(ffn_norm[1]), rw_hi, rw_lo, rb)

    n_steps = (2 * t + nblk * N_EXPERTS * (SEG_ALIGN - 1)) // te + N_EXPERTS
    tiles_e = (totals + te - 1) // te
    tile_end = jnp.cumsum(tiles_e)
    n_tiles = tile_end[-1]
    step = jnp.minimum(jnp.arange(n_steps, dtype=jnp.int32), n_tiles - 1)
    tile_expert = jnp.sum(step[:, None] >= tile_end[None, :], axis=1).astype(jnp.int32)
    tile_block = (tile_expert * (cap // te) + step - (tile_end - tiles_e)[tile_expert]).astype(jnp.int32)

    xrow = lambda w: pl.BlockSpec((te, w), lambda i, te_r, tb_r, nt_r: (tb_r[i], 0))
    wexp = lambda r, c, cb: pl.BlockSpec((None, r, c), lambda i, te_r, tb_r, nt_r: (te_r[i], 0, cb))
    ys = pl.pallas_call(
        _expert_kernel,
        grid_spec=pltpu.PrefetchScalarGridSpec(
            num_scalar_prefetch=3,
            grid=(n_steps,),
            in_specs=[xrow(SORTED_WIDTH), wexp(d, D_FF_EXPERT, 0), wexp(d, D_FF_EXPERT, 1), wexp(D_FF_EXPERT, d, 0)],
            out_specs=xrow(d),
        ),
        out_shape=jax.ShapeDtypeStruct((N_EXPERTS * cap, d), _BF),
        compiler_params=pltpu.CompilerParams(dimension_semantics=("arbitrary",), vmem_limit_bytes=VMEM_LIMIT),
        name="experts",
    )(tile_expert, tile_block, n_tiles.reshape(1).astype(jnp.int32), xs, wegu, wegu, wed)

    out = pl.pallas_call(
        _combine_kernel,
        grid_spec=pltpu.PrefetchScalarGridSpec(
            num_scalar_prefetch=1,
            grid=(nblk,),
            in_specs=[tok(d), tok(PLE_DIM), tok(ROUTER_PAD), pl.BlockSpec(memory_space=pl.ANY),
                      _const((1, d)), _const((d, d)), _const((PLE_DIM, d))],
            out_specs=tok(d),
            scratch_shapes=[pltpu.VMEM((2, SEG_ROWS, d), _BF), pltpu.SemaphoreType.DMA((2,))],
        ),
        out_shape=jax.ShapeDtypeStruct((t, d), _F32),
        compiler_params=pltpu.CompilerParams(dimension_semantics=("arbitrary",), vmem_limit_bytes=VMEM_LIMIT),
        name="combine",
    )(seg, h2.reshape(t, d), p[1].reshape(t, PLE_DIM), meta, ys, _row(ple_gate_norm[1]),
      ple_gate_w[1].astype(_BF), ple_w[1].astype(_BF))
    return out.reshape(b, s, d)
```

```python
import jax
import jax.numpy as jnp
from jax import lax
from jax.experimental import pallas as pl
from jax.experimental.pallas import tpu as pltpu

D_MODEL = 1024
PLE_DIM = 256
POOL_WINDOWS = (2, 4, 8, 16)
POOL_GROUP = D_MODEL // len(POOL_WINDOWS)
POOL_HALO = 32
HEAD_DIM = 64
N_HEADS = 16
N_KV_HEADS = 4
KV_DIM = N_KV_HEADS * HEAD_DIM
WINDOW = 128
ROPE_THETA = 10000.0
D_FF = 2816
FF_CHUNK = 256
N_EXPERTS = 8
D_FF_EXPERT = 1024
EPS = 1e-6
LANES = 128
ROUTER_PAD = LANES

SEQ_TILE = 512
ATTN_TILE = 512
ROUTE_TILE = 512
EXPERT_TILE = 512
SEG_ALIGN = 16
SEG_SIZES = tuple(ROUTE_TILE >> i for i in range((ROUTE_TILE // SEG_ALIGN).bit_length()))
SEG_ROWS = 2 * ROUTE_TILE + N_EXPERTS * SEG_ALIGN
GATE_LANES = LANES
SORTED_WIDTH = D_MODEL + GATE_LANES
VMEM_LIMIT = 56 * 1024 * 1024

_BF = jnp.bfloat16
_F32 = jnp.float32


def _rms(x, g):
    return x * lax.rsqrt(jnp.mean(x * x, axis=-1, keepdims=True) + EPS) * g


def _sigmoid(x):
    return 1.0 / (1.0 + jnp.exp(-x))


def _dot(a, b):
    return jnp.dot(a, b, preferred_element_type=_F32)


def _head_meansq(x, hmat):
    x2 = (x * x).astype(_BF)
    width = hmat.shape[0]
    return jnp.concatenate([_dot(x2[:, c:c + width], hmat) for c in range(0, x.shape[1], width)], axis=1)


def _rope(x, cos, sin_signed):
    n = x.shape[1]
    reps = n // LANES
    lane = lax.broadcasted_iota(jnp.int32, (1, n), 1)
    first_half = (lane % HEAD_DIM) < (HEAD_DIM // 2)
    partner = jnp.where(first_half, pltpu.roll(x, n - HEAD_DIM // 2, 1), pltpu.roll(x, HEAD_DIM // 2, 1))
    c = jnp.concatenate([cos] * reps, axis=1)
    s = jnp.concatenate([sin_signed] * reps, axis=1)
    return x * c + partner * s


def _layer0_kernel(x_ref, p_ref, cos_ref, sin_ref, pool_norm, pool_w, pool_scale, ffn_norm, wgu, wd,
                   gate_norm, gate_w, ple_w, attn_norm, wq, q_gain, kv_norm, wkv, k_gain, hmat,
                   h_out, q_out, k_out, v_out, ubuf, sum_a, sum_b):
    si = pl.program_id(1)
    ts = x_ref.shape[0]
    x = x_ref[...]
    halo, end = POOL_HALO, POOL_HALO + ts

    @pl.when(si == 0)
    def _():
        ubuf[0:halo, :] = jnp.zeros((halo, D_MODEL), _F32)

    u = _rms(x, pool_norm[...])
    ubuf[halo:end, :] = u
    g1, g2, g3 = POOL_GROUP, 2 * POOL_GROUP, 3 * POOL_GROUP
    s2 = ubuf[8:end, :] + ubuf[7:end - 1, :]
    sum_a[8:end, g1:] = s2[:, g1:]
    s4 = sum_a[16:end, g1:] + sum_a[14:end - 2, g1:]
    sum_b[16:end, g2:] = s4[:, g1:]
    s8 = sum_b[24:end, g2:] + sum_b[20:end - 4, g2:]
    sum_a[24:end, g3:] = s8[:, g1:]
    s16 = sum_a[32:end, g3:] + sum_a[24:end - 8, g3:]
    window_sums = (s2[halo - 8:, :g1], s4[halo - 16:, :g1], s8[halo - 24:, :g1], s16)
    pos = (si * ts + 1 + lax.broadcasted_iota(jnp.int32, (ts, 1), 0)).astype(_F32)
    mixed = []
    for g, win in enumerate(POOL_WINDOWS):
        ug = u[:, g * POOL_GROUP:(g + 1) * POOL_GROUP]
        d = window_sums[g] / jnp.minimum(pos, float(win)) - ug
        mixed.append(_dot(d.astype(_BF), pool_w[g]))
    h = x + jnp.concatenate(mixed, axis=1) * pool_scale[...]
    ubuf[0:halo, :] = ubuf[ts:end, :]

    uf = _rms(h, ffn_norm[...]).astype(_BF)

    acc = jnp.zeros((ts, D_MODEL), _F32)
    for c0 in range(0, D_FF, FF_CHUNK):
        a = _dot(uf, wgu[:, c0:c0 + FF_CHUNK])
        g = _dot(uf, wgu[:, D_FF + c0:D_FF + c0 + FF_CHUNK])
        act = (a * _sigmoid(a) * g).astype(_BF)
        acc = acc + _dot(act, wd[c0:c0 + FF_CHUNK, :])
    h = h + acc

    gate = _sigmoid(_dot(_rms(h, gate_norm[...]).astype(_BF), gate_w[...]))
    h = h + gate * _dot(p_ref[...].astype(_BF), ple_w[...])
    h_out[...] = h

    cos = cos_ref[...]
    sin = sin_ref[...]
    unit = h * lax.rsqrt(jnp.mean(h * h, axis=-1, keepdims=True) + EPS)
    q = _dot((unit * attn_norm[...]).astype(_BF), wq[...])
    q = q * lax.rsqrt(_head_meansq(q, hmat[...]) + EPS) * q_gain[...]
    q_out[...] = _rope(q, cos, sin).astype(_BF)
    kv = _dot((unit * kv_norm[...]).astype(_BF), wkv[...])
    k = kv[:, :KV_DIM]
    k = k * lax.rsqrt(_head_meansq(k, hmat[...]) + EPS) * k_gain[...]
    k_out[...] = _rope(k, cos, sin).astype(_BF)
    v_out[...] = kv[:, KV_DIM:].astype(_BF)


def _attn_kernel(sinks_ref, q_ref, kp_ref, kc_ref, vp_ref, vc_ref, h_ref, wo_ref, out_ref,
                 klo, khi, vlo, vhi, o_scr):
    i = pl.program_id(1)
    tq = q_ref.shape[0]
    nblk = tq // WINDOW
    lane = lax.broadcasted_iota(jnp.int32, (1, LANES), 1)
    low = lane < HEAD_DIM

    qi = lax.broadcasted_iota(jnp.int32, (2 * WINDOW, 2 * WINDOW), 0) % WINDOW
    kj = lax.broadcasted_iota(jnp.int32, (2 * WINDOW, 2 * WINDOW), 1)
    in_window = (kj > qi) & (kj <= qi + WINDOW)
    upper_rows = lax.broadcasted_iota(jnp.int32, (2 * WINDOW, 1), 0) >= WINDOW

    for g in range(N_KV_HEADS):
        slab = slice((g // 2) * LANES, (g // 2 + 1) * LANES)
        for src_p, src_c, lo_ref, hi_ref in ((kp_ref, kc_ref, klo, khi), (vp_ref, vc_ref, vlo, vhi)):
            t = jnp.concatenate([src_p[:, slab], src_c[:, slab]], axis=0).astype(_F32)
            r = pltpu.roll(t, HEAD_DIM, 1)
            in_low, in_high = (t, r) if g % 2 == 0 else (r, t)
            lo_ref[...] = jnp.where(low, in_low, 0.0).astype(_BF)
            hi_ref[...] = jnp.where(low, 0.0, in_high).astype(_BF)

        for j in range(nblk):
            qrows = pl.ds(j * WINDOW, WINDOW)
            krows = pl.ds(j * WINDOW, 2 * WINDOW)
            qg = jnp.concatenate([q_ref[qrows, (2 * g) * LANES:(2 * g + 1) * LANES],
                                  q_ref[qrows, (2 * g + 1) * LANES:(2 * g + 2) * LANES]], axis=0)
            mask = in_window & (kj >= jnp.where(i > 0, 0, WINDOW)) if j == 0 else in_window
            o = jnp.zeros((2 * WINDOW, LANES), _F32)
            for half, (k_ref, v_ref) in enumerate(((klo, vlo), (khi, vhi))):
                sink = jnp.where(upper_rows, sinks_ref[4 * g + 2 + half], sinks_ref[4 * g + half])
                s = lax.dot_general(qg, k_ref[krows, :], (((1,), (1,)), ((), ())),
                                    preferred_element_type=_F32)
                s = jnp.where(mask, s, -jnp.inf)
                m = jnp.maximum(jnp.max(s, axis=1, keepdims=True), sink)
                pr = jnp.exp(s - m)
                denom = jnp.sum(pr, axis=1, keepdims=True) + jnp.exp(sink - m)
                o = o + _dot(pr.astype(_BF), v_ref[krows, :]) / denom
            o_scr[qrows, (2 * g) * LANES:(2 * g + 1) * LANES] = o[:WINDOW].astype(_BF)
            o_scr[qrows, (2 * g + 1) * LANES:(2 * g + 2) * LANES] = o[WINDOW:].astype(_BF)

    out_ref[...] = h_ref[...] + _dot(o_scr[...], wo_ref[...])


def _segment_copies(rows, src_ref, src_base, dst_ref, dst_base, sem, sizes=SEG_SIZES):
    out = []
    for sz in sizes:
        done = rows & (-2 * sz)
        src = src_ref.at[pl.ds(pl.multiple_of(src_base + done, SEG_ALIGN), sz)]
        dst = dst_ref.at[pl.ds(pl.multiple_of(dst_base + done, SEG_ALIGN), sz)]
        out.append(((rows & sz) != 0, pltpu.make_async_copy(src, dst, sem)))
    return out


def _run(copies, op):
    for cond, cp in copies:
        @pl.when(cond)
        def _(cp=cp):
            getattr(cp, op)()


def _block_copies(seg_ref, blk, hbm_ref, buf_ref, sem, to_hbm):
    cap = hbm_ref.shape[0] // N_EXPERTS
    copies = []
    off = jnp.int32(0)
    for e in range(N_EXPERTS):
        base = e * cap + seg_ref[blk * 2 * N_EXPERTS + e]
        rows = seg_ref[blk * 2 * N_EXPERTS + N_EXPERTS + e]
        if to_hbm:
            copies += _segment_copies(rows, buf_ref, off, hbm_ref, base, sem)
        else:
            copies += _segment_copies(rows, hbm_ref, base, buf_ref, off, sem)
        off = off + rows
    return copies


def _route_kernel(h_ref, ffn_norm, rw_hi, rw_lo, rb, xs_hbm, meta_ref, seg_ref, tot_ref,
                  comp_scr, zero_scr, cum_ref, sem):
    k = pl.program_id(0)
    bs = h_ref.shape[0]
    cap = xs_hbm.shape[0] // N_EXPERTS
    lane = lax.broadcasted_iota(jnp.int32, (1, ROUTER_PAD), 1)

    @pl.when(k == 0)
    def _():
        for e in range(N_EXPERTS):
            cum_ref[e] = 0

    u = _rms(h_ref[...], ffn_norm[...])
    u_hi = u.astype(_BF)
    u_lo = (u - u_hi.astype(_F32)).astype(_BF)
    logits = _dot(u_hi, rw_hi[...]) + _dot(u_lo, rw_hi[...]) + _dot(u_hi, rw_lo[...]) + rb[...]
    logits = jnp.where(lane < N_EXPERTS, logits, -jnp.inf)
    m1 = jnp.max(logits, axis=1, keepdims=True)
    i1 = jnp.min(jnp.where(logits == m1, lane, ROUTER_PAD), axis=1, keepdims=True)
    rest = jnp.where(lane == i1, -jnp.inf, logits)
    m2 = jnp.max(rest, axis=1, keepdims=True)
    i2 = jnp.min(jnp.where(rest == m2, lane, ROUTER_PAD), axis=1, keepdims=True)
    t = jnp.exp(m2 - m1)
    w1 = 1.0 / (1.0 + t)
    w2 = t * w1

    a1 = lane == i1
    a2 = lane == i2
    assigned = jnp.where(a1 | a2, 1.0, 0.0)
    before = (lax.broadcasted_iota(jnp.int32, (bs, bs), 1) < lax.broadcasted_iota(jnp.int32, (bs, bs), 0))
    rank = _dot(jnp.where(before, 1.0, 0.0).astype(_BF), assigned.astype(_BF))
    count = jnp.sum(assigned, axis=0, keepdims=True).astype(jnp.int32)
    seg_off = jnp.zeros((1, ROUTER_PAD), _F32)
    off = jnp.int32(0)
    for e in range(N_EXPERTS):
        rows = ((count[0, e] + (SEG_ALIGN - 1)) // SEG_ALIGN) * SEG_ALIGN
        seg_off = jnp.where(lane == e, off.astype(_F32), seg_off)
        base = cum_ref[e]
        seg_ref[k * 2 * N_EXPERTS + e] = base
        seg_ref[k * 2 * N_EXPERTS + N_EXPERTS + e] = rows
        cum_ref[e] = base + rows
        off = off + rows
    dst1 = jnp.sum(jnp.where(a1, rank + seg_off, 0.0), axis=1, keepdims=True)
    dst2 = jnp.sum(jnp.where(a2, rank + seg_off, 0.0), axis=1, keepdims=True)
    meta = jnp.where(lane == 0, dst1, jnp.where(lane == 1, dst2, jnp.where(lane == 2, w1, jnp.where(lane == 3, w2, 0.0))))
    meta_ref[...] = meta

    by_token = meta.T
    row = lax.broadcasted_iota(jnp.int32, (SEG_ROWS, 1), 0).astype(_F32)
    from1 = row == by_token[0:1, :]
    from2 = row == by_token[1:2, :]
    onehot = jnp.where(from1 | from2, 1.0, 0.0).astype(_BF)
    gate = jnp.sum(jnp.where(from1, by_token[2:3, :], 0.0) + jnp.where(from2, by_token[3:4, :], 0.0),
                   axis=1, keepdims=True)
    hi = gate.astype(_BF).astype(_F32)
    mid = (gate - hi).astype(_BF).astype(_F32)
    gate_parts = jnp.where(lane == 0, hi, jnp.where(lane == 1, mid, jnp.where(lane == 2, gate - hi - mid, 0.0)))

    slot = k % 2
    comp = comp_scr.at[slot]
    comp[:, :D_MODEL] = _dot(onehot, u_hi).astype(_BF)
    comp[:, D_MODEL:] = gate_parts.astype(_BF)
    _run(_block_copies(seg_ref, k, xs_hbm, comp, sem.at[slot], to_hbm=True), "start")

    @pl.when(k > 0)
    def _():
        _run(_block_copies(seg_ref, k - 1, xs_hbm, comp_scr.at[1 - slot], sem.at[1 - slot], to_hbm=True), "wait")

    @pl.when(k == pl.num_programs(0) - 1)
    def _():
        _run(_block_copies(seg_ref, k, xs_hbm, comp, sem.at[slot], to_hbm=True), "wait")
        zero_scr[...] = jnp.zeros_like(zero_scr)
        fills = []
        for e in range(N_EXPERTS):
            total = cum_ref[e]
            tot_ref[e] = total
            fills += _segment_copies((-total) & (EXPERT_TILE - 1), zero_scr, 0, xs_hbm, e * cap + total,
                                     sem.at[slot], sizes=SEG_SIZES[1:])
        _run(fills, "start")
        _run(fills, "wait")


def _expert_kernel(te_ref, tb_ref, nt_ref, x_ref, wa, wg, wd, y_ref):
    @pl.when(pl.program_id(0) < nt_ref[0])
    def _():
        x = x_ref[:, :D_MODEL]
        gate = jnp.sum(x_ref[:, D_MODEL:].astype(_F32), axis=1, keepdims=True)
        a = _dot(x, wa[...])
        g = _dot(x, wg[...])
        act = (a * _sigmoid(a) * g).astype(_BF)
        y_ref[...] = (gate * _dot(act, wd[...])).astype(_BF)


def _combine_kernel(seg_ref, h_ref, p_ref, meta_ref, ys_hbm, gate_norm, gate_w, ple_w, out_ref, ycat, sem):
    k = pl.program_id(0)
    slot = k % 2
    fetch = lambda blk, s, op: _run(_block_copies(seg_ref, blk, ys_hbm, ycat.at[s], sem.at[s], to_hbm=False), op)

    @pl.when(k == 0)
    def _():
        ycat[...] = jnp.zeros_like(ycat)
        fetch(k, slot, "start")

    @pl.when(k + 1 < pl.num_programs(0))
    def _():
        fetch(k + 1, 1 - slot, "start")

    fetch(k, slot, "wait")

    pos = lax.broadcasted_iota(jnp.int32, (1, SEG_ROWS), 1).astype(_F32)
    meta = meta_ref[...]
    onehot = jnp.where((pos == meta[:, 0:1]) | (pos == meta[:, 1:2]), 1.0, 0.0).astype(_BF)
    h = h_ref[...] + _dot(onehot, ycat[slot])
    gate = _sigmoid(_dot(_rms(h, gate_norm[...]).astype(_BF), gate_w[...]))
    out_ref[...] = h + gate * _dot(p_ref[...].astype(_BF), ple_w[...])


def _const(shape):
    return pl.BlockSpec(shape, lambda *_: (0,) * len(shape), pipeline_mode=pl.Buffered(1))


def _row(v):
    return v.reshape(1, -1).astype(_F32)


def kernel(x, p, pool_norm, pool_w, pool_scale, kv_norm, w_kv, k_norm, attn_norm, w_q, q_norm, sinks, w_o,
           ffn_norm, w_gu, w_down, router_w, router_b, we_gu, we_down, ple_gate_norm, ple_gate_w, ple_w):
    b, s, d = x.shape
    ts, tq = SEQ_TILE, ATTN_TILE

    inv = ROPE_THETA ** (-jnp.arange(0, HEAD_DIM, 2, dtype=_F32) / HEAD_DIM)
    ang = jnp.arange(s, dtype=_F32)[:, None] * inv[None, :]
    cos_t = jnp.tile(jnp.cos(ang), (1, LANES // (HEAD_DIM // 2)))
    sin_t = jnp.tile(jnp.concatenate([-jnp.sin(ang), jnp.sin(ang)], axis=1), (1, LANES // HEAD_DIM))
    hid = jnp.arange(KV_DIM) // HEAD_DIM
    hmat = ((hid[:, None] == hid[None, :]) * (1.0 / HEAD_DIM)).astype(_BF)

    q_gain = jnp.tile(q_norm[0].astype(_F32) * (HEAD_DIM ** -0.5), N_HEADS).reshape(1, d)
    k_gain = jnp.tile(k_norm.astype(_F32), N_KV_HEADS).reshape(1, KV_DIM)

    tile3 = lambda w: pl.BlockSpec((None, ts, w), lambda bi, si: (bi, si, 0))
    h1, q, k, v = pl.pallas_call(
        _layer0_kernel,
        grid=(b, s // ts),
        in_specs=[
            tile3(d), pl.BlockSpec((None, None, ts, PLE_DIM), lambda bi, si: (0, bi, si, 0)),
            pl.BlockSpec((ts, LANES), lambda bi, si: (si, 0)),
            pl.BlockSpec((ts, LANES), lambda bi, si: (si, 0)),
            _const((1, d)), _const((len(POOL_WINDOWS), POOL_GROUP, POOL_GROUP)), _const((1, d)),
            _const((1, d)), _const((d, 2 * D_FF)), _const((D_FF, d)),
            _const((1, d)), _const((d, d)), _const((PLE_DIM, d)),
            _const((1, d)), _const((d, d)), _const((1, d)),
            _const((1, d)), _const((d, 2 * KV_DIM)), _const((1, KV_DIM)), _const((KV_DIM, KV_DIM)),
        ],
        out_specs=[tile3(d), tile3(d), tile3(KV_DIM), tile3(KV_DIM)],
        out_shape=[jax.ShapeDtypeStruct((b, s, d), _F32), jax.ShapeDtypeStruct((b, s, d), _BF),
                   jax.ShapeDtypeStruct((b, s, KV_DIM), _BF), jax.ShapeDtypeStruct((b, s, KV_DIM), _BF)],
        scratch_shapes=[pltpu.VMEM((POOL_HALO + ts, d), _F32)] * 3,
        compiler_params=pltpu.CompilerParams(dimension_semantics=("arbitrary", "arbitrary"),
                                             vmem_limit_bytes=VMEM_LIMIT),
        name="layer0",
    )(x, p, cos_t, sin_t, _row(pool_norm[0]), pool_w[0].astype(_BF), _row(pool_scale[0]),
      _row(ffn_norm[0]), w_gu[0].astype(_BF), w_down[0].astype(_BF), _row(ple_gate_norm[0]), ple_gate_w[0].astype(_BF), ple_w[0].astype(_BF),
      _row(attn_norm[0]), w_q[0].astype(_BF), q_gain, _row(kv_norm), w_kv.astype(_BF), k_gain, hmat)

    blocks_per_tile = tq // WINDOW
    prev = lambda bi, qi, *_: (bi, jnp.maximum(qi * blocks_per_tile - 1, 0), 0)
    cur = lambda bi, qi, *_: (bi, qi, 0)
    h2 = pl.pallas_call(
        _attn_kernel,
        grid_spec=pltpu.PrefetchScalarGridSpec(
            num_scalar_prefetch=1,
            grid=(b, s // tq),
            in_specs=[
                pl.BlockSpec((None, tq, d), cur),
                pl.BlockSpec((None, WINDOW, KV_DIM), prev), pl.BlockSpec((None, tq, KV_DIM), cur),
                pl.BlockSpec((None, WINDOW, KV_DIM), prev), pl.BlockSpec((None, tq, KV_DIM), cur),
                pl.BlockSpec((None, tq, d), cur),
                pl.BlockSpec((d, d), lambda *_: (0, 0), pipeline_mode=pl.Buffered(1)),
            ],
            out_specs=pl.BlockSpec((None, tq, d), cur),
            scratch_shapes=[pltpu.VMEM((WINDOW + tq, LANES), _BF)] * 4 + [pltpu.VMEM((tq, d), _BF)],
        ),
        out_shape=jax.ShapeDtypeStruct((b, s, d), _F32),
        compiler_params=pltpu.CompilerParams(dimension_semantics=("arbitrary", "arbitrary"),
                                             vmem_limit_bytes=VMEM_LIMIT),
        name="attn",
    )(sinks[0].astype(_F32), q, k, k, v, v, h1, w_o[0].astype(_BF))

    t = b * s
    rw = jnp.pad(router_w[0].astype(_F32), ((0, 0), (0, ROUTER_PAD - N_EXPERTS)))
    rw_hi = rw.astype(_BF)
    rw_lo = (rw - rw_hi.astype(_F32)).astype(_BF)
    rb = jnp.pad(router_b[0].astype(_F32), (0, ROUTER_PAD - N_EXPERTS)).reshape(1, ROUTER_PAD)
    wegu = we_gu[0].astype(_BF)
    wed = we_down[0].astype(_BF)

    bs, te = ROUTE_TILE, EXPERT_TILE
    nblk = t // bs
    cap = t
    smem = pl.BlockSpec(memory_space=pltpu.SMEM)
    tok = lambda w: pl.BlockSpec((bs, w), lambda ki, *_: (ki, 0))
    xs, meta, seg, totals = pl.pallas_call(
        _route_kernel,
        grid=(nblk,),
        in_specs=[tok(d), _const((1, d)), _const((d, ROUTER_PAD)), _const((d, ROUTER_PAD)), _const((1, ROUTER_PAD))],
        out_specs=[pl.BlockSpec(memory_space=pl.ANY), tok(ROUTER_PAD), smem, smem],
        out_shape=[jax.ShapeDtypeStruct((N_EXPERTS * cap, SORTED_WIDTH), _BF),
                   jax.ShapeDtypeStruct((t, ROUTER_PAD), _F32),
                   jax.ShapeDtypeStruct((nblk * 2 * N_EXPERTS,), jnp.int32),
                   jax.ShapeDtypeStruct((N_EXPERTS,), jnp.int32)],
        scratch_shapes=[pltpu.VMEM((2, SEG_ROWS, SORTED_WIDTH), _BF), pltpu.VMEM((EXPERT_TILE, SORTED_WIDTH), _BF),
                        pltpu.SMEM((N_EXPERTS,), jnp.int32), pltpu.SemaphoreType.DMA((2,))],
        compiler_params=pltpu.CompilerParams(dimension_semantics=("arbitrary",), vmem_limit_bytes=VMEM_LIMIT),
        name="route",
    )(h2.reshape(t, d), _row(ffn_norm[1]), rw_hi, rw_lo, rb)

    n_steps = (2 * t + nblk * N_EXPERTS * (SEG_ALIGN - 1)) // te + N_EXPERTS
    tiles_e = (totals + te - 1) // te
    tile_end = jnp.cumsum(tiles_e)
    n_tiles = tile_end[-1]
    step = jnp.minimum(jnp.arange(n_steps, dtype=jnp.int32), n_tiles - 1)
    tile_expert = jnp.sum(step[:, None] >= tile_end[None, :], axis=1).astype(jnp.int32)
    tile_block = (tile_expert * (cap // te) + step - (tile_end - tiles_e)[tile_expert]).astype(jnp.int32)

    xrow = lambda w: pl.BlockSpec((te, w), lambda i, te_r, tb_r, nt_r: (tb_r[i], 0))
    wexp = lambda r, c, cb: pl.BlockSpec((None, r, c), lambda i, te_r, tb_r, nt_r: (te_r[i], 0, cb))
    ys = pl.pallas_call(
        _expert_kernel,
        grid_spec=pltpu.PrefetchScalarGridSpec(
            num_scalar_prefetch=3,
            grid=(n_steps,),
            in_specs=[xrow(SORTED_WIDTH), wexp(d, D_FF_EXPERT, 0), wexp(d, D_FF_EXPERT, 1), wexp(D_FF_EXPERT, d, 0)],
            out_specs=xrow(d),
        ),
        out_shape=jax.ShapeDtypeStruct((N_EXPERTS * cap, d), _BF),
        compiler_params=pltpu.CompilerParams(dimension_semantics=("arbitrary",), vmem_limit_bytes=VMEM_LIMIT),
        name="experts",
    )(tile_expert, tile_block, n_tiles.reshape(1).astype(jnp.int32), xs, wegu, wegu, wed)

    out = pl.pallas_call(
        _combine_kernel,
        grid_spec=pltpu.PrefetchScalarGridSpec(
            num_scalar_prefetch=1,
            grid=(nblk,),
            in_specs=[tok(d),
                      pl.BlockSpec((None, None, bs, PLE_DIM), lambda ki, *_: (1, ki // (s // bs), ki % (s // bs), 0)),
                      tok(ROUTER_PAD), pl.BlockSpec(memory_space=pl.ANY),
                      _const((1, d)), _const((d, d)), _const((PLE_DIM, d))],
            out_specs=tok(d),
            scratch_shapes=[pltpu.VMEM((2, SEG_ROWS, d), _BF), pltpu.SemaphoreType.DMA((2,))],
        ),
        out_shape=jax.ShapeDtypeStruct((t, d), _F32),
        compiler_params=pltpu.CompilerParams(dimension_semantics=("arbitrary",), vmem_limit_bytes=VMEM_LIMIT),
        name="combine",
    )(seg, h2.reshape(t, d), p, meta, ys, _row(ple_gate_norm[1]),
      ple_gate_w[1].astype(_BF), ple_w[1].astype(_BF))
    return out.reshape(b, s, d)
```

```python
import jax
import jax.numpy as jnp
from jax import lax
from jax.experimental import pallas as pl
from jax.experimental.pallas import tpu as pltpu

D_MODEL = 1024
PLE_DIM = 256
POOL_WINDOWS = (2, 4, 8, 16)
POOL_GROUP = D_MODEL // len(POOL_WINDOWS)
POOL_HALO = 32
HEAD_DIM = 64
N_HEADS = 16
N_KV_HEADS = 4
KV_DIM = N_KV_HEADS * HEAD_DIM
WINDOW = 128
ROPE_THETA = 10000.0
D_FF = 2816
FF_CHUNK = 256
N_EXPERTS = 8
D_FF_EXPERT = 1024
EPS = 1e-6
LOG2E = 1.4426950408889634
LANES = 128
ROUTER_PAD = LANES

SEQ_TILE = 512
ATTN_TILE = 512
ROUTE_TILE = 512
EXPERT_TILE = 512
SEG_ALIGN = 16
SEG_SIZES = tuple(ROUTE_TILE >> i for i in range((ROUTE_TILE // SEG_ALIGN).bit_length()))
SEG_ROWS = 2 * ROUTE_TILE + N_EXPERTS * SEG_ALIGN
GATE_LANES = LANES
SORTED_WIDTH = D_MODEL + GATE_LANES
VMEM_LIMIT = 56 * 1024 * 1024

_BF = jnp.bfloat16
_F32 = jnp.float32


def _rms(x, g):
    return x * lax.rsqrt(jnp.mean(x * x, axis=-1, keepdims=True) + EPS) * g


def _sigmoid(x):
    return 0.5 * jnp.tanh(0.5 * x) + 0.5


def _dot(a, b):
    return jnp.dot(a, b, preferred_element_type=_F32)


def _head_meansq(x, hmat):
    x2 = (x * x).astype(_BF)
    width = hmat.shape[0]
    return jnp.concatenate([_dot(x2[:, c:c + width], hmat) for c in range(0, x.shape[1], width)], axis=1)


def _rope(x, cos, sin_signed):
    n = x.shape[1]
    reps = n // LANES
    lane = lax.broadcasted_iota(jnp.int32, (1, n), 1)
    first_half = (lane % HEAD_DIM) < (HEAD_DIM // 2)
    partner = jnp.where(first_half, pltpu.roll(x, n - HEAD_DIM // 2, 1), pltpu.roll(x, HEAD_DIM // 2, 1))
    c = jnp.concatenate([cos] * reps, axis=1)
    s = jnp.concatenate([sin_signed] * reps, axis=1)
    return x * c + partner * s


def _layer0_kernel(x_ref, p_ref, cos_ref, sin_ref, pool_norm, pool_w, pool_scale, ffn_norm, wgu, wd,
                   gate_norm, gate_w, ple_w, attn_norm, wq, q_gain, kv_norm, wkv, k_gain, hmat,
                   h_out, q_out, k_out, v_out, ubuf, sum_a, sum_b):
    si = pl.program_id(1)
    ts = x_ref.shape[0]
    x = x_ref[...]
    halo, end = POOL_HALO, POOL_HALO + ts

    @pl.when(si == 0)
    def _():
        ubuf[0:halo, :] = jnp.zeros((halo, D_MODEL), _F32)

    u = _rms(x, pool_norm[...])
    ubuf[halo:end, :] = u
    g1, g2, g3 = POOL_GROUP, 2 * POOL_GROUP, 3 * POOL_GROUP
    s2 = ubuf[8:end, :] + ubuf[7:end - 1, :]
    sum_a[8:end, g1:] = s2[:, g1:]
    s4 = sum_a[16:end, g1:] + sum_a[14:end - 2, g1:]
    sum_b[16:end, g2:] = s4[:, g1:]
    s8 = sum_b[24:end, g2:] + sum_b[20:end - 4, g2:]
    sum_a[24:end, g3:] = s8[:, g1:]
    s16 = sum_a[32:end, g3:] + sum_a[24:end - 8, g3:]
    window_sums = (s2[halo - 8:, :g1], s4[halo - 16:, :g1], s8[halo - 24:, :g1], s16)
    pos = (si * ts + 1 + lax.broadcasted_iota(jnp.int32, (ts, 1), 0)).astype(_F32)
    mixed = []
    for g, win in enumerate(POOL_WINDOWS):
        ug = u[:, g * POOL_GROUP:(g + 1) * POOL_GROUP]
        d = window_sums[g] / jnp.minimum(pos, float(win)) - ug
        mixed.append(_dot(d.astype(_BF), pool_w[g]))
    h = x + jnp.concatenate(mixed, axis=1) * pool_scale[...]
    ubuf[0:halo, :] = ubuf[ts:end, :]

    uf = _rms(h, ffn_norm[...]).astype(_BF)

    acc = jnp.zeros((ts, D_MODEL), _F32)
    for c0 in range(0, D_FF, FF_CHUNK):
        a = _dot(uf, wgu[:, c0:c0 + FF_CHUNK])
        g = _dot(uf, wgu[:, D_FF + c0:D_FF + c0 + FF_CHUNK])
        act = (a * _sigmoid(a) * g).astype(_BF)
        acc = acc + _dot(act, wd[c0:c0 + FF_CHUNK, :])
    h = h + acc

    gate = _sigmoid(_dot(_rms(h, gate_norm[...]).astype(_BF), gate_w[...]))
    h = h + gate * _dot(p_ref[...].astype(_BF), ple_w[...])
    h_out[...] = h

    cos = cos_ref[...]
    sin = sin_ref[...]
    unit = h * lax.rsqrt(jnp.mean(h * h, axis=-1, keepdims=True) + EPS)
    q = _dot((unit * attn_norm[...]).astype(_BF), wq[...])
    q = q * lax.rsqrt(_head_meansq(q, hmat[...]) + EPS) * q_gain[...]
    q_out[...] = _rope(q, cos, sin).astype(_BF)
    kv = _dot((unit * kv_norm[...]).astype(_BF), wkv[...])
    k = kv[:, :KV_DIM]
    k = k * lax.rsqrt(_head_meansq(k, hmat[...]) + EPS) * k_gain[...]
    k_out[...] = _rope(k, cos, sin).astype(_BF)
    v_out[...] = kv[:, KV_DIM:].astype(_BF)


def _attn_kernel(sinks_ref, q_ref, kp_ref, kc_ref, vp_ref, vc_ref, h_ref, wo_ref, out_ref,
                 klo, khi, vlo, vhi, o_scr):
    i = pl.program_id(1)
    tq = q_ref.shape[0]
    nblk = tq // WINDOW
    lane = lax.broadcasted_iota(jnp.int32, (1, LANES), 1)
    low = lane < HEAD_DIM

    qi = lax.broadcasted_iota(jnp.int32, (2 * WINDOW, 2 * WINDOW), 0) % WINDOW
    kj = lax.broadcasted_iota(jnp.int32, (2 * WINDOW, 2 * WINDOW), 1)
    in_window = (kj > qi) & (kj <= qi + WINDOW)
    upper_rows = lax.broadcasted_iota(jnp.int32, (2 * WINDOW, 1), 0) >= WINDOW

    for g in range(N_KV_HEADS):
        slab = slice((g // 2) * LANES, (g // 2 + 1) * LANES)
        for src_p, src_c, lo_ref, hi_ref in ((kp_ref, kc_ref, klo, khi), (vp_ref, vc_ref, vlo, vhi)):
            t = jnp.concatenate([src_p[:, slab], src_c[:, slab]], axis=0).astype(_F32)
            r = pltpu.roll(t, HEAD_DIM, 1)
            in_low, in_high = (t, r) if g % 2 == 0 else (r, t)
            lo_ref[...] = jnp.where(low, in_low, 0.0).astype(_BF)
            hi_ref[...] = jnp.where(low, 0.0, in_high).astype(_BF)

        for j in range(nblk):
            qrows = pl.ds(j * WINDOW, WINDOW)
            krows = pl.ds(j * WINDOW, 2 * WINDOW)
            qg = jnp.concatenate([q_ref[qrows, (2 * g) * LANES:(2 * g + 1) * LANES],
                                  q_ref[qrows, (2 * g + 1) * LANES:(2 * g + 2) * LANES]], axis=0)
            mask = in_window & (kj >= jnp.where(i > 0, 0, WINDOW)) if j == 0 else in_window
            o = jnp.zeros((2 * WINDOW, LANES), _F32)
            for half, (k_ref, v_ref) in enumerate(((klo, vlo), (khi, vhi))):
                sink = jnp.where(upper_rows, sinks_ref[4 * g + 2 + half], sinks_ref[4 * g + half]) * LOG2E
                s = lax.dot_general(qg, k_ref[krows, :], (((1,), (1,)), ((), ())),
                                    preferred_element_type=_F32)
                s = jnp.where(mask, s, -jnp.inf)
                m = jnp.maximum(jnp.max(s, axis=1, keepdims=True), sink)
                pr = jnp.exp2(s - m)
                denom = jnp.sum(pr, axis=1, keepdims=True) + jnp.exp2(sink - m)
                o = o + _dot(pr.astype(_BF), v_ref[krows, :]) / denom
            o_scr[qrows, (2 * g) * LANES:(2 * g + 1) * LANES] = o[:WINDOW].astype(_BF)
            o_scr[qrows, (2 * g + 1) * LANES:(2 * g + 2) * LANES] = o[WINDOW:].astype(_BF)

    out_ref[...] = h_ref[...] + _dot(o_scr[...], wo_ref[...])


def _segment_copies(rows, src_ref, src_base, dst_ref, dst_base, sem, sizes=SEG_SIZES):
    out = []
    for sz in sizes:
        done = rows & (-2 * sz)
        src = src_ref.at[pl.ds(pl.multiple_of(src_base + done, SEG_ALIGN), sz)]
        dst = dst_ref.at[pl.ds(pl.multiple_of(dst_base + done, SEG_ALIGN), sz)]
        out.append(((rows & sz) != 0, pltpu.make_async_copy(src, dst, sem)))
    return out


def _run(copies, op):
    for cond, cp in copies:
        @pl.when(cond)
        def _(cp=cp):
            getattr(cp, op)()


def _block_copies(seg_ref, blk, hbm_ref, buf_ref, sem, to_hbm):
    cap = hbm_ref.shape[0] // N_EXPERTS
    copies = []
    off = jnp.int32(0)
    for e in range(N_EXPERTS):
        base = e * cap + seg_ref[blk * 2 * N_EXPERTS + e]
        rows = seg_ref[blk * 2 * N_EXPERTS + N_EXPERTS + e]
        if to_hbm:
            copies += _segment_copies(rows, buf_ref, off, hbm_ref, base, sem)
        else:
            copies += _segment_copies(rows, hbm_ref, base, buf_ref, off, sem)
        off = off + rows
    return copies


def _route_kernel(h_ref, ffn_norm, rw_hi, rw_lo, rb, xs_hbm, meta_ref, seg_ref, tot_ref,
                  comp_scr, zero_scr, cum_ref, sem):
    k = pl.program_id(0)
    bs = h_ref.shape[0]
    cap = xs_hbm.shape[0] // N_EXPERTS
    lane = lax.broadcasted_iota(jnp.int32, (1, ROUTER_PAD), 1)

    @pl.when(k == 0)
    def _():
        for e in range(N_EXPERTS):
            cum_ref[e] = 0

    u = _rms(h_ref[...], ffn_norm[...])
    u_hi = u.astype(_BF)
    u_lo = (u - u_hi.astype(_F32)).astype(_BF)
    logits = _dot(u_hi, rw_hi[...]) + _dot(u_lo, rw_hi[...]) + _dot(u_hi, rw_lo[...]) + rb[...]
    logits = jnp.where(lane < N_EXPERTS, logits, -jnp.inf)
    m1 = jnp.max(logits, axis=1, keepdims=True)
    i1 = jnp.min(jnp.where(logits == m1, lane, ROUTER_PAD), axis=1, keepdims=True)
    rest = jnp.where(lane == i1, -jnp.inf, logits)
    m2 = jnp.max(rest, axis=1, keepdims=True)
    i2 = jnp.min(jnp.where(rest == m2, lane, ROUTER_PAD), axis=1, keepdims=True)
    t = jnp.exp(m2 - m1)
    w1 = 1.0 / (1.0 + t)
    w2 = t * w1

    a1 = lane == i1
    a2 = lane == i2
    assigned = jnp.where(a1 | a2, 1.0, 0.0)
    before = (lax.broadcasted_iota(jnp.int32, (bs, bs), 1) < lax.broadcasted_iota(jnp.int32, (bs, bs), 0))
    rank = _dot(jnp.where(before, 1.0, 0.0).astype(_BF), assigned.astype(_BF))
    count = jnp.sum(assigned, axis=0, keepdims=True).astype(jnp.int32)
    seg_off = jnp.zeros((1, ROUTER_PAD), _F32)
    off = jnp.int32(0)
    for e in range(N_EXPERTS):
        rows = ((count[0, e] + (SEG_ALIGN - 1)) // SEG_ALIGN) * SEG_ALIGN
        seg_off = jnp.where(lane == e, off.astype(_F32), seg_off)
        base = cum_ref[e]
        seg_ref[k * 2 * N_EXPERTS + e] = base
        seg_ref[k * 2 * N_EXPERTS + N_EXPERTS + e] = rows
        cum_ref[e] = base + rows
        off = off + rows
    dst1 = jnp.sum(jnp.where(a1, rank + seg_off, 0.0), axis=1, keepdims=True)
    dst2 = jnp.sum(jnp.where(a2, rank + seg_off, 0.0), axis=1, keepdims=True)
    meta = jnp.where(lane == 0, dst1, jnp.where(lane == 1, dst2, jnp.where(lane == 2, w1, jnp.where(lane == 3, w2, 0.0))))
    meta_ref[...] = meta

    by_token = meta.T
    row = lax.broadcasted_iota(jnp.int32, (SEG_ROWS, 1), 0).astype(_F32)
    from1 = row == by_token[0:1, :]
    from2 = row == by_token[1:2, :]
    onehot = jnp.where(from1 | from2, 1.0, 0.0).astype(_BF)
    gate = jnp.sum(jnp.where(from1, by_token[2:3, :], 0.0) + jnp.where(from2, by_token[3:4, :], 0.0),
                   axis=1, keepdims=True)
    hi = gate.astype(_BF).astype(_F32)
    mid = (gate - hi).astype(_BF).astype(_F32)
    gate_parts = jnp.where(lane == 0, hi, jnp.where(lane == 1, mid, jnp.where(lane == 2, gate - hi - mid, 0.0)))

    slot = k % 2
    comp = comp_scr.at[slot]
    comp[:, :D_MODEL] = _dot(onehot, u_hi).astype(_BF)
    comp[:, D_MODEL:] = gate_parts.astype(_BF)
    _run(_block_copies(seg_ref, k, xs_hbm, comp, sem.at[slot], to_hbm=True), "start")

    @pl.when(k > 0)
    def _():
        _run(_block_copies(seg_ref, k - 1, xs_hbm, comp_scr.at[1 - slot], sem.at[1 - slot], to_hbm=True), "wait")

    @pl.when(k == pl.num_programs(0) - 1)
    def _():
        _run(_block_copies(seg_ref, k, xs_hbm, comp, sem.at[slot], to_hbm=True), "wait")
        zero_scr[...] = jnp.zeros_like(zero_scr)
        fills = []
        for e in range(N_EXPERTS):
            total = cum_ref[e]
            tot_ref[e] = total
            fills += _segment_copies((-total) & (EXPERT_TILE - 1), zero_scr, 0, xs_hbm, e * cap + total,
                                     sem.at[slot], sizes=SEG_SIZES[1:])
        _run(fills, "start")
        _run(fills, "wait")


def _expert_kernel(te_ref, tb_ref, nt_ref, x_ref, wa_f32, wg_f32, wd_f32, y_ref, wa, wg, wd):
    i = pl.program_id(0)

    @pl.when((i == 0) | (te_ref[i] != te_ref[jnp.maximum(i - 1, 0)]))
    def _():
        wa[...] = wa_f32[...].astype(_BF)
        wg[...] = wg_f32[...].astype(_BF)
        wd[...] = wd_f32[...].astype(_BF)

    @pl.when(i < nt_ref[0])
    def _():
        x = x_ref[:, :D_MODEL]
        gate = jnp.sum(x_ref[:, D_MODEL:].astype(_F32), axis=1, keepdims=True)
        a = _dot(x, wa[...])
        g = _dot(x, wg[...])
        act = (a * _sigmoid(a) * g).astype(_BF)
        y_ref[...] = (gate * _dot(act, wd[...])).astype(_BF)


def _combine_kernel(seg_ref, h_ref, p_ref, meta_ref, ys_hbm, gate_norm, gate_w, ple_w, out_ref, ycat, sem):
    k = pl.program_id(0)
    slot = k % 2
    fetch = lambda blk, s, op: _run(_block_copies(seg_ref, blk, ys_hbm, ycat.at[s], sem.at[s], to_hbm=False), op)

    @pl.when(k == 0)
    def _():
        ycat[...] = jnp.zeros_like(ycat)
        fetch(k, slot, "start")

    @pl.when(k + 1 < pl.num_programs(0))
    def _():
        fetch(k + 1, 1 - slot, "start")

    fetch(k, slot, "wait")

    pos = lax.broadcasted_iota(jnp.int32, (1, SEG_ROWS), 1).astype(_F32)
    meta = meta_ref[...]
    onehot = jnp.where((pos == meta[:, 0:1]) | (pos == meta[:, 1:2]), 1.0, 0.0).astype(_BF)
    h = h_ref[...] + _dot(onehot, ycat[slot])
    gate = _sigmoid(_dot(_rms(h, gate_norm[...]).astype(_BF), gate_w[...]))
    out_ref[...] = h + gate * _dot(p_ref[...].astype(_BF), ple_w[...])


def _const(shape):
    return pl.BlockSpec(shape, lambda *_: (0,) * len(shape), pipeline_mode=pl.Buffered(1))


def _row(v):
    return v.reshape(1, -1).astype(_F32)


def kernel(x, p, pool_norm, pool_w, pool_scale, kv_norm, w_kv, k_norm, attn_norm, w_q, q_norm, sinks, w_o,
           ffn_norm, w_gu, w_down, router_w, router_b, we_gu, we_down, ple_gate_norm, ple_gate_w, ple_w):
    b, s, d = x.shape
    ts, tq = SEQ_TILE, ATTN_TILE

    inv = ROPE_THETA ** (-jnp.arange(0, HEAD_DIM, 2, dtype=_F32) / HEAD_DIM)
    ang = jnp.arange(s, dtype=_F32)[:, None] * inv[None, :]
    cos_t = jnp.tile(jnp.cos(ang), (1, LANES // (HEAD_DIM // 2)))
    sin_t = jnp.tile(jnp.concatenate([-jnp.sin(ang), jnp.sin(ang)], axis=1), (1, LANES // HEAD_DIM))
    hid = jnp.arange(KV_DIM) // HEAD_DIM
    hmat = ((hid[:, None] == hid[None, :]) * (1.0 / HEAD_DIM)).astype(_BF)

    q_gain = jnp.tile(q_norm[0].astype(_F32) * (HEAD_DIM ** -0.5 * LOG2E), N_HEADS).reshape(1, d)
    k_gain = jnp.tile(k_norm.astype(_F32), N_KV_HEADS).reshape(1, KV_DIM)

    tile3 = lambda w: pl.BlockSpec((None, ts, w), lambda bi, si: (bi, si, 0))
    h1, q, k, v = pl.pallas_call(
        _layer0_kernel,
        grid=(b, s // ts),
        in_specs=[
            tile3(d), pl.BlockSpec((None, None, ts, PLE_DIM), lambda bi, si: (0, bi, si, 0)),
            pl.BlockSpec((ts, LANES), lambda bi, si: (si, 0)),
            pl.BlockSpec((ts, LANES), lambda bi, si: (si, 0)),
            _const((1, d)), _const((len(POOL_WINDOWS), POOL_GROUP, POOL_GROUP)), _const((1, d)),
            _const((1, d)), _const((d, 2 * D_FF)), _const((D_FF, d)),
            _const((1, d)), _const((d, d)), _const((PLE_DIM, d)),
            _const((1, d)), _const((d, d)), _const((1, d)),
            _const((1, d)), _const((d, 2 * KV_DIM)), _const((1, KV_DIM)), _const((KV_DIM, KV_DIM)),
        ],
        out_specs=[tile3(d), tile3(d), tile3(KV_DIM), tile3(KV_DIM)],
        out_shape=[jax.ShapeDtypeStruct((b, s, d), _F32), jax.ShapeDtypeStruct((b, s, d), _BF),
                   jax.ShapeDtypeStruct((b, s, KV_DIM), _BF), jax.ShapeDtypeStruct((b, s, KV_DIM), _BF)],
        scratch_shapes=[pltpu.VMEM((POOL_HALO + ts, d), _F32)] * 3,
        compiler_params=pltpu.CompilerParams(dimension_semantics=("arbitrary", "arbitrary"),
                                             vmem_limit_bytes=VMEM_LIMIT),
        name="layer0",
    )(x, p, cos_t, sin_t, _row(pool_norm[0]), pool_w[0].astype(_BF), _row(pool_scale[0]),
      _row(ffn_norm[0]), w_gu[0].astype(_BF), w_down[0].astype(_BF), _row(ple_gate_norm[0]), ple_gate_w[0].astype(_BF), ple_w[0].astype(_BF),
      _row(attn_norm[0]), w_q[0].astype(_BF), q_gain, _row(kv_norm), w_kv.astype(_BF), k_gain, hmat)

    blocks_per_tile = tq // WINDOW
    prev = lambda bi, qi, *_: (bi, jnp.maximum(qi * blocks_per_tile - 1, 0), 0)
    cur = lambda bi, qi, *_: (bi, qi, 0)
    h2 = pl.pallas_call(
        _attn_kernel,
        grid_spec=pltpu.PrefetchScalarGridSpec(
            num_scalar_prefetch=1,
            grid=(b, s // tq),
            in_specs=[
                pl.BlockSpec((None, tq, d), cur),
                pl.BlockSpec((None, WINDOW, KV_DIM), prev), pl.BlockSpec((None, tq, KV_DIM), cur),
                pl.BlockSpec((None, WINDOW, KV_DIM), prev), pl.BlockSpec((None, tq, KV_DIM), cur),
                pl.BlockSpec((None, tq, d), cur),
                pl.BlockSpec((d, d), lambda *_: (0, 0), pipeline_mode=pl.Buffered(1)),
            ],
            out_specs=pl.BlockSpec((None, tq, d), cur),
            scratch_shapes=[pltpu.VMEM((WINDOW + tq, LANES), _BF)] * 4 + [pltpu.VMEM((tq, d), _BF)],
        ),
        out_shape=jax.ShapeDtypeStruct((b, s, d), _F32),
        compiler_params=pltpu.CompilerParams(dimension_semantics=("arbitrary", "arbitrary"),
                                             vmem_limit_bytes=VMEM_LIMIT),
        name="attn",
    )(sinks[0].astype(_F32), q, k, k, v, v, h1, w_o[0].astype(_BF))

    t = b * s
    rw = jnp.pad(router_w[0].astype(_F32), ((0, 0), (0, ROUTER_PAD - N_EXPERTS)))
    rw_hi = rw.astype(_BF)
    rw_lo = (rw - rw_hi.astype(_F32)).astype(_BF)
    rb = jnp.pad(router_b[0].astype(_F32), (0, ROUTER_PAD - N_EXPERTS)).reshape(1, ROUTER_PAD)

    bs, te = ROUTE_TILE, EXPERT_TILE
    nblk = t // bs
    cap = t
    smem = pl.BlockSpec(memory_space=pltpu.SMEM)
    tok = lambda w: pl.BlockSpec((bs, w), lambda ki, *_: (ki, 0))
    xs, meta, seg, totals = pl.pallas_call(
        _route_kernel,
        grid=(nblk,),
        in_specs=[tok(d), _const((1, d)), _const((d, ROUTER_PAD)), _const((d, ROUTER_PAD)), _const((1, ROUTER_PAD))],
        out_specs=[pl.BlockSpec(memory_space=pl.ANY), tok(ROUTER_PAD), smem, smem],
        out_shape=[jax.ShapeDtypeStruct((N_EXPERTS * cap, SORTED_WIDTH), _BF),
                   jax.ShapeDtypeStruct((t, ROUTER_PAD), _F32),
                   jax.ShapeDtypeStruct((nblk * 2 * N_EXPERTS,), jnp.int32),
                   jax.ShapeDtypeStruct((N_EXPERTS,), jnp.int32)],
        scratch_shapes=[pltpu.VMEM((2, SEG_ROWS, SORTED_WIDTH), _BF), pltpu.VMEM((EXPERT_TILE, SORTED_WIDTH), _BF),
                        pltpu.SMEM((N_EXPERTS,), jnp.int32), pltpu.SemaphoreType.DMA((2,))],
        compiler_params=pltpu.CompilerParams(dimension_semantics=("arbitrary",), vmem_limit_bytes=VMEM_LIMIT),
        name="route",
    )(h2.reshape(t, d), _row(ffn_norm[1]), rw_hi, rw_lo, rb)

    n_steps = (2 * t + nblk * N_EXPERTS * (SEG_ALIGN - 1)) // te + N_EXPERTS
    tiles_e = (totals + te - 1) // te
    tile_end = jnp.cumsum(tiles_e)
    n_tiles = tile_end[-1]
    step = jnp.minimum(jnp.arange(n_steps, dtype=jnp.int32), n_tiles - 1)
    tile_expert = jnp.sum(step[:, None] >= tile_end[None, :], axis=1).astype(jnp.int32)
    tile_block = (tile_expert * (cap // te) + step - (tile_end - tiles_e)[tile_expert]).astype(jnp.int32)

    xrow = lambda w: pl.BlockSpec((te, w), lambda i, te_r, tb_r, nt_r: (tb_r[i], 0))
    wexp = lambda r, c, cb: pl.BlockSpec((None, None, r, c), lambda i, te_r, tb_r, nt_r: (0, te_r[i], 0, cb))
    ys = pl.pallas_call(
        _expert_kernel,
        grid_spec=pltpu.PrefetchScalarGridSpec(
            num_scalar_prefetch=3,
            grid=(n_steps,),
            in_specs=[xrow(SORTED_WIDTH), wexp(d, D_FF_EXPERT, 0), wexp(d, D_FF_EXPERT, 1), wexp(D_FF_EXPERT, d, 0)],
            out_specs=xrow(d),
            scratch_shapes=[pltpu.VMEM((d, D_FF_EXPERT), _BF)] * 2 + [pltpu.VMEM((D_FF_EXPERT, d), _BF)],
        ),
        out_shape=jax.ShapeDtypeStruct((N_EXPERTS * cap, d), _BF),
        compiler_params=pltpu.CompilerParams(dimension_semantics=("arbitrary",), vmem_limit_bytes=VMEM_LIMIT),
        name="experts",
    )(tile_expert, tile_block, n_tiles.reshape(1).astype(jnp.int32), xs, we_gu, we_gu, we_down)

    out = pl.pallas_call(
        _combine_kernel,
        grid_spec=pltpu.PrefetchScalarGridSpec(
            num_scalar_prefetch=1,
            grid=(nblk,),
            in_specs=[tok(d),
                      pl.BlockSpec((None, None, bs, PLE_DIM), lambda ki, *_: (1, ki // (s // bs), ki % (s // bs), 0)),
                      tok(ROUTER_PAD), pl.BlockSpec(memory_space=pl.ANY),
                      _const((1, d)), _const((d, d)), _const((PLE_DIM, d))],
            out_specs=tok(d),
            scratch_shapes=[pltpu.VMEM((2, SEG_ROWS, d), _BF), pltpu.SemaphoreType.DMA((2,))],
        ),
        out_shape=jax.ShapeDtypeStruct((t, d), _F32),
        compiler_params=pltpu.CompilerParams(dimension_semantics=("arbitrary",), vmem_limit_bytes=VMEM_LIMIT),
        name="combine",
    )(seg, h2.reshape(t, d), p, meta, ys, _row(ple_gate_norm[1]),
      ple_gate_w[1].astype(_BF), ple_w[1].astype(_BF))
    return out.reshape(b, s, d)
```

```python
import functools

import jax
import jax.numpy as jnp
from jax import lax
from jax.experimental import pallas as pl
from jax.experimental.pallas import tpu as pltpu

D_MODEL = 1024
PLE_DIM = 256
POOL_WINDOWS = (2, 4, 8, 16)
POOL_GROUP = D_MODEL // len(POOL_WINDOWS)
POOL_HALO = 32
POOL_ROWS = 64
HEAD_DIM = 64
N_HEADS = 16
N_KV_HEADS = 4
KV_DIM = N_KV_HEADS * HEAD_DIM
WINDOW = 128
ROPE_THETA = 10000.0
D_FF = 2816
FF_CHUNK = 256
N_EXPERTS = 8
D_FF_EXPERT = 1024
EPS = 1e-6
LANES = 128
ROUTER_PAD = LANES

SEQ_TILE = 512
ATTN_TILE = 512
ROUTE_TILE = 512
EXPERT_TILE = 1024
SEG_ALIGN = 16
SEG_SIZES = tuple(ROUTE_TILE >> i for i in range((ROUTE_TILE // SEG_ALIGN).bit_length()))
FILL_SIZES = tuple((EXPERT_TILE // 2) >> i for i in range((EXPERT_TILE // 2 // SEG_ALIGN).bit_length()))
SEG_ROWS = 2 * ROUTE_TILE + N_EXPERTS * SEG_ALIGN
GATE_LANES = LANES
SORTED_WIDTH = D_MODEL + GATE_LANES
VMEM_LIMIT = 56 * 1024 * 1024

_BF = jnp.bfloat16
_F32 = jnp.float32


def _rms(x, g):
    return x * lax.rsqrt(jnp.mean(x * x, axis=-1, keepdims=True) + EPS) * g


def _sigmoid(x):
    return 0.5 * jnp.tanh(0.5 * x) + 0.5


def _dot(a, b):
    return jnp.dot(a, b, preferred_element_type=_F32)


def _head_meansq(x, hmat):
    x2 = (x * x).astype(_BF)
    width = hmat.shape[0]
    return jnp.concatenate([_dot(x2[:, c:c + width], hmat) for c in range(0, x.shape[1], width)], axis=1)


def _rope(x, cos, sin_signed):
    n = x.shape[1]
    reps = n // LANES
    lane = lax.broadcasted_iota(jnp.int32, (1, n), 1)
    first_half = (lane % HEAD_DIM) < (HEAD_DIM // 2)
    partner = jnp.where(first_half, pltpu.roll(x, n - HEAD_DIM // 2, 1), pltpu.roll(x, HEAD_DIM // 2, 1))
    c = jnp.concatenate([cos] * reps, axis=1)
    s = jnp.concatenate([sin_signed] * reps, axis=1)
    return x * c + partner * s


def _layer0_kernel(x_ref, p_ref, cos_ref, sin_ref, pool_norm, pool_w, pool_scale, ffn_norm, wgu, wd,
                   gate_norm, gate_w, ple_w, attn_norm, wq, q_gain, kv_norm, wkv, k_gain, hmat,
                   h_out, q_out, k_out, v_out, ubuf, sum_a, sum_b, h1_cur, uf_cur, h1_next, uf_next, h2_buf,
                   *, tiles_per_seq):
    step = pl.program_id(0)
    si = jnp.minimum(step, pl.num_programs(0) - 3) % tiles_per_seq
    ts = x_ref.shape[0]
    halo, end = POOL_HALO, POOL_HALO + ts
    n_chunks = D_FF // FF_CHUNK
    assert ts % POOL_ROWS == 0 and ts // POOL_ROWS <= n_chunks

    @pl.when(step == 0)
    def _():
        h1_cur[...] = jnp.zeros_like(h1_cur)
        uf_cur[...] = jnp.zeros_like(uf_cur)
        h2_buf[...] = jnp.zeros_like(h2_buf)

    @pl.when(si == 0)
    def _():
        ubuf[0:halo, :] = jnp.zeros((halo, D_MODEL), _F32)

    def first_stage_rows(r0, n):
        a, b = halo + r0, halo + r0 + n
        xr = x_ref[r0:r0 + n, :]
        u = _rms(xr, pool_norm[...])
        ubuf[a:b, :] = u
        g1, g2, g3 = POOL_GROUP, 2 * POOL_GROUP, 3 * POOL_GROUP
        m = n + 24
        s2 = ubuf[a - 24:b, :] + ubuf[a - 25:b - 1, :]
        sum_a[0:m, g1:] = s2[:, g1:]
        s4 = sum_a[8:m, g1:] + sum_a[6:m - 2, g1:]
        sum_b[8:m, g2:] = s4[:, g1:]
        s8 = sum_b[16:m, g2:] + sum_b[12:m - 4, g2:]
        sum_a[16:m, g3:] = s8[:, g1:]
        s16 = sum_a[24:m, g3:] + sum_a[16:m - 8, g3:]
        window_sums = (s2[24:, :g1], s4[16:, :g1], s8[8:, :g1], s16)
        pos = (si * ts + r0 + 1 + lax.broadcasted_iota(jnp.int32, (n, 1), 0)).astype(_F32)
        mixed = []
        for g, win in enumerate(POOL_WINDOWS):
            ug = u[:, g * POOL_GROUP:(g + 1) * POOL_GROUP]
            d = window_sums[g] / jnp.minimum(pos, float(win)) - ug
            mixed.append(_dot(d.astype(_BF), pool_w[g]))
        h_mixed = xr + jnp.concatenate(mixed, axis=1) * pool_scale[...]
        h1_next[r0:r0 + n, :] = h_mixed
        uf_next[r0:r0 + n, :] = _rms(h_mixed, ffn_norm[...]).astype(_BF)

    t = {}

    def embed_gate_matmul():
        t["h"] = h2_buf[...]
        t["gate_pre"] = _dot(_rms(t["h"], gate_norm[...]).astype(_BF), gate_w[...])

    def embed_and_unit_norm():
        h = t["h"] + _sigmoid(t["gate_pre"]) * _dot(p_ref[...].astype(_BF), ple_w[...])
        h_out[...] = h
        t["unit"] = h * lax.rsqrt(jnp.mean(h * h, axis=-1, keepdims=True) + EPS)

    def q_matmul():
        t["q"] = _dot((t["unit"] * attn_norm[...]).astype(_BF), wq[...])

    def q_norm_rope():
        q = t["q"] * lax.rsqrt(_head_meansq(t["q"], hmat[...]) + EPS) * q_gain[...]
        q_out[...] = _rope(q, cos_ref[...], sin_ref[...]).astype(_BF)

    def kv_matmul():
        t["kv"] = _dot((t["unit"] * kv_norm[...]).astype(_BF), wkv[...])

    def k_norm_rope():
        k = t["kv"][:, :KV_DIM]
        k = k * lax.rsqrt(_head_meansq(k, hmat[...]) + EPS) * k_gain[...]
        k_out[...] = _rope(k, cos_ref[...], sin_ref[...]).astype(_BF)
        v_out[...] = t["kv"][:, KV_DIM:].astype(_BF)

    third_stage = (embed_gate_matmul, embed_and_unit_norm, q_matmul, q_norm_rope, kv_matmul, k_norm_rope)
    assert len(third_stage) <= n_chunks

    uf = uf_cur[...]
    acc = jnp.zeros((ts, D_MODEL), _F32)
    for c in range(n_chunks):
        c0 = c * FF_CHUNK
        a = _dot(uf, wgu[:, c0:c0 + FF_CHUNK])
        g = _dot(uf, wgu[:, D_FF + c0:D_FF + c0 + FF_CHUNK])
        act = (a * _sigmoid(a) * g).astype(_BF)
        acc = acc + _dot(act, wd[c0:c0 + FF_CHUNK, :])
        if c < len(third_stage):
            third_stage[c]()
        if c * POOL_ROWS < ts:
            first_stage_rows(c * POOL_ROWS, POOL_ROWS)

    h2_buf[...] = h1_cur[...] + acc
    ubuf[0:halo, :] = ubuf[ts:end, :]
    h1_cur[...] = h1_next[...]
    uf_cur[...] = uf_next[...]


def _attn_kernel(sinks_ref, q_ref, kp_ref, kc_ref, vp_ref, vc_ref, h_ref, wo_ref, out_ref,
                 klo, khi, vlo, vhi, o_scr):
    i = pl.program_id(1)
    tq = q_ref.shape[0]
    nblk = tq // WINDOW
    lane = lax.broadcasted_iota(jnp.int32, (1, LANES), 1)
    low = lane < HEAD_DIM

    qi = lax.broadcasted_iota(jnp.int32, (2 * WINDOW, 2 * WINDOW), 0) % WINDOW
    kj = lax.broadcasted_iota(jnp.int32, (2 * WINDOW, 2 * WINDOW), 1)
    in_window = (kj > qi) & (kj <= qi + WINDOW)
    upper_rows = lax.broadcasted_iota(jnp.int32, (2 * WINDOW, 1), 0) >= WINDOW

    for g in range(N_KV_HEADS):
        slab = slice((g // 2) * LANES, (g // 2 + 1) * LANES)
        for src_p, src_c, lo_ref, hi_ref in ((kp_ref, kc_ref, klo, khi), (vp_ref, vc_ref, vlo, vhi)):
            t = jnp.concatenate([src_p[:, slab], src_c[:, slab]], axis=0).astype(_F32)
            r = pltpu.roll(t, HEAD_DIM, 1)
            in_low, in_high = (t, r) if g % 2 == 0 else (r, t)
            lo_ref[...] = jnp.where(low, in_low, 0.0).astype(_BF)
            hi_ref[...] = jnp.where(low, 0.0, in_high).astype(_BF)

        for j in range(nblk):
            qrows = pl.ds(j * WINDOW, WINDOW)
            krows = pl.ds(j * WINDOW, 2 * WINDOW)
            qg = jnp.concatenate([q_ref[qrows, (2 * g) * LANES:(2 * g + 1) * LANES],
                                  q_ref[qrows, (2 * g + 1) * LANES:(2 * g + 2) * LANES]], axis=0)
            mask = in_window & (kj >= jnp.where(i > 0, 0, WINDOW)) if j == 0 else in_window
            o = jnp.zeros((2 * WINDOW, LANES), _F32)
            for half, (k_ref, v_ref) in enumerate(((klo, vlo), (khi, vhi))):
                sink = jnp.where(upper_rows, sinks_ref[4 * g + 2 + half], sinks_ref[4 * g + half])
                s = lax.dot_general(qg, k_ref[krows, :], (((1,), (1,)), ((), ())),
                                    preferred_element_type=_F32)
                s = jnp.where(mask, s, -jnp.inf)
                m = jnp.maximum(jnp.max(s, axis=1, keepdims=True), sink)
                pr = jnp.exp(s - m)
                denom = jnp.sum(pr, axis=1, keepdims=True) + jnp.exp(sink - m)
                o = o + _dot(pr.astype(_BF), v_ref[krows, :]) / denom
            o_scr[qrows, (2 * g) * LANES:(2 * g + 1) * LANES] = o[:WINDOW].astype(_BF)
            o_scr[qrows, (2 * g + 1) * LANES:(2 * g + 2) * LANES] = o[WINDOW:].astype(_BF)

    out_ref[...] = h_ref[...] + _dot(o_scr[...], wo_ref[...])


def _segment_copies(rows, src_ref, src_base, dst_ref, dst_base, sem, sizes=SEG_SIZES):
    out = []
    for sz in sizes:
        done = rows & (-2 * sz)
        src = src_ref.at[pl.ds(pl.multiple_of(src_base + done, SEG_ALIGN), sz)]
        dst = dst_ref.at[pl.ds(pl.multiple_of(dst_base + done, SEG_ALIGN), sz)]
        out.append(((rows & sz) != 0, pltpu.make_async_copy(src, dst, sem)))
    return out


def _run(copies, op):
    for cond, cp in copies:
        @pl.when(cond)
        def _(cp=cp):
            getattr(cp, op)()


def _block_copies(seg_ref, blk, hbm_ref, buf_ref, sem, to_hbm):
    cap = hbm_ref.shape[0] // N_EXPERTS
    copies = []
    off = jnp.int32(0)
    for e in range(N_EXPERTS):
        base = e * cap + seg_ref[blk * 2 * N_EXPERTS + e]
        rows = seg_ref[blk * 2 * N_EXPERTS + N_EXPERTS + e]
        if to_hbm:
            copies += _segment_copies(rows, buf_ref, off, hbm_ref, base, sem)
        else:
            copies += _segment_copies(rows, hbm_ref, base, buf_ref, off, sem)
        off = off + rows
    return copies


def _route_kernel(h_ref, ffn_norm, rw_hi, rw_lo, rb, xs_hbm, meta_ref, seg_ref, tot_ref,
                  comp_scr, zero_scr, cum_ref, sem):
    k = pl.program_id(0)
    bs = h_ref.shape[0]
    cap = xs_hbm.shape[0] // N_EXPERTS
    lane = lax.broadcasted_iota(jnp.int32, (1, ROUTER_PAD), 1)

    @pl.when(k == 0)
    def _():
        for e in range(N_EXPERTS):
            cum_ref[e] = 0

    u = _rms(h_ref[...], ffn_norm[...])
    u_hi = u.astype(_BF)
    u_lo = (u - u_hi.astype(_F32)).astype(_BF)
    logits = _dot(u_hi, rw_hi[...]) + _dot(u_lo, rw_hi[...]) + _dot(u_hi, rw_lo[...]) + rb[...]
    logits = jnp.where(lane < N_EXPERTS, logits, -jnp.inf)
    m1 = jnp.max(logits, axis=1, keepdims=True)
    i1 = jnp.min(jnp.where(logits == m1, lane, ROUTER_PAD), axis=1, keepdims=True)
    rest = jnp.where(lane == i1, -jnp.inf, logits)
    m2 = jnp.max(rest, axis=1, keepdims=True)
    i2 = jnp.min(jnp.where(rest == m2, lane, ROUTER_PAD), axis=1, keepdims=True)
    t = jnp.exp(m2 - m1)
    w1 = 1.0 / (1.0 + t)
    w2 = t * w1

    a1 = lane == i1
    a2 = lane == i2
    assigned = jnp.where(a1 | a2, 1.0, 0.0)
    before = (lax.broadcasted_iota(jnp.int32, (bs, bs), 1) < lax.broadcasted_iota(jnp.int32, (bs, bs), 0))
    rank = _dot(jnp.where(before, 1.0, 0.0).astype(_BF), assigned.astype(_BF))
    count = jnp.sum(assigned, axis=0, keepdims=True).astype(jnp.int32)
    seg_off = jnp.zeros((1, ROUTER_PAD), _F32)
    off = jnp.int32(0)
    for e in range(N_EXPERTS):
        rows = ((count[0, e] + (SEG_ALIGN - 1)) // SEG_ALIGN) * SEG_ALIGN
        seg_off = jnp.where(lane == e, off.astype(_F32), seg_off)
        base = cum_ref[e]
        seg_ref[k * 2 * N_EXPERTS + e] = base
        seg_ref[k * 2 * N_EXPERTS + N_EXPERTS + e] = rows
        cum_ref[e] = base + rows
        off = off + rows
    dst1 = jnp.sum(jnp.where(a1, rank + seg_off, 0.0), axis=1, keepdims=True)
    dst2 = jnp.sum(jnp.where(a2, rank + seg_off, 0.0), axis=1, keepdims=True)
    meta = jnp.where(lane == 0, dst1, jnp.where(lane == 1, dst2, jnp.where(lane == 2, w1, jnp.where(lane == 3, w2, 0.0))))
    meta_ref[...] = meta

    by_token = meta.T
    row = lax.broadcasted_iota(jnp.int32, (SEG_ROWS, 1), 0).astype(_F32)
    from1 = row == by_token[0:1, :]
    from2 = row == by_token[1:2, :]
    onehot = jnp.where(from1 | from2, 1.0, 0.0).astype(_BF)
    gate = jnp.sum(jnp.where(from1, by_token[2:3, :], 0.0) + jnp.where(from2, by_token[3:4, :], 0.0),
                   axis=1, keepdims=True)
    hi = gate.astype(_BF).astype(_F32)
    mid = (gate - hi).astype(_BF).astype(_F32)
    gate_parts = jnp.where(lane == 0, hi, jnp.where(lane == 1, mid, jnp.where(lane == 2, gate - hi - mid, 0.0)))

    slot = k % 2
    comp = comp_scr.at[slot]
    comp[:, :D_MODEL] = _dot(onehot, u_hi).astype(_BF)
    comp[:, D_MODEL:] = gate_parts.astype(_BF)
    _run(_block_copies(seg_ref, k, xs_hbm, comp, sem.at[slot], to_hbm=True), "start")

    @pl.when(k > 0)
    def _():
        _run(_block_copies(seg_ref, k - 1, xs_hbm, comp_scr.at[1 - slot], sem.at[1 - slot], to_hbm=True), "wait")

    @pl.when(k == pl.num_programs(0) - 1)
    def _():
        _run(_block_copies(seg_ref, k, xs_hbm, comp, sem.at[slot], to_hbm=True), "wait")
        zero_scr[...] = jnp.zeros_like(zero_scr)
        fills = []
        for e in range(N_EXPERTS):
            total = cum_ref[e]
            tot_ref[e] = total
            fills += _segment_copies((-total) & (EXPERT_TILE - 1), zero_scr, 0, xs_hbm, e * cap + total,
                                     sem.at[slot], sizes=FILL_SIZES)
        _run(fills, "start")
        _run(fills, "wait")


def _expert_kernel(te_ref, tb_ref, nt_ref, x_ref, wa_f32, wg_f32, wd_f32, y_ref, wa, wg, wd):
    i = pl.program_id(0)

    @pl.when((i == 0) | (te_ref[i] != te_ref[jnp.maximum(i - 1, 0)]))
    def _():
        wa[...] = wa_f32[...].astype(_BF)
        wg[...] = wg_f32[...].astype(_BF)
        wd[...] = wd_f32[...].astype(_BF)

    @pl.when(i < nt_ref[0])
    def _():
        x = x_ref[:, :D_MODEL]
        gate = jnp.sum(x_ref[:, D_MODEL:].astype(_F32), axis=1, keepdims=True)
        a = _dot(x, wa[...])
        g = _dot(x, wg[...])
        act = (a * _sigmoid(a) * g).astype(_BF)
        y_ref[...] = (gate * _dot(act, wd[...])).astype(_BF)


def _combine_kernel(seg_ref, h_ref, p_ref, meta_ref, ys_hbm, gate_norm, gate_w, ple_w, out_ref, ycat, sem):
    k = pl.program_id(0)
    slot = k % 2
    fetch = lambda blk, s, op: _run(_block_copies(seg_ref, blk, ys_hbm, ycat.at[s], sem.at[s], to_hbm=False), op)

    @pl.when(k == 0)
    def _():
        ycat[...] = jnp.zeros_like(ycat)
        fetch(k, slot, "start")

    @pl.when(k + 1 < pl.num_programs(0))
    def _():
        fetch(k + 1, 1 - slot, "start")

    fetch(k, slot, "wait")

    pos = lax.broadcasted_iota(jnp.int32, (1, SEG_ROWS), 1).astype(_F32)
    meta = meta_ref[...]
    onehot = jnp.where((pos == meta[:, 0:1]) | (pos == meta[:, 1:2]), 1.0, 0.0).astype(_BF)
    h = h_ref[...] + _dot(onehot, ycat[slot])
    gate = _sigmoid(_dot(_rms(h, gate_norm[...]).astype(_BF), gate_w[...]))
    out_ref[...] = h + gate * _dot(p_ref[...].astype(_BF), ple_w[...])


def _const(shape):
    return pl.BlockSpec(shape, lambda *_: (0,) * len(shape), pipeline_mode=pl.Buffered(1))


def _row(v):
    return v.reshape(1, -1).astype(_F32)


def kernel(x, p, pool_norm, pool_w, pool_scale, kv_norm, w_kv, k_norm, attn_norm, w_q, q_norm, sinks, w_o,
           ffn_norm, w_gu, w_down, router_w, router_b, we_gu, we_down, ple_gate_norm, ple_gate_w, ple_w):
    b, s, d = x.shape
    ts, tq = SEQ_TILE, ATTN_TILE

    inv = ROPE_THETA ** (-jnp.arange(0, HEAD_DIM, 2, dtype=_F32) / HEAD_DIM)
    ang = jnp.arange(s, dtype=_F32)[:, None] * inv[None, :]
    cos_t = jnp.tile(jnp.cos(ang), (1, LANES // (HEAD_DIM // 2)))
    sin_t = jnp.tile(jnp.concatenate([-jnp.sin(ang), jnp.sin(ang)], axis=1), (1, LANES // HEAD_DIM))
    hid = jnp.arange(KV_DIM) // HEAD_DIM
    hmat = ((hid[:, None] == hid[None, :]) * (1.0 / HEAD_DIM)).astype(_BF)

    q_gain = jnp.tile(q_norm[0].astype(_F32) * (HEAD_DIM ** -0.5), N_HEADS).reshape(1, d)
    k_gain = jnp.tile(k_norm.astype(_F32), N_KV_HEADS).reshape(1, KV_DIM)

    tps = s // ts
    n_tiles0 = b * tps
    first = lambda st: jnp.minimum(st, n_tiles0 - 1)
    third = lambda st: jnp.maximum(st - 2, 0)
    tile3 = lambda w: pl.BlockSpec((None, ts, w), lambda st: (third(st) // tps, third(st) % tps, 0))
    h1, q, k, v = pl.pallas_call(
        functools.partial(_layer0_kernel, tiles_per_seq=tps),
        grid=(n_tiles0 + 2,),
        in_specs=[
            pl.BlockSpec((None, ts, d), lambda st: (first(st) // tps, first(st) % tps, 0)),
            pl.BlockSpec((None, None, ts, PLE_DIM), lambda st: (0, third(st) // tps, third(st) % tps, 0)),
            pl.BlockSpec((ts, LANES), lambda st: (third(st) % tps, 0)),
            pl.BlockSpec((ts, LANES), lambda st: (third(st) % tps, 0)),
            _const((1, d)), _const((len(POOL_WINDOWS), POOL_GROUP, POOL_GROUP)), _const((1, d)),
            _const((1, d)), _const((d, 2 * D_FF)), _const((D_FF, d)),
            _const((1, d)), _const((d, d)), _const((PLE_DIM, d)),
            _const((1, d)), _const((d, d)), _const((1, d)),
            _const((1, d)), _const((d, 2 * KV_DIM)), _const((1, KV_DIM)), _const((KV_DIM, KV_DIM)),
        ],
        out_specs=[tile3(d), tile3(d), tile3(KV_DIM), tile3(KV_DIM)],
        out_shape=[jax.ShapeDtypeStruct((b, s, d), _F32), jax.ShapeDtypeStruct((b, s, d), _BF),
                   jax.ShapeDtypeStruct((b, s, KV_DIM), _BF), jax.ShapeDtypeStruct((b, s, KV_DIM), _BF)],
        scratch_shapes=[pltpu.VMEM((POOL_HALO + ts, d), _F32)] + [pltpu.VMEM((POOL_ROWS + 24, d), _F32)] * 2
                       + [pltpu.VMEM((ts, d), _F32), pltpu.VMEM((ts, d), _BF)] * 2 + [pltpu.VMEM((ts, d), _F32)],
        compiler_params=pltpu.CompilerParams(dimension_semantics=("arbitrary",), vmem_limit_bytes=VMEM_LIMIT),
        name="layer0",
    )(x, p, cos_t, sin_t, _row(pool_norm[0]), pool_w[0].astype(_BF), _row(pool_scale[0]),
      _row(ffn_norm[0]), w_gu[0].astype(_BF), w_down[0].astype(_BF), _row(ple_gate_norm[0]), ple_gate_w[0].astype(_BF), ple_w[0].astype(_BF),
      _row(attn_norm[0]), w_q[0].astype(_BF), q_gain, _row(kv_norm), w_kv.astype(_BF), k_gain, hmat)

    blocks_per_tile = tq // WINDOW
    prev = lambda bi, qi, *_: (bi, jnp.maximum(qi * blocks_per_tile - 1, 0), 0)
    cur = lambda bi, qi, *_: (bi, qi, 0)
    h2 = pl.pallas_call(
        _attn_kernel,
        grid_spec=pltpu.PrefetchScalarGridSpec(
            num_scalar_prefetch=1,
            grid=(b, s // tq),
            in_specs=[
                pl.BlockSpec((None, tq, d), cur),
                pl.BlockSpec((None, WINDOW, KV_DIM), prev), pl.BlockSpec((None, tq, KV_DIM), cur),
                pl.BlockSpec((None, WINDOW, KV_DIM), prev), pl.BlockSpec((None, tq, KV_DIM), cur),
                pl.BlockSpec((None, tq, d), cur),
                pl.BlockSpec((d, d), lambda *_: (0, 0), pipeline_mode=pl.Buffered(1)),
            ],
            out_specs=pl.BlockSpec((None, tq, d), cur),
            scratch_shapes=[pltpu.VMEM((WINDOW + tq, LANES), _BF)] * 4 + [pltpu.VMEM((tq, d), _BF)],
        ),
        out_shape=jax.ShapeDtypeStruct((b, s, d), _F32),
        compiler_params=pltpu.CompilerParams(dimension_semantics=("arbitrary", "arbitrary"),
                                             vmem_limit_bytes=VMEM_LIMIT),
        name="attn",
    )(sinks[0].astype(_F32), q, k, k, v, v, h1, w_o[0].astype(_BF))

    t = b * s
    rw = jnp.pad(router_w[0].astype(_F32), ((0, 0), (0, ROUTER_PAD - N_EXPERTS)))
    rw_hi = rw.astype(_BF)
    rw_lo = (rw - rw_hi.astype(_F32)).astype(_BF)
    rb = jnp.pad(router_b[0].astype(_F32), (0, ROUTER_PAD - N_EXPERTS)).reshape(1, ROUTER_PAD)

    bs, te = ROUTE_TILE, EXPERT_TILE
    nblk = t // bs
    cap = t
    smem = pl.BlockSpec(memory_space=pltpu.SMEM)
    tok = lambda w: pl.BlockSpec((bs, w), lambda ki, *_: (ki, 0))
    xs, meta, seg, totals = pl.pallas_call(
        _route_kernel,
        grid=(nblk,),
        in_specs=[tok(d), _const((1, d)), _const((d, ROUTER_PAD)), _const((d, ROUTER_PAD)), _const((1, ROUTER_PAD))],
        out_specs=[pl.BlockSpec(memory_space=pl.ANY), tok(ROUTER_PAD), smem, smem],
        out_shape=[jax.ShapeDtypeStruct((N_EXPERTS * cap, SORTED_WIDTH), _BF),
                   jax.ShapeDtypeStruct((t, ROUTER_PAD), _F32),
                   jax.ShapeDtypeStruct((nblk * 2 * N_EXPERTS,), jnp.int32),
                   jax.ShapeDtypeStruct((N_EXPERTS,), jnp.int32)],
        scratch_shapes=[pltpu.VMEM((2, SEG_ROWS, SORTED_WIDTH), _BF), pltpu.VMEM((EXPERT_TILE, SORTED_WIDTH), _BF),
                        pltpu.SMEM((N_EXPERTS,), jnp.int32), pltpu.SemaphoreType.DMA((2,))],
        compiler_params=pltpu.CompilerParams(dimension_semantics=("arbitrary",), vmem_limit_bytes=VMEM_LIMIT),
        name="route",
    )(h2.reshape(t, d), _row(ffn_norm[1]), rw_hi, rw_lo, rb)

    n_steps = (2 * t + nblk * N_EXPERTS * (SEG_ALIGN - 1)) // te + N_EXPERTS
    tiles_e = (totals + te - 1) // te
    tile_end = jnp.cumsum(tiles_e)
    n_tiles = tile_end[-1]
    step = jnp.minimum(jnp.arange(n_steps, dtype=jnp.int32), n_tiles - 1)
    tile_expert = jnp.sum(step[:, None] >= tile_end[None, :], axis=1).astype(jnp.int32)
    tile_block = (tile_expert * (cap // te) + step - (tile_end - tiles_e)[tile_expert]).astype(jnp.int32)

    xrow = lambda w: pl.BlockSpec((te, w), lambda i, te_r, tb_r, nt_r: (tb_r[i], 0))
    wexp = lambda r, c, cb: pl.BlockSpec((None, None, r, c), lambda i, te_r, tb_r, nt_r: (0, te_r[i], 0, cb))
    ys = pl.pallas_call(
        _expert_kernel,
        grid_spec=pltpu.PrefetchScalarGridSpec(
            num_scalar_prefetch=3,
            grid=(n_steps,),
            in_specs=[xrow(SORTED_WIDTH), wexp(d, D_FF_EXPERT, 0), wexp(d, D_FF_EXPERT, 1), wexp(D_FF_EXPERT, d, 0)],
            out_specs=xrow(d),
            scratch_shapes=[pltpu.VMEM((d, D_FF_EXPERT), _BF)] * 2 + [pltpu.VMEM((D_FF_EXPERT, d), _BF)],
        ),
        out_shape=jax.ShapeDtypeStruct((N_EXPERTS * cap, d), _BF),
        compiler_params=pltpu.CompilerParams(dimension_semantics=("arbitrary",), vmem_limit_bytes=VMEM_LIMIT),
        name="experts",
    )(tile_expert, tile_block, n_tiles.reshape(1).astype(jnp.int32), xs, we_gu, we_gu, we_down)

    out = pl.pallas_call(
        _combine_kernel,
        grid_spec=pltpu.PrefetchScalarGridSpec(
            num_scalar_prefetch=1,
            grid=(nblk,),
            in_specs=[tok(d),
                      pl.BlockSpec((None, None, bs, PLE_DIM), lambda ki, *_: (1, ki // (s // bs), ki % (s // bs), 0)),
                      tok(ROUTER_PAD), pl.BlockSpec(memory_space=pl.ANY),
                      _const((1, d)), _const((d, d)), _const((PLE_DIM, d))],
            out_specs=tok(d),
            scratch_shapes=[pltpu.VMEM((2, SEG_ROWS, d), _BF), pltpu.SemaphoreType.DMA((2,))],
        ),
        out_shape=jax.ShapeDtypeStruct((t, d), _F32),
        compiler_params=pltpu.CompilerParams(dimension_semantics=("arbitrary",), vmem_limit_bytes=VMEM_LIMIT),
        name="combine",
    )(seg, h2.reshape(t, d), p, meta, ys, _row(ple_gate_norm[1]),
      ple_gate_w[1].astype(_BF), ple_w[1].astype(_BF))
    return out.reshape(b, s, d)
```

```python
import functools

import jax
import jax.numpy as jnp
from jax import lax
from jax.experimental import pallas as pl
from jax.experimental.pallas import tpu as pltpu

D_MODEL = 1024
PLE_DIM = 256
POOL_WINDOWS = (2, 4, 8, 16)
POOL_GROUP = D_MODEL // len(POOL_WINDOWS)
POOL_HALO = 32
HEAD_DIM = 64
N_HEADS = 16
N_KV_HEADS = 4
KV_DIM = N_KV_HEADS * HEAD_DIM
WINDOW = 128
ROPE_THETA = 10000.0
D_FF = 2816
FF_SPLITS = (0, 1280, D_FF)
N_EXPERTS = 8
D_FF_EXPERT = 1024
EPS = 1e-6
LANES = 128
ROUTER_PAD = LANES

SEQ_TILE = 512
ATTN_TILE = 512
ROUTE_TILE = 512
EXPERT_TILE = 1024
SEG_ALIGN = 16
SEG_SIZES = tuple(ROUTE_TILE >> i for i in range((ROUTE_TILE // SEG_ALIGN).bit_length()))
FILL_SIZES = tuple((EXPERT_TILE // 2) >> i for i in range((EXPERT_TILE // 2 // SEG_ALIGN).bit_length()))
SEG_ROWS = 2 * ROUTE_TILE + N_EXPERTS * SEG_ALIGN
GATE_LANES = LANES
SORTED_WIDTH = D_MODEL + GATE_LANES
VMEM_LIMIT = 56 * 1024 * 1024

_BF = jnp.bfloat16
_F32 = jnp.float32


def _rms(x, g):
    return x * lax.rsqrt(jnp.mean(x * x, axis=-1, keepdims=True) + EPS) * g


def _sigmoid(x):
    return 0.5 * jnp.tanh(0.5 * x) + 0.5


def _dot(a, b):
    return jnp.dot(a, b, preferred_element_type=_F32)


def _head_meansq(x, hmat):
    x2 = (x * x).astype(_BF)
    width = hmat.shape[0]
    return jnp.concatenate([_dot(x2[:, c:c + width], hmat) for c in range(0, x.shape[1], width)], axis=1)


def _rope(x, cos, sin_signed):
    n = x.shape[1]
    reps = n // LANES
    lane = lax.broadcasted_iota(jnp.int32, (1, n), 1)
    first_half = (lane % HEAD_DIM) < (HEAD_DIM // 2)
    partner = jnp.where(first_half, pltpu.roll(x, n - HEAD_DIM // 2, 1), pltpu.roll(x, HEAD_DIM // 2, 1))
    c = jnp.concatenate([cos] * reps, axis=1)
    s = jnp.concatenate([sin_signed] * reps, axis=1)
    return x * c + partner * s


def _layer0_kernel(x_ref, p_ref, cos_ref, sin_ref, pool_norm, pool_w, pool_scale, ffn_norm, wgu_t, wd_t,
                   gate_norm, gate_w, ple_w, attn_norm, wq, q_gain, kv_norm, wkv, k_gain, hmat,
                   h_out, q_out, k_out, v_out, ubuf, sum_a, sum_b):
    si = pl.program_id(1)
    ts = x_ref.shape[0]
    x = x_ref[...]
    halo, end = POOL_HALO, POOL_HALO + ts

    @pl.when(si == 0)
    def _():
        ubuf[0:halo, :] = jnp.zeros((halo, D_MODEL), _F32)

    u = _rms(x, pool_norm[...])
    ubuf[halo:end, :] = u
    g1, g2, g3 = POOL_GROUP, 2 * POOL_GROUP, 3 * POOL_GROUP
    s2 = ubuf[8:end, :] + ubuf[7:end - 1, :]
    sum_a[8:end, g1:] = s2[:, g1:]
    s4 = sum_a[16:end, g1:] + sum_a[14:end - 2, g1:]
    sum_b[16:end, g2:] = s4[:, g1:]
    s8 = sum_b[24:end, g2:] + sum_b[20:end - 4, g2:]
    sum_a[24:end, g3:] = s8[:, g1:]
    s16 = sum_a[32:end, g3:] + sum_a[24:end - 8, g3:]
    window_sums = (s2[halo - 8:, :g1], s4[halo - 16:, :g1], s8[halo - 24:, :g1], s16)
    pos = (si * ts + 1 + lax.broadcasted_iota(jnp.int32, (ts, 1), 0)).astype(_F32)
    mixed = []
    for g, win in enumerate(POOL_WINDOWS):
        ug = u[:, g * POOL_GROUP:(g + 1) * POOL_GROUP]
        d = window_sums[g] / jnp.minimum(pos, float(win)) - ug
        mixed.append(_dot(d.astype(_BF), pool_w[g]))
    h = x + jnp.concatenate(mixed, axis=1) * pool_scale[...]
    ubuf[0:halo, :] = ubuf[ts:end, :]

    uf = _rms(h, ffn_norm[...]).astype(_BF)
    by_token = (((1,), (1,)), ((), ()))
    acc_t = jnp.zeros((D_MODEL, ts), _F32)
    for c0, c1 in zip(FF_SPLITS[:-1], FF_SPLITS[1:]):
        a = lax.dot_general(wgu_t[c0:c1, :], uf, by_token, preferred_element_type=_F32)
        g = lax.dot_general(wgu_t[D_FF + c0:D_FF + c1, :], uf, by_token, preferred_element_type=_F32)
        act = (a * _sigmoid(a) * g).astype(_BF)
        acc_t = acc_t + _dot(wd_t[:, c0:c1], act)
    h = h + acc_t.T

    gate = _sigmoid(_dot(_rms(h, gate_norm[...]).astype(_BF), gate_w[...]))
    h = h + gate * _dot(p_ref[...].astype(_BF), ple_w[...])
    h_out[...] = h

    cos = cos_ref[...]
    sin = sin_ref[...]
    unit = h * lax.rsqrt(jnp.mean(h * h, axis=-1, keepdims=True) + EPS)
    q = _dot((unit * attn_norm[...]).astype(_BF), wq[...])
    q = q * lax.rsqrt(_head_meansq(q, hmat[...]) + EPS) * q_gain[...]
    q_out[...] = _rope(q, cos, sin).astype(_BF)
    kv = _dot((unit * kv_norm[...]).astype(_BF), wkv[...])
    k = kv[:, :KV_DIM]
    k = k * lax.rsqrt(_head_meansq(k, hmat[...]) + EPS) * k_gain[...]
    k_out[...] = _rope(k, cos, sin).astype(_BF)
    v_out[...] = kv[:, KV_DIM:].astype(_BF)


def _attn_kernel(sinks_ref, q_ref, kp_ref, kc_ref, vp_ref, vc_ref, h_ref, wo_ref, out_ref,
                 k2_scr, v2_scr, o_scr):
    i = pl.program_id(1)
    tq = q_ref.shape[0]
    nblk = tq // WINDOW
    lane = lax.broadcasted_iota(jnp.int32, (1, LANES), 1)
    low = lane < HEAD_DIM
    keep_low = jnp.where(low, 1.0, 0.0).astype(_BF)
    keep_high = jnp.where(low, 0.0, 1.0).astype(_BF)

    rows = 4 * WINDOW
    qi = lax.broadcasted_iota(jnp.int32, (rows, 2 * WINDOW), 0) % WINDOW
    kj = lax.broadcasted_iota(jnp.int32, (rows, 2 * WINDOW), 1)
    in_window = (kj > qi) & (kj <= qi + WINDOW)
    head_of_row = lax.broadcasted_iota(jnp.int32, (rows, 1), 0) // WINDOW

    for g in range(N_KV_HEADS):
        slab = slice((g // 2) * LANES, (g // 2 + 1) * LANES)
        for src_p, src_c, dst in ((kp_ref, kc_ref, k2_scr), (vp_ref, vc_ref, v2_scr)):
            t = jnp.concatenate([src_p[:, slab], src_c[:, slab]], axis=0).astype(_F32)
            r = pltpu.roll(t, HEAD_DIM, 1)
            dst[...] = (jnp.where(low, t, r) if g % 2 == 0 else jnp.where(low, r, t)).astype(_BF)
        heads = (4 * g, 4 * g + 2, 4 * g + 1, 4 * g + 3)
        sink = jnp.zeros((rows, 1), _F32)
        for r_idx, hd in enumerate(heads):
            sink = jnp.where(head_of_row == r_idx, sinks_ref[hd], sink)

        for j in range(nblk):
            qrows = pl.ds(j * WINDOW, WINDOW)
            krows = pl.ds(j * WINDOW, 2 * WINDOW)
            qa = q_ref[qrows, (2 * g) * LANES:(2 * g + 1) * LANES]
            qb = q_ref[qrows, (2 * g + 1) * LANES:(2 * g + 2) * LANES]
            q4 = jnp.concatenate([qa * keep_low, qb * keep_low, qa * keep_high, qb * keep_high], axis=0)
            mask = in_window & (kj >= jnp.where(i > 0, 0, WINDOW)) if j == 0 else in_window
            s = lax.dot_general(q4, k2_scr[krows, :], (((1,), (1,)), ((), ())), preferred_element_type=_F32)
            s = jnp.where(mask, s, -jnp.inf)
            m = jnp.maximum(jnp.max(s, axis=1, keepdims=True), sink)
            pr = jnp.exp(s - m)
            denom = jnp.sum(pr, axis=1, keepdims=True) + jnp.exp(sink - m)
            o4 = _dot(pr.astype(_BF), v2_scr[krows, :]) / denom
            o = jnp.where(low, o4[:2 * WINDOW], o4[2 * WINDOW:])
            o_scr[qrows, (2 * g) * LANES:(2 * g + 1) * LANES] = o[:WINDOW].astype(_BF)
            o_scr[qrows, (2 * g + 1) * LANES:(2 * g + 2) * LANES] = o[WINDOW:].astype(_BF)

    out_ref[...] = h_ref[...] + _dot(o_scr[...], wo_ref[...])


def _segment_copies(rows, src_ref, src_base, dst_ref, dst_base, sem, sizes=SEG_SIZES):
    out = []
    for sz in sizes:
        done = rows & (-2 * sz)
        src = src_ref.at[pl.ds(pl.multiple_of(src_base + done, SEG_ALIGN), sz)]
        dst = dst_ref.at[pl.ds(pl.multiple_of(dst_base + done, SEG_ALIGN), sz)]
        out.append(((rows & sz) != 0, pltpu.make_async_copy(src, dst, sem)))
    return out


def _run(copies, op):
    for cond, cp in copies:
        @pl.when(cond)
        def _(cp=cp):
            getattr(cp, op)()


def _block_copies(seg_ref, blk, hbm_ref, buf_ref, sem, to_hbm):
    cap = hbm_ref.shape[0] // N_EXPERTS
    copies = []
    off = jnp.int32(0)
    for e in range(N_EXPERTS):
        base = e * cap + seg_ref[blk * 2 * N_EXPERTS + e]
        rows = seg_ref[blk * 2 * N_EXPERTS + N_EXPERTS + e]
        if to_hbm:
            copies += _segment_copies(rows, buf_ref, off, hbm_ref, base, sem)
        else:
            copies += _segment_copies(rows, hbm_ref, base, buf_ref, off, sem)
        off = off + rows
    return copies


def _route_kernel(h_ref, ffn_norm, rw_hi, rw_lo, rb, xs_hbm, meta_ref, seg_ref, tot_ref,
                  comp_scr, zero_scr, cum_ref, sem):
    k = pl.program_id(0)
    bs = h_ref.shape[0]
    cap = xs_hbm.shape[0] // N_EXPERTS
    lane = lax.broadcasted_iota(jnp.int32, (1, ROUTER_PAD), 1)

    @pl.when(k == 0)
    def _():
        for e in range(N_EXPERTS):
            cum_ref[e] = 0

    u = _rms(h_ref[...], ffn_norm[...])
    u_hi = u.astype(_BF)
    u_lo = (u - u_hi.astype(_F32)).astype(_BF)
    logits = _dot(u_hi, rw_hi[...]) + _dot(u_lo, rw_hi[...]) + _dot(u_hi, rw_lo[...]) + rb[...]
    logits = jnp.where(lane < N_EXPERTS, logits, -jnp.inf)
    m1 = jnp.max(logits, axis=1, keepdims=True)
    i1 = jnp.min(jnp.where(logits == m1, lane, ROUTER_PAD), axis=1, keepdims=True)
    rest = jnp.where(lane == i1, -jnp.inf, logits)
    m2 = jnp.max(rest, axis=1, keepdims=True)
    i2 = jnp.min(jnp.where(rest == m2, lane, ROUTER_PAD), axis=1, keepdims=True)
    t = jnp.exp(m2 - m1)
    w1 = 1.0 / (1.0 + t)
    w2 = t * w1

    a1 = lane == i1
    a2 = lane == i2
    assigned = jnp.where(a1 | a2, 1.0, 0.0)
    before = (lax.broadcasted_iota(jnp.int32, (bs, bs), 1) < lax.broadcasted_iota(jnp.int32, (bs, bs), 0))
    rank = _dot(jnp.where(before, 1.0, 0.0).astype(_BF), assigned.astype(_BF))
    count = jnp.sum(assigned, axis=0, keepdims=True).astype(jnp.int32)
    seg_off = jnp.zeros((1, ROUTER_PAD), _F32)
    off = jnp.int32(0)
    for e in range(N_EXPERTS):
        rows = ((count[0, e] + (SEG_ALIGN - 1)) // SEG_ALIGN) * SEG_ALIGN
        seg_off = jnp.where(lane == e, off.astype(_F32), seg_off)
        base = cum_ref[e]
        seg_ref[k * 2 * N_EXPERTS + e] = base
        seg_ref[k * 2 * N_EXPERTS + N_EXPERTS + e] = rows
        cum_ref[e] = base + rows
        off = off + rows
    dst1 = jnp.sum(jnp.where(a1, rank + seg_off, 0.0), axis=1, keepdims=True)
    dst2 = jnp.sum(jnp.where(a2, rank + seg_off, 0.0), axis=1, keepdims=True)
    meta = jnp.where(lane == 0, dst1, jnp.where(lane == 1, dst2, jnp.where(lane == 2, w1, jnp.where(lane == 3, w2, 0.0))))
    meta_ref[...] = meta

    by_token = meta.T
    row = lax.broadcasted_iota(jnp.int32, (SEG_ROWS, 1), 0).astype(_F32)
    from1 = row == by_token[0:1, :]
    from2 = row == by_token[1:2, :]
    onehot = jnp.where(from1 | from2, 1.0, 0.0).astype(_BF)
    gate = jnp.sum(jnp.where(from1, by_token[2:3, :], 0.0) + jnp.where(from2, by_token[3:4, :], 0.0),
                   axis=1, keepdims=True)
    hi = gate.astype(_BF).astype(_F32)
    mid = (gate - hi).astype(_BF).astype(_F32)
    gate_parts = jnp.where(lane == 0, hi, jnp.where(lane == 1, mid, jnp.where(lane == 2, gate - hi - mid, 0.0)))

    slot = k % 2
    comp = comp_scr.at[slot]
    comp[:, :D_MODEL] = _dot(onehot, u_hi).astype(_BF)
    comp[:, D_MODEL:] = gate_parts.astype(_BF)
    _run(_block_copies(seg_ref, k, xs_hbm, comp, sem.at[slot], to_hbm=True), "start")

    @pl.when(k > 0)
    def _():
        _run(_block_copies(seg_ref, k - 1, xs_hbm, comp_scr.at[1 - slot], sem.at[1 - slot], to_hbm=True), "wait")

    @pl.when(k == pl.num_programs(0) - 1)
    def _():
        _run(_block_copies(seg_ref, k, xs_hbm, comp, sem.at[slot], to_hbm=True), "wait")
        zero_scr[...] = jnp.zeros_like(zero_scr)
        fills = []
        for e in range(N_EXPERTS):
            total = cum_ref[e]
            tot_ref[e] = total
            fills += _segment_copies((-total) & (EXPERT_TILE - 1), zero_scr, 0, xs_hbm, e * cap + total,
                                     sem.at[slot], sizes=FILL_SIZES)
        _run(fills, "start")
        _run(fills, "wait")


def _expert_kernel(te_ref, tb_ref, nt_ref, x_ref, wa_f32, wg_f32, wd_f32, y_ref, wa, wg, wd):
    i = pl.program_id(0)

    @pl.when((i == 0) | (te_ref[i] != te_ref[jnp.maximum(i - 1, 0)]))
    def _():
        wa[...] = wa_f32[...].astype(_BF)
        wg[...] = wg_f32[...].astype(_BF)
        wd[...] = wd_f32[...].astype(_BF)

    @pl.when(i < nt_ref[0])
    def _():
        x = x_ref[:, :D_MODEL]
        gate = jnp.sum(x_ref[:, D_MODEL:].astype(_F32), axis=1, keepdims=True)
        a = _dot(x, wa[...])
        g = _dot(x, wg[...])
        act = (a * _sigmoid(a) * g).astype(_BF)
        y_ref[...] = (gate * _dot(act, wd[...])).astype(_BF)


def _combine_kernel(seg_ref, h_ref, p_ref, meta_ref, ys_hbm, gate_norm, gate_w, ple_w, out_ref, ycat, sem):
    k = pl.program_id(0)
    slot = k % 2
    fetch = lambda blk, s, op: _run(_block_copies(seg_ref, blk, ys_hbm, ycat.at[s], sem.at[s], to_hbm=False), op)

    @pl.when(k == 0)
    def _():
        ycat[...] = jnp.zeros_like(ycat)
        fetch(k, slot, "start")

    @pl.when(k + 1 < pl.num_programs(0))
    def _():
        fetch(k + 1, 1 - slot, "start")

    fetch(k, slot, "wait")

    pos = lax.broadcasted_iota(jnp.int32, (1, SEG_ROWS), 1).astype(_F32)
    meta = meta_ref[...]
    onehot = jnp.where((pos == meta[:, 0:1]) | (pos == meta[:, 1:2]), 1.0, 0.0).astype(_BF)
    h = h_ref[...] + _dot(onehot, ycat[slot])
    gate = _sigmoid(_dot(_rms(h, gate_norm[...]).astype(_BF), gate_w[...]))
    out_ref[...] = h + gate * _dot(p_ref[...].astype(_BF), ple_w[...])


def _const(shape):
    return pl.BlockSpec(shape, lambda *_: (0,) * len(shape), pipeline_mode=pl.Buffered(1))


def _row(v):
    return v.reshape(1, -1).astype(_F32)


def kernel(x, p, pool_norm, pool_w, pool_scale, kv_norm, w_kv, k_norm, attn_norm, w_q, q_norm, sinks, w_o,
           ffn_norm, w_gu, w_down, router_w, router_b, we_gu, we_down, ple_gate_norm, ple_gate_w, ple_w):
    b, s, d = x.shape
    ts, tq = SEQ_TILE, ATTN_TILE

    inv = ROPE_THETA ** (-jnp.arange(0, HEAD_DIM, 2, dtype=_F32) / HEAD_DIM)
    ang = jnp.arange(s, dtype=_F32)[:, None] * inv[None, :]
    cos_t = jnp.tile(jnp.cos(ang), (1, LANES // (HEAD_DIM // 2)))
    sin_t = jnp.tile(jnp.concatenate([-jnp.sin(ang), jnp.sin(ang)], axis=1), (1, LANES // HEAD_DIM))
    hid = jnp.arange(KV_DIM) // HEAD_DIM
    hmat = ((hid[:, None] == hid[None, :]) * (1.0 / HEAD_DIM)).astype(_BF)

    q_gain = jnp.tile(q_norm[0].astype(_F32) * (HEAD_DIM ** -0.5), N_HEADS).reshape(1, d)
    k_gain = jnp.tile(k_norm.astype(_F32), N_KV_HEADS).reshape(1, KV_DIM)

    tile3 = lambda w: pl.BlockSpec((None, ts, w), lambda bi, si: (bi, si, 0))
    h1, q, k, v = pl.pallas_call(
        _layer0_kernel,
        grid=(b, s // ts),
        in_specs=[
            tile3(d), pl.BlockSpec((None, None, ts, PLE_DIM), lambda bi, si: (0, bi, si, 0)),
            pl.BlockSpec((ts, LANES), lambda bi, si: (si, 0)),
            pl.BlockSpec((ts, LANES), lambda bi, si: (si, 0)),
            _const((1, d)), _const((len(POOL_WINDOWS), POOL_GROUP, POOL_GROUP)), _const((1, d)),
            _const((1, d)), _const((2 * D_FF, d)), _const((d, D_FF)),
            _const((1, d)), _const((d, d)), _const((PLE_DIM, d)),
            _const((1, d)), _const((d, d)), _const((1, d)),
            _const((1, d)), _const((d, 2 * KV_DIM)), _const((1, KV_DIM)), _const((KV_DIM, KV_DIM)),
        ],
        out_specs=[tile3(d), tile3(d), tile3(KV_DIM), tile3(KV_DIM)],
        out_shape=[jax.ShapeDtypeStruct((b, s, d), _F32), jax.ShapeDtypeStruct((b, s, d), _BF),
                   jax.ShapeDtypeStruct((b, s, KV_DIM), _BF), jax.ShapeDtypeStruct((b, s, KV_DIM), _BF)],
        scratch_shapes=[pltpu.VMEM((POOL_HALO + ts, d), _F32)] * 3,
        compiler_params=pltpu.CompilerParams(dimension_semantics=("arbitrary", "arbitrary"),
                                             vmem_limit_bytes=VMEM_LIMIT),
        name="layer0",
    )(x, p, cos_t, sin_t, _row(pool_norm[0]), pool_w[0].astype(_BF), _row(pool_scale[0]),
      _row(ffn_norm[0]), w_gu[0].T.astype(_BF), w_down[0].T.astype(_BF), _row(ple_gate_norm[0]), ple_gate_w[0].astype(_BF), ple_w[0].astype(_BF),
      _row(attn_norm[0]), w_q[0].astype(_BF), q_gain, _row(kv_norm), w_kv.astype(_BF), k_gain, hmat)

    blocks_per_tile = tq // WINDOW
    prev = lambda bi, qi, *_: (bi, jnp.maximum(qi * blocks_per_tile - 1, 0), 0)
    cur = lambda bi, qi, *_: (bi, qi, 0)
    h2 = pl.pallas_call(
        _attn_kernel,
        grid_spec=pltpu.PrefetchScalarGridSpec(
            num_scalar_prefetch=1,
            grid=(b, s // tq),
            in_specs=[
                pl.BlockSpec((None, tq, d), cur),
                pl.BlockSpec((None, WINDOW, KV_DIM), prev), pl.BlockSpec((None, tq, KV_DIM), cur),
                pl.BlockSpec((None, WINDOW, KV_DIM), prev), pl.BlockSpec((None, tq, KV_DIM), cur),
                pl.BlockSpec((None, tq, d), cur),
                pl.BlockSpec((d, d), lambda *_: (0, 0), pipeline_mode=pl.Buffered(1)),
            ],
            out_specs=pl.BlockSpec((None, tq, d), cur),
            scratch_shapes=[pltpu.VMEM((WINDOW + tq, LANES), _BF)] * 2 + [pltpu.VMEM((tq, d), _BF)],
        ),
        out_shape=jax.ShapeDtypeStruct((b, s, d), _F32),
        compiler_params=pltpu.CompilerParams(dimension_semantics=("arbitrary", "arbitrary"),
                                             vmem_limit_bytes=VMEM_LIMIT),
        name="attn",
    )(sinks[0].astype(_F32), q, k, k, v, v, h1, w_o[0].astype(_BF))

    t = b * s
    rw = jnp.pad(router_w[0].astype(_F32), ((0, 0), (0, ROUTER_PAD - N_EXPERTS)))
    rw_hi = rw.astype(_BF)
    rw_lo = (rw - rw_hi.astype(_F32)).astype(_BF)
    rb = jnp.pad(router_b[0].astype(_F32), (0, ROUTER_PAD - N_EXPERTS)).reshape(1, ROUTER_PAD)

    bs, te = ROUTE_TILE, EXPERT_TILE
    nblk = t // bs
    cap = t
    smem = pl.BlockSpec(memory_space=pltpu.SMEM)
    tok = lambda w: pl.BlockSpec((bs, w), lambda ki, *_: (ki, 0))
    xs, meta, seg, totals = pl.pallas_call(
        _route_kernel,
        grid=(nblk,),
        in_specs=[tok(d), _const((1, d)), _const((d, ROUTER_PAD)), _const((d, ROUTER_PAD)), _const((1, ROUTER_PAD))],
        out_specs=[pl.BlockSpec(memory_space=pl.ANY), tok(ROUTER_PAD), smem, smem],
        out_shape=[jax.ShapeDtypeStruct((N_EXPERTS * cap, SORTED_WIDTH), _BF),
                   jax.ShapeDtypeStruct((t, ROUTER_PAD), _F32),
                   jax.ShapeDtypeStruct((nblk * 2 * N_EXPERTS,), jnp.int32),
                   jax.ShapeDtypeStruct((N_EXPERTS,), jnp.int32)],
        scratch_shapes=[pltpu.VMEM((2, SEG_ROWS, SORTED_WIDTH), _BF), pltpu.VMEM((EXPERT_TILE, SORTED_WIDTH), _BF),
                        pltpu.SMEM((N_EXPERTS,), jnp.int32), pltpu.SemaphoreType.DMA((2,))],
        compiler_params=pltpu.CompilerParams(dimension_semantics=("arbitrary",), vmem_limit_bytes=VMEM_LIMIT),
        name="route",
    )(h2.reshape(t, d), _row(ffn_norm[1]), rw_hi, rw_lo, rb)

    n_steps = (2 * t + nblk * N_EXPERTS * (SEG_ALIGN - 1)) // te + N_EXPERTS
    tiles_e = (totals + te - 1) // te
    tile_end = jnp.cumsum(tiles_e)
    n_tiles = tile_end[-1]
    step = jnp.minimum(jnp.arange(n_steps, dtype=jnp.int32), n_tiles - 1)
    tile_expert = jnp.sum(step[:, None] >= tile_end[None, :], axis=1).astype(jnp.int32)
    tile_block = (tile_expert * (cap // te) + step - (tile_end - tiles_e)[tile_expert]).astype(jnp.int32)

    xrow = lambda w: pl.BlockSpec((te, w), lambda i, te_r, tb_r, nt_r: (tb_r[i], 0))
    wexp = lambda r, c, cb: pl.BlockSpec((None, None, r, c), lambda i, te_r, tb_r, nt_r: (0, te_r[i], 0, cb))
    ys = pl.pallas_call(
        _expert_kernel,
        grid_spec=pltpu.PrefetchScalarGridSpec(
            num_scalar_prefetch=3,
            grid=(n_steps,),
            in_specs=[xrow(SORTED_WIDTH), wexp(d, D_FF_EXPERT, 0), wexp(d, D_FF_EXPERT, 1), wexp(D_FF_EXPERT, d, 0)],
            out_specs=xrow(d),
            scratch_shapes=[pltpu.VMEM((d, D_FF_EXPERT), _BF)] * 2 + [pltpu.VMEM((D_FF_EXPERT, d), _BF)],
        ),
        out_shape=jax.ShapeDtypeStruct((N_EXPERTS * cap, d), _BF),
        compiler_params=pltpu.CompilerParams(dimension_semantics=("arbitrary",), vmem_limit_bytes=VMEM_LIMIT),
        name="experts",
    )(tile_expert, tile_block, n_tiles.reshape(1).astype(jnp.int32), xs, we_gu, we_gu, we_down)

    out = pl.pallas_call(
        _combine_kernel,
        grid_spec=pltpu.PrefetchScalarGridSpec(
            num_scalar_prefetch=1,
            grid=(nblk,),
            in_specs=[tok(d),
                      pl.BlockSpec((None, None, bs, PLE_DIM), lambda ki, *_: (1, ki // (s // bs), ki % (s // bs), 0)),
                      tok(ROUTER_PAD), pl.BlockSpec(memory_space=pl.ANY),
                      _const((1, d)), _const((d, d)), _const((PLE_DIM, d))],
            out_specs=tok(d),
            scratch_shapes=[pltpu.VMEM((2, SEG_ROWS, d), _BF), pltpu.SemaphoreType.DMA((2,))],
        ),
        out_shape=jax.ShapeDtypeStruct((t, d), _F32),
        compiler_params=pltpu.CompilerParams(dimension_semantics=("arbitrary",), vmem_limit_bytes=VMEM_LIMIT),
        name="combine",
    )(seg, h2.reshape(t, d), p, meta, ys, _row(ple_gate_norm[1]),
      ple_gate_w[1].astype(_BF), ple_w[1].astype(_BF))
    return out.reshape(b, s, d)
```

```python
import functools

import jax
import jax.numpy as jnp
from jax import lax
from jax.experimental import pallas as pl
from jax.experimental.pallas import tpu as pltpu

D_MODEL = 1024
PLE_DIM = 256
POOL_WINDOWS = (2, 4, 8, 16)
POOL_GROUP = D_MODEL // len(POOL_WINDOWS)
POOL_HALO = 32
POOL_ROWS = 64
HEAD_DIM = 64
N_HEADS = 16
N_KV_HEADS = 4
KV_DIM = N_KV_HEADS * HEAD_DIM
WINDOW = 128
ROPE_THETA = 10000.0
D_FF = 2816
FF_CHUNK = 256
N_EXPERTS = 8
D_FF_EXPERT = 1024
EPS = 1e-6
LANES = 128
ROUTER_PAD = LANES

SEQ_TILE = 512
ATTN_TILE = 1024
ROUTE_TILE = 512
EXPERT_TILE = 1024
SEG_ALIGN = 16
SEG_SIZES = tuple(ROUTE_TILE >> i for i in range((ROUTE_TILE // SEG_ALIGN).bit_length()))
FILL_SIZES = tuple((EXPERT_TILE // 2) >> i for i in range((EXPERT_TILE // 2 // SEG_ALIGN).bit_length()))
SEG_ROWS = 2 * ROUTE_TILE + N_EXPERTS * SEG_ALIGN
GATE_LANES = LANES
SORTED_WIDTH = D_MODEL + GATE_LANES
VMEM_LIMIT = 60 * 1024 * 1024

_BF = jnp.bfloat16
_F32 = jnp.float32


def _rms(x, g):
    return x * lax.rsqrt(jnp.mean(x * x, axis=-1, keepdims=True) + EPS) * g


def _sigmoid(x):
    return 0.5 * jnp.tanh(0.5 * x) + 0.5


def _dot(a, b):
    return jnp.dot(a, b, preferred_element_type=_F32)


def _head_meansq(x, hmat):
    x2 = (x * x).astype(_BF)
    width = hmat.shape[0]
    return jnp.concatenate([_dot(x2[:, c:c + width], hmat) for c in range(0, x.shape[1], width)], axis=1)


def _rope(x, cos, sin_signed):
    n = x.shape[1]
    reps = n // LANES
    lane = lax.broadcasted_iota(jnp.int32, (1, n), 1)
    first_half = (lane % HEAD_DIM) < (HEAD_DIM // 2)
    partner = jnp.where(first_half, pltpu.roll(x, n - HEAD_DIM // 2, 1), pltpu.roll(x, HEAD_DIM // 2, 1))
    c = jnp.concatenate([cos] * reps, axis=1)
    s = jnp.concatenate([sin_signed] * reps, axis=1)
    return x * c + partner * s


def _layer0_kernel(x_ref, p_ref, cos_ref, sin_ref, pool_norm, pool_w, pool_scale, ffn_norm, wgu, wd,
                   gate_norm, gate_w, ple_w, attn_norm, wq, q_gain, kv_norm, wkv, k_gain, hmat,
                   h_out, q_out, k_out, v_out, ubuf, sum_a, sum_b, h1_cur, uf_cur, h1_next, uf_next, h2_buf,
                   *, tiles_per_seq):
    step = pl.program_id(0)
    si = jnp.minimum(step, pl.num_programs(0) - 3) % tiles_per_seq
    ts = x_ref.shape[0]
    halo, end = POOL_HALO, POOL_HALO + ts
    n_chunks = D_FF // FF_CHUNK
    assert ts % POOL_ROWS == 0 and ts // POOL_ROWS <= n_chunks

    @pl.when(step == 0)
    def _():
        h1_cur[...] = jnp.zeros_like(h1_cur)
        uf_cur[...] = jnp.zeros_like(uf_cur)
        h2_buf[...] = jnp.zeros_like(h2_buf)

    @pl.when(si == 0)
    def _():
        ubuf[0:halo, :] = jnp.zeros((halo, D_MODEL), _F32)

    def first_stage_rows(r0, n):
        a, b = halo + r0, halo + r0 + n
        xr = x_ref[r0:r0 + n, :]
        u = _rms(xr, pool_norm[...])
        ubuf[a:b, :] = u
        g1, g2, g3 = POOL_GROUP, 2 * POOL_GROUP, 3 * POOL_GROUP
        m = n + 24
        s2 = ubuf[a - 24:b, :] + ubuf[a - 25:b - 1, :]
        sum_a[0:m, g1:] = s2[:, g1:]
        s4 = sum_a[8:m, g1:] + sum_a[6:m - 2, g1:]
        sum_b[8:m, g2:] = s4[:, g1:]
        s8 = sum_b[16:m, g2:] + sum_b[12:m - 4, g2:]
        sum_a[16:m, g3:] = s8[:, g1:]
        s16 = sum_a[24:m, g3:] + sum_a[16:m - 8, g3:]
        window_sums = (s2[24:, :g1], s4[16:, :g1], s8[8:, :g1], s16)
        pos = (si * ts + r0 + 1 + lax.broadcasted_iota(jnp.int32, (n, 1), 0)).astype(_F32)
        mixed = []
        for g, win in enumerate(POOL_WINDOWS):
            ug = u[:, g * POOL_GROUP:(g + 1) * POOL_GROUP]
            d = window_sums[g] / jnp.minimum(pos, float(win)) - ug
            mixed.append(_dot(d.astype(_BF), pool_w[g]))
        h_mixed = xr + jnp.concatenate(mixed, axis=1) * pool_scale[...]
        h1_next[r0:r0 + n, :] = h_mixed
        uf_next[r0:r0 + n, :] = _rms(h_mixed, ffn_norm[...]).astype(_BF)

    t = {}

    def embed_gate_matmul():
        t["h"] = h2_buf[...]
        t["gate_pre"] = _dot(_rms(t["h"], gate_norm[...]).astype(_BF), gate_w[...])

    def embed_and_unit_norm():
        h = t["h"] + _sigmoid(t["gate_pre"]) * _dot(p_ref[...].astype(_BF), ple_w[...])
        h_out[...] = h
        t["unit"] = h * lax.rsqrt(jnp.mean(h * h, axis=-1, keepdims=True) + EPS)

    def q_matmul():
        t["q"] = _dot((t["unit"] * attn_norm[...]).astype(_BF), wq[...])

    def q_norm_rope():
        q = t["q"] * lax.rsqrt(_head_meansq(t["q"], hmat[...]) + EPS) * q_gain[...]
        q_out[...] = _rope(q, cos_ref[...], sin_ref[...]).astype(_BF)

    def kv_matmul():
        t["kv"] = _dot((t["unit"] * kv_norm[...]).astype(_BF), wkv[...])

    def k_norm_rope():
        k = t["kv"][:, :KV_DIM]
        k = k * lax.rsqrt(_head_meansq(k, hmat[...]) + EPS) * k_gain[...]
        k_out[...] = _rope(k, cos_ref[...], sin_ref[...]).astype(_BF)
        v_out[...] = t["kv"][:, KV_DIM:].astype(_BF)

    third_stage = (embed_gate_matmul, embed_and_unit_norm, q_matmul, q_norm_rope, kv_matmul, k_norm_rope)
    assert len(third_stage) <= n_chunks

    uf = uf_cur[...]
    acc = jnp.zeros((ts, D_MODEL), _F32)
    for c in range(n_chunks):
        c0 = c * FF_CHUNK
        a = _dot(uf, wgu[:, c0:c0 + FF_CHUNK])
        g = _dot(uf, wgu[:, D_FF + c0:D_FF + c0 + FF_CHUNK])
        act = (a * _sigmoid(a) * g).astype(_BF)
        acc = acc + _dot(act, wd[c0:c0 + FF_CHUNK, :])
        if c < len(third_stage):
            third_stage[c]()
        if c * POOL_ROWS < ts:
            first_stage_rows(c * POOL_ROWS, POOL_ROWS)

    h2_buf[...] = h1_cur[...] + acc
    ubuf[0:halo, :] = ubuf[ts:end, :]
    h1_cur[...] = h1_next[...]
    uf_cur[...] = uf_next[...]


def _attn_kernel(sinks_ref, q_ref, kp_ref, kc_ref, vp_ref, vc_ref, h_ref, wo_ref, out_ref,
                 klo, khi, vlo, vhi, o_scr):
    i = pl.program_id(1)
    tq = q_ref.shape[0]
    nblk = tq // WINDOW
    lane = lax.broadcasted_iota(jnp.int32, (1, LANES), 1)
    low = lane < HEAD_DIM

    qi = lax.broadcasted_iota(jnp.int32, (2 * WINDOW, 2 * WINDOW), 0) % WINDOW
    kj = lax.broadcasted_iota(jnp.int32, (2 * WINDOW, 2 * WINDOW), 1)
    in_window = (kj > qi) & (kj <= qi + WINDOW)
    upper_rows = lax.broadcasted_iota(jnp.int32, (2 * WINDOW, 1), 0) >= WINDOW

    for g in range(N_KV_HEADS):
        slab = slice((g // 2) * LANES, (g // 2 + 1) * LANES)
        for src_p, src_c, lo_ref, hi_ref in ((kp_ref, kc_ref, klo, khi), (vp_ref, vc_ref, vlo, vhi)):
            t = jnp.concatenate([src_p[:, slab], src_c[:, slab]], axis=0).astype(_F32)
            r = pltpu.roll(t, HEAD_DIM, 1)
            in_low, in_high = (t, r) if g % 2 == 0 else (r, t)
            lo_ref[...] = jnp.where(low, in_low, 0.0).astype(_BF)
            hi_ref[...] = jnp.where(low, 0.0, in_high).astype(_BF)

        for j in range(nblk):
            qrows = pl.ds(j * WINDOW, WINDOW)
            krows = pl.ds(j * WINDOW, 2 * WINDOW)
            qg = jnp.concatenate([q_ref[qrows, (2 * g) * LANES:(2 * g + 1) * LANES],
                                  q_ref[qrows, (2 * g + 1) * LANES:(2 * g + 2) * LANES]], axis=0)
            mask = in_window & (kj >= jnp.where(i > 0, 0, WINDOW)) if j == 0 else in_window
            o = jnp.zeros((2 * WINDOW, LANES), _F32)
            for half, (k_ref, v_ref) in enumerate(((klo, vlo), (khi, vhi))):
                sink = jnp.where(upper_rows, sinks_ref[4 * g + 2 + half], sinks_ref[4 * g + half])
                s = lax.dot_general(qg, k_ref[krows, :], (((1,), (1,)), ((), ())),
                                    preferred_element_type=_F32)
                s = jnp.where(mask, s, -jnp.inf)
                m = jnp.maximum(jnp.max(s, axis=1, keepdims=True), sink)
                pr = jnp.exp(s - m)
                denom = jnp.sum(pr, axis=1, keepdims=True) + jnp.exp(sink - m)
                o = o + _dot(pr.astype(_BF), v_ref[krows, :]) / denom
            o_scr[qrows, (2 * g) * LANES:(2 * g + 1) * LANES] = o[:WINDOW].astype(_BF)
            o_scr[qrows, (2 * g + 1) * LANES:(2 * g + 2) * LANES] = o[WINDOW:].astype(_BF)

    out_ref[...] = h_ref[...] + _dot(o_scr[...], wo_ref[...])


def _segment_copies(rows, src_ref, src_base, dst_ref, dst_base, sem, sizes=SEG_SIZES):
    out = []
    for sz in sizes:
        done = rows & (-2 * sz)
        src = src_ref.at[pl.ds(pl.multiple_of(src_base + done, SEG_ALIGN), sz)]
        dst = dst_ref.at[pl.ds(pl.multiple_of(dst_base + done, SEG_ALIGN), sz)]
        out.append(((rows & sz) != 0, pltpu.make_async_copy(src, dst, sem)))
    return out


def _run(copies, op):
    for cond, cp in copies:
        @pl.when(cond)
        def _(cp=cp):
            getattr(cp, op)()


def _block_copies(seg_ref, blk, hbm_ref, buf_ref, sem, to_hbm):
    cap = hbm_ref.shape[0] // N_EXPERTS
    copies = []
    off = jnp.int32(0)
    for e in range(N_EXPERTS):
        base = e * cap + seg_ref[blk * 2 * N_EXPERTS + e]
        rows = seg_ref[blk * 2 * N_EXPERTS + N_EXPERTS + e]
        if to_hbm:
            copies += _segment_copies(rows, buf_ref, off, hbm_ref, base, sem)
        else:
            copies += _segment_copies(rows, hbm_ref, base, buf_ref, off, sem)
        off = off + rows
    return copies


def _route_kernel(h_ref, ffn_norm, rw_hi, rw_lo, rb, xs_hbm, meta_ref, seg_ref, tot_ref,
                  comp_scr, zero_scr, cum_ref, sem):
    k = pl.program_id(0)
    bs = h_ref.shape[0]
    cap = xs_hbm.shape[0] // N_EXPERTS
    lane = lax.broadcasted_iota(jnp.int32, (1, ROUTER_PAD), 1)

    @pl.when(k == 0)
    def _():
        for e in range(N_EXPERTS):
            cum_ref[e] = 0

    u = _rms(h_ref[...], ffn_norm[...])
    u_hi = u.astype(_BF)
    u_lo = (u - u_hi.astype(_F32)).astype(_BF)
    logits = _dot(u_hi, rw_hi[...]) + _dot(u_lo, rw_hi[...]) + _dot(u_hi, rw_lo[...]) + rb[...]
    logits = jnp.where(lane < N_EXPERTS, logits, -jnp.inf)
    m1 = jnp.max(logits, axis=1, keepdims=True)
    i1 = jnp.min(jnp.where(logits == m1, lane, ROUTER_PAD), axis=1, keepdims=True)
    rest = jnp.where(lane == i1, -jnp.inf, logits)
    m2 = jnp.max(rest, axis=1, keepdims=True)
    i2 = jnp.min(jnp.where(rest == m2, lane, ROUTER_PAD), axis=1, keepdims=True)
    t = jnp.exp(m2 - m1)
    w1 = 1.0 / (1.0 + t)
    w2 = t * w1

    a1 = lane == i1
    a2 = lane == i2
    assigned = jnp.where(a1 | a2, 1.0, 0.0)
    before = (lax.broadcasted_iota(jnp.int32, (bs, bs), 1) < lax.broadcasted_iota(jnp.int32, (bs, bs), 0))
    rank = _dot(jnp.where(before, 1.0, 0.0).astype(_BF), assigned.astype(_BF))
    count = jnp.sum(assigned, axis=0, keepdims=True).astype(jnp.int32)
    seg_off = jnp.zeros((1, ROUTER_PAD), _F32)
    off = jnp.int32(0)
    for e in range(N_EXPERTS):
        rows = ((count[0, e] + (SEG_ALIGN - 1)) // SEG_ALIGN) * SEG_ALIGN
        seg_off = jnp.where(lane == e, off.astype(_F32), seg_off)
        base = cum_ref[e]
        seg_ref[k * 2 * N_EXPERTS + e] = base
        seg_ref[k * 2 * N_EXPERTS + N_EXPERTS + e] = rows
        cum_ref[e] = base + rows
        off = off + rows
    dst1 = jnp.sum(jnp.where(a1, rank + seg_off, 0.0), axis=1, keepdims=True)
    dst2 = jnp.sum(jnp.where(a2, rank + seg_off, 0.0), axis=1, keepdims=True)
    meta = jnp.where(lane == 0, dst1, jnp.where(lane == 1, dst2, jnp.where(lane == 2, w1, jnp.where(lane == 3, w2, 0.0))))
    meta_ref[...] = meta

    by_token = meta.T
    row = lax.broadcasted_iota(jnp.int32, (SEG_ROWS, 1), 0).astype(_F32)
    from1 = row == by_token[0:1, :]
    from2 = row == by_token[1:2, :]
    onehot = jnp.where(from1 | from2, 1.0, 0.0).astype(_BF)
    gate = jnp.sum(jnp.where(from1, by_token[2:3, :], 0.0) + jnp.where(from2, by_token[3:4, :], 0.0),
                   axis=1, keepdims=True)
    hi = gate.astype(_BF).astype(_F32)
    mid = (gate - hi).astype(_BF).astype(_F32)
    gate_parts = jnp.where(lane == 0, hi, jnp.where(lane == 1, mid, jnp.where(lane == 2, gate - hi - mid, 0.0)))

    slot = k % 2
    comp = comp_scr.at[slot]
    comp[:, :D_MODEL] = _dot(onehot, u_hi).astype(_BF)
    comp[:, D_MODEL:] = gate_parts.astype(_BF)
    _run(_block_copies(seg_ref, k, xs_hbm, comp, sem.at[slot], to_hbm=True), "start")

    @pl.when(k > 0)
    def _():
        _run(_block_copies(seg_ref, k - 1, xs_hbm, comp_scr.at[1 - slot], sem.at[1 - slot], to_hbm=True), "wait")

    @pl.when(k == pl.num_programs(0) - 1)
    def _():
        _run(_block_copies(seg_ref, k, xs_hbm, comp, sem.at[slot], to_hbm=True), "wait")
        zero_scr[...] = jnp.zeros_like(zero_scr)
        fills = []
        for e in range(N_EXPERTS):
            total = cum_ref[e]
            tot_ref[e] = total
            fills += _segment_copies((-total) & (EXPERT_TILE - 1), zero_scr, 0, xs_hbm, e * cap + total,
                                     sem.at[slot], sizes=FILL_SIZES)
        _run(fills, "start")
        _run(fills, "wait")


def _expert_kernel(te_ref, tb_ref, nt_ref, x_ref, wa_f32, wg_f32, wd_f32, y_ref, wa, wg, wd):
    i = pl.program_id(0)

    @pl.when((i == 0) | (te_ref[i] != te_ref[jnp.maximum(i - 1, 0)]))
    def _():
        wa[...] = wa_f32[...].astype(_BF)
        wg[...] = wg_f32[...].astype(_BF)
        wd[...] = wd_f32[...].astype(_BF)

    @pl.when(i < nt_ref[0])
    def _():
        x = x_ref[:, :D_MODEL]
        gate = jnp.sum(x_ref[:, D_MODEL:].astype(_F32), axis=1, keepdims=True)
        a = _dot(x, wa[...])
        g = _dot(x, wg[...])
        act = (a * _sigmoid(a) * g).astype(_BF)
        y_ref[...] = (gate * _dot(act, wd[...])).astype(_BF)


def _combine_kernel(seg_ref, h_ref, p_ref, meta_ref, ys_hbm, gate_norm, gate_w, ple_w, out_ref, ycat, sem):
    k = pl.program_id(0)
    slot = k % 2
    fetch = lambda blk, s, op: _run(_block_copies(seg_ref, blk, ys_hbm, ycat.at[s], sem.at[s], to_hbm=False), op)

    @pl.when(k == 0)
    def _():
        ycat[...] = jnp.zeros_like(ycat)
        fetch(k, slot, "start")

    @pl.when(k + 1 < pl.num_programs(0))
    def _():
        fetch(k + 1, 1 - slot, "start")

    fetch(k, slot, "wait")

    pos = lax.broadcasted_iota(jnp.int32, (1, SEG_ROWS), 1).astype(_F32)
    meta = meta_ref[...]
    onehot = jnp.where((pos == meta[:, 0:1]) | (pos == meta[:, 1:2]), 1.0, 0.0).astype(_BF)
    h = h_ref[...] + _dot(onehot, ycat[slot])
    gate = _sigmoid(_dot(_rms(h, gate_norm[...]).astype(_BF), gate_w[...]))
    out_ref[...] = h + gate * _dot(p_ref[...].astype(_BF), ple_w[...])


def _const(shape):
    return pl.BlockSpec(shape, lambda *_: (0,) * len(shape), pipeline_mode=pl.Buffered(1))


def _row(v):
    return v.reshape(1, -1).astype(_F32)


def kernel(x, p, pool_norm, pool_w, pool_scale, kv_norm, w_kv, k_norm, attn_norm, w_q, q_norm, sinks, w_o,
           ffn_norm, w_gu, w_down, router_w, router_b, we_gu, we_down, ple_gate_norm, ple_gate_w, ple_w):
    b, s, d = x.shape
    ts, tq = SEQ_TILE, ATTN_TILE

    inv = ROPE_THETA ** (-jnp.arange(0, HEAD_DIM, 2, dtype=_F32) / HEAD_DIM)
    ang = jnp.arange(s, dtype=_F32)[:, None] * inv[None, :]
    cos_t = jnp.tile(jnp.cos(ang), (1, LANES // (HEAD_DIM // 2)))
    sin_t = jnp.tile(jnp.concatenate([-jnp.sin(ang), jnp.sin(ang)], axis=1), (1, LANES // HEAD_DIM))
    hid = jnp.arange(KV_DIM) // HEAD_DIM
    hmat = ((hid[:, None] == hid[None, :]) * (1.0 / HEAD_DIM)).astype(_BF)

    q_gain = jnp.tile(q_norm[0].astype(_F32) * (HEAD_DIM ** -0.5), N_HEADS).reshape(1, d)
    k_gain = jnp.tile(k_norm.astype(_F32), N_KV_HEADS).reshape(1, KV_DIM)

    tps = s // ts
    n_tiles0 = b * tps
    first = lambda st: jnp.minimum(st, n_tiles0 - 1)
    third = lambda st: jnp.maximum(st - 2, 0)
    tile3 = lambda w: pl.BlockSpec((None, ts, w), lambda st: (third(st) // tps, third(st) % tps, 0))
    h1, q, k, v = pl.pallas_call(
        functools.partial(_layer0_kernel, tiles_per_seq=tps),
        grid=(n_tiles0 + 2,),
        in_specs=[
            pl.BlockSpec((None, ts, d), lambda st: (first(st) // tps, first(st) % tps, 0)),
            pl.BlockSpec((None, None, ts, PLE_DIM), lambda st: (0, third(st) // tps, third(st) % tps, 0)),
            pl.BlockSpec((ts, LANES), lambda st: (third(st) % tps, 0)),
            pl.BlockSpec((ts, LANES), lambda st: (third(st) % tps, 0)),
            _const((1, d)), _const((len(POOL_WINDOWS), POOL_GROUP, POOL_GROUP)), _const((1, d)),
            _const((1, d)), _const((d, 2 * D_FF)), _const((D_FF, d)),
            _const((1, d)), _const((d, d)), _const((PLE_DIM, d)),
            _const((1, d)), _const((d, d)), _const((1, d)),
            _const((1, d)), _const((d, 2 * KV_DIM)), _const((1, KV_DIM)), _const((KV_DIM, KV_DIM)),
        ],
        out_specs=[tile3(d), tile3(d), tile3(KV_DIM), tile3(KV_DIM)],
        out_shape=[jax.ShapeDtypeStruct((b, s, d), _F32), jax.ShapeDtypeStruct((b, s, d), _BF),
                   jax.ShapeDtypeStruct((b, s, KV_DIM), _BF), jax.ShapeDtypeStruct((b, s, KV_DIM), _BF)],
        scratch_shapes=[pltpu.VMEM((POOL_HALO + ts, d), _F32)] + [pltpu.VMEM((POOL_ROWS + 24, d), _F32)] * 2
                       + [pltpu.VMEM((ts, d), _F32), pltpu.VMEM((ts, d), _BF)] * 2 + [pltpu.VMEM((ts, d), _F32)],
        compiler_params=pltpu.CompilerParams(dimension_semantics=("arbitrary",), vmem_limit_bytes=VMEM_LIMIT),
        name="layer0",
    )(x, p, cos_t, sin_t, _row(pool_norm[0]), pool_w[0].astype(_BF), _row(pool_scale[0]),
      _row(ffn_norm[0]), w_gu[0].astype(_BF), w_down[0].astype(_BF), _row(ple_gate_norm[0]), ple_gate_w[0].astype(_BF), ple_w[0].astype(_BF),
      _row(attn_norm[0]), w_q[0].astype(_BF), q_gain, _row(kv_norm), w_kv.astype(_BF), k_gain, hmat)

    blocks_per_tile = tq // WINDOW
    prev = lambda bi, qi, *_: (bi, jnp.maximum(qi * blocks_per_tile - 1, 0), 0)
    cur = lambda bi, qi, *_: (bi, qi, 0)
    h2 = pl.pallas_call(
        _attn_kernel,
        grid_spec=pltpu.PrefetchScalarGridSpec(
            num_scalar_prefetch=1,
            grid=(b, s // tq),
            in_specs=[
                pl.BlockSpec((None, tq, d), cur),
                pl.BlockSpec((None, WINDOW, KV_DIM), prev), pl.BlockSpec((None, tq, KV_DIM), cur),
                pl.BlockSpec((None, WINDOW, KV_DIM), prev), pl.BlockSpec((None, tq, KV_DIM), cur),
                pl.BlockSpec((None, tq, d), cur),
                pl.BlockSpec((d, d), lambda *_: (0, 0), pipeline_mode=pl.Buffered(1)),
            ],
            out_specs=pl.BlockSpec((None, tq, d), cur),
            scratch_shapes=[pltpu.VMEM((WINDOW + tq, LANES), _BF)] * 4 + [pltpu.VMEM((tq, d), _BF)],
        ),
        out_shape=jax.ShapeDtypeStruct((b, s, d), _F32),
        compiler_params=pltpu.CompilerParams(dimension_semantics=("arbitrary", "arbitrary"),
                                             vmem_limit_bytes=VMEM_LIMIT),
        name="attn",
    )(sinks[0].astype(_F32), q, k, k, v, v, h1, w_o[0].astype(_BF))

    t = b * s
    rw = jnp.pad(router_w[0].astype(_F32), ((0, 0), (0, ROUTER_PAD - N_EXPERTS)))
    rw_hi = rw.astype(_BF)
    rw_lo = (rw - rw_hi.astype(_F32)).astype(_BF)
    rb = jnp.pad(router_b[0].astype(_F32), (0, ROUTER_PAD - N_EXPERTS)).reshape(1, ROUTER_PAD)

    bs, te = ROUTE_TILE, EXPERT_TILE
    nblk = t // bs
    cap = t
    smem = pl.BlockSpec(memory_space=pltpu.SMEM)
    tok = lambda w: pl.BlockSpec((bs, w), lambda ki, *_: (ki, 0))
    xs, meta, seg, totals = pl.pallas_call(
        _route_kernel,
        grid=(nblk,),
        in_specs=[tok(d), _const((1, d)), _const((d, ROUTER_PAD)), _const((d, ROUTER_PAD)), _const((1, ROUTER_PAD))],
        out_specs=[pl.BlockSpec(memory_space=pl.ANY), tok(ROUTER_PAD), smem, smem],
        out_shape=[jax.ShapeDtypeStruct((N_EXPERTS * cap, SORTED_WIDTH), _BF),
                   jax.ShapeDtypeStruct((t, ROUTER_PAD), _F32),
                   jax.ShapeDtypeStruct((nblk * 2 * N_EXPERTS,), jnp.int32),
                   jax.ShapeDtypeStruct((N_EXPERTS,), jnp.int32)],
        scratch_shapes=[pltpu.VMEM((2, SEG_ROWS, SORTED_WIDTH), _BF), pltpu.VMEM((EXPERT_TILE, SORTED_WIDTH), _BF),
                        pltpu.SMEM((N_EXPERTS,), jnp.int32), pltpu.SemaphoreType.DMA((2,))],
        compiler_params=pltpu.CompilerParams(dimension_semantics=("arbitrary",), vmem_limit_bytes=VMEM_LIMIT),
        name="route",
    )(h2.reshape(t, d), _row(ffn_norm[1]), rw_hi, rw_lo, rb)

    n_steps = (2 * t + nblk * N_EXPERTS * (SEG_ALIGN - 1)) // te + N_EXPERTS
    tiles_e = (totals + te - 1) // te
    tile_end = jnp.cumsum(tiles_e)
    n_tiles = tile_end[-1]
    step = jnp.minimum(jnp.arange(n_steps, dtype=jnp.int32), n_tiles - 1)
    tile_expert = jnp.sum(step[:, None] >= tile_end[None, :], axis=1).astype(jnp.int32)
    tile_block = (tile_expert * (cap // te) + step - (tile_end - tiles_e)[tile_expert]).astype(jnp.int32)

    xrow = lambda w: pl.BlockSpec((te, w), lambda i, te_r, tb_r, nt_r: (tb_r[i], 0))
    wexp = lambda r, c, cb: pl.BlockSpec((None, None, r, c), lambda i, te_r, tb_r, nt_r: (0, te_r[i], 0, cb))
    ys = pl.pallas_call(
        _expert_kernel,
        grid_spec=pltpu.PrefetchScalarGridSpec(
            num_scalar_prefetch=3,
            grid=(n_steps,),
            in_specs=[xrow(SORTED_WIDTH), wexp(d, D_FF_EXPERT, 0), wexp(d, D_FF_EXPERT, 1), wexp(D_FF_EXPERT, d, 0)],
            out_specs=xrow(d),
            scratch_shapes=[pltpu.VMEM((d, D_FF_EXPERT), _BF)] * 2 + [pltpu.VMEM((D_FF_EXPERT, d), _BF)],
        ),
        out_shape=jax.ShapeDtypeStruct((N_EXPERTS * cap, d), _BF),
        compiler_params=pltpu.CompilerParams(dimension_semantics=("arbitrary",), vmem_limit_bytes=VMEM_LIMIT),
        name="experts",
    )(tile_expert, tile_block, n_tiles.reshape(1).astype(jnp.int32), xs, we_gu, we_gu, we_down)

    out = pl.pallas_call(
        _combine_kernel,
        grid_spec=pltpu.PrefetchScalarGridSpec(
            num_scalar_prefetch=1,
            grid=(nblk,),
            in_specs=[tok(d),
                      pl.BlockSpec((None, None, bs, PLE_DIM), lambda ki, *_: (1, ki // (s // bs), ki % (s // bs), 0)),
                      tok(ROUTER_PAD), pl.BlockSpec(memory_space=pl.ANY),
                      _const((1, d)), _const((d, d)), _const((PLE_DIM, d))],
            out_specs=tok(d),
            scratch_shapes=[pltpu.VMEM((2, SEG_ROWS, d), _BF), pltpu.SemaphoreType.DMA((2,))],
        ),
        out_shape=jax.ShapeDtypeStruct((t, d), _F32),
        compiler_params=pltpu.CompilerParams(dimension_semantics=("arbitrary",), vmem_limit_bytes=VMEM_LIMIT),
        name="combine",
    )(seg, h2.reshape(t, d), p, meta, ys, _row(ple_gate_norm[1]),
      ple_gate_w[1].astype(_BF), ple_w[1].astype(_BF))
    return out.reshape(b, s, d)
```

```python
import functools

import jax
import jax.numpy as jnp
from jax import lax
from jax.experimental import pallas as pl
from jax.experimental.pallas import tpu as pltpu

D_MODEL = 1024
PLE_DIM = 256
POOL_WINDOWS = (2, 4, 8, 16)
POOL_GROUP = D_MODEL // len(POOL_WINDOWS)
POOL_HALO = 32
POOL_ROWS = 64
HEAD_DIM = 64
N_HEADS = 16
N_KV_HEADS = 4
KV_DIM = N_KV_HEADS * HEAD_DIM
WINDOW = 128
ROPE_THETA = 10000.0
D_FF = 2816
FF_CHUNK = 256
N_EXPERTS = 8
D_FF_EXPERT = 1024
EPS = 1e-6
LANES = 128
ROUTER_PAD = LANES

SEQ_TILE = 512
ATTN_TILE = 1024
ROUTE_TILE = 512
EXPERT_TILE = 1024
SEG_ALIGN = 16
SEG_SIZES = tuple(ROUTE_TILE >> i for i in range((ROUTE_TILE // SEG_ALIGN).bit_length()))
FILL_SIZES = tuple((EXPERT_TILE // 2) >> i for i in range((EXPERT_TILE // 2 // SEG_ALIGN).bit_length()))
SEG_ROWS = 2 * ROUTE_TILE + N_EXPERTS * SEG_ALIGN
GATE_LANES = LANES
SORTED_WIDTH = D_MODEL + GATE_LANES
VMEM_LIMIT = 60 * 1024 * 1024

_BF = jnp.bfloat16
_F32 = jnp.float32


def _rms(x, g):
    return x * lax.rsqrt(jnp.mean(x * x, axis=-1, keepdims=True) + EPS) * g


def _sigmoid(x):
    return 0.5 * jnp.tanh(0.5 * x) + 0.5


def _dot(a, b):
    return jnp.dot(a, b, preferred_element_type=_F32)


def _head_meansq(x, hmat):
    x2 = (x * x).astype(_BF)
    width = hmat.shape[0]
    return jnp.concatenate([_dot(x2[:, c:c + width], hmat) for c in range(0, x.shape[1], width)], axis=1)


def _rope(x, cos, sin_signed):
    n = x.shape[1]
    reps = n // LANES
    lane = lax.broadcasted_iota(jnp.int32, (1, n), 1)
    first_half = (lane % HEAD_DIM) < (HEAD_DIM // 2)
    partner = jnp.where(first_half, pltpu.roll(x, n - HEAD_DIM // 2, 1), pltpu.roll(x, HEAD_DIM // 2, 1))
    c = jnp.concatenate([cos] * reps, axis=1)
    s = jnp.concatenate([sin_signed] * reps, axis=1)
    return x * c + partner * s


def _layer0_kernel(x_ref, p_ref, cos_ref, sin_ref, pool_norm, pool_w, pool_scale, ffn_norm, wgu, wd,
                   gate_norm, gate_w, ple_w, attn_norm, wq, q_gain, kv_norm, wkv, k_gain, hmat,
                   h_out, q_out, k_out, v_out, ubuf, sum_a, sum_b, h1_cur, uf_cur, h1_next, uf_next, h2_buf,
                   act_scr, *, tiles_per_seq):
    step = pl.program_id(0)
    si = jnp.minimum(step, pl.num_programs(0) - 3) % tiles_per_seq
    ts = x_ref.shape[0]
    halo, end = POOL_HALO, POOL_HALO + ts
    n_chunks = D_FF // FF_CHUNK
    assert ts % POOL_ROWS == 0 and ts // POOL_ROWS <= n_chunks

    @pl.when(step == 0)
    def _():
        h1_cur[...] = jnp.zeros_like(h1_cur)
        uf_cur[...] = jnp.zeros_like(uf_cur)
        h2_buf[...] = jnp.zeros_like(h2_buf)

    @pl.when(si == 0)
    def _():
        ubuf[0:halo, :] = jnp.zeros((halo, D_MODEL), _F32)

    def first_stage_rows(r0, n):
        a, b = halo + r0, halo + r0 + n
        xr = x_ref[r0:r0 + n, :]
        u = _rms(xr, pool_norm[...])
        ubuf[a:b, :] = u
        g1, g2, g3 = POOL_GROUP, 2 * POOL_GROUP, 3 * POOL_GROUP
        m = n + 24
        s2 = ubuf[a - 24:b, :] + ubuf[a - 25:b - 1, :]
        sum_a[0:m, g1:] = s2[:, g1:]
        s4 = sum_a[8:m, g1:] + sum_a[6:m - 2, g1:]
        sum_b[8:m, g2:] = s4[:, g1:]
        s8 = sum_b[16:m, g2:] + sum_b[12:m - 4, g2:]
        sum_a[16:m, g3:] = s8[:, g1:]
        s16 = sum_a[24:m, g3:] + sum_a[16:m - 8, g3:]
        window_sums = (s2[24:, :g1], s4[16:, :g1], s8[8:, :g1], s16)
        pos = (si * ts + r0 + 1 + lax.broadcasted_iota(jnp.int32, (n, 1), 0)).astype(_F32)
        mixed = []
        for g, win in enumerate(POOL_WINDOWS):
            ug = u[:, g * POOL_GROUP:(g + 1) * POOL_GROUP]
            d = window_sums[g] / jnp.minimum(pos, float(win)) - ug
            mixed.append(_dot(d.astype(_BF), pool_w[g]))
        h_mixed = xr + jnp.concatenate(mixed, axis=1) * pool_scale[...]
        h1_next[r0:r0 + n, :] = h_mixed
        uf_next[r0:r0 + n, :] = _rms(h_mixed, ffn_norm[...]).astype(_BF)

    t = {}

    def embed_gate_matmul():
        t["h"] = h2_buf[...]
        t["gate_pre"] = _dot(_rms(t["h"], gate_norm[...]).astype(_BF), gate_w[...])

    def embed_and_unit_norm():
        h = t["h"] + _sigmoid(t["gate_pre"]) * _dot(p_ref[...].astype(_BF), ple_w[...])
        h_out[...] = h
        t["unit"] = h * lax.rsqrt(jnp.mean(h * h, axis=-1, keepdims=True) + EPS)

    def q_matmul():
        t["q"] = _dot((t["unit"] * attn_norm[...]).astype(_BF), wq[...])

    def q_norm_rope():
        q = t["q"] * lax.rsqrt(_head_meansq(t["q"], hmat[...]) + EPS) * q_gain[...]
        q_out[...] = _rope(q, cos_ref[...], sin_ref[...]).astype(_BF)

    def kv_matmul():
        t["kv"] = _dot((t["unit"] * kv_norm[...]).astype(_BF), wkv[...])

    def k_norm_rope():
        k = t["kv"][:, :KV_DIM]
        k = k * lax.rsqrt(_head_meansq(k, hmat[...]) + EPS) * k_gain[...]
        k_out[...] = _rope(k, cos_ref[...], sin_ref[...]).astype(_BF)
        v_out[...] = t["kv"][:, KV_DIM:].astype(_BF)

    third_stage = (embed_gate_matmul, embed_and_unit_norm, q_matmul, q_norm_rope, kv_matmul, k_norm_rope)
    assert len(third_stage) <= n_chunks

    uf = uf_cur[...]
    for c in range(n_chunks):
        c0 = c * FF_CHUNK
        a = _dot(uf, wgu[:, c0:c0 + FF_CHUNK])
        g = _dot(uf, wgu[:, D_FF + c0:D_FF + c0 + FF_CHUNK])
        act_scr[:, c0:c0 + FF_CHUNK] = (a * _sigmoid(a) * g).astype(_BF)
        if c < len(third_stage):
            third_stage[c]()
        if c * POOL_ROWS < ts:
            first_stage_rows(c * POOL_ROWS, POOL_ROWS)
    acc = _dot(act_scr[...], wd[...])

    h2_buf[...] = h1_cur[...] + acc
    ubuf[0:halo, :] = ubuf[ts:end, :]
    h1_cur[...] = h1_next[...]
    uf_cur[...] = uf_next[...]


def _attn_kernel(sinks_ref, q_ref, kp_ref, kc_ref, vp_ref, vc_ref, h_ref, wo_ref, out_ref,
                 klo, khi, vlo, vhi, o_scr):
    i = pl.program_id(1)
    tq = q_ref.shape[0]
    nblk = tq // WINDOW
    lane = lax.broadcasted_iota(jnp.int32, (1, LANES), 1)
    low = lane < HEAD_DIM

    qi = lax.broadcasted_iota(jnp.int32, (2 * WINDOW, 2 * WINDOW), 0) % WINDOW
    kj = lax.broadcasted_iota(jnp.int32, (2 * WINDOW, 2 * WINDOW), 1)
    in_window = (kj > qi) & (kj <= qi + WINDOW)
    upper_rows = lax.broadcasted_iota(jnp.int32, (2 * WINDOW, 1), 0) >= WINDOW

    for g in range(N_KV_HEADS):
        slab = slice((g // 2) * LANES, (g // 2 + 1) * LANES)
        for src_p, src_c, lo_ref, hi_ref in ((kp_ref, kc_ref, klo, khi), (vp_ref, vc_ref, vlo, vhi)):
            t = jnp.concatenate([src_p[:, slab], src_c[:, slab]], axis=0).astype(_F32)
            r = pltpu.roll(t, HEAD_DIM, 1)
            in_low, in_high = (t, r) if g % 2 == 0 else (r, t)
            lo_ref[...] = jnp.where(low, in_low, 0.0).astype(_BF)
            hi_ref[...] = jnp.where(low, 0.0, in_high).astype(_BF)

        for j in range(nblk):
            qrows = pl.ds(j * WINDOW, WINDOW)
            krows = pl.ds(j * WINDOW, 2 * WINDOW)
            qg = jnp.concatenate([q_ref[qrows, (2 * g) * LANES:(2 * g + 1) * LANES],
                                  q_ref[qrows, (2 * g + 1) * LANES:(2 * g + 2) * LANES]], axis=0)
            mask = in_window & (kj >= jnp.where(i > 0, 0, WINDOW)) if j == 0 else in_window
            o = jnp.zeros((2 * WINDOW, LANES), _F32)
            for half, (k_ref, v_ref) in enumerate(((klo, vlo), (khi, vhi))):
                sink = jnp.where(upper_rows, sinks_ref[4 * g + 2 + half], sinks_ref[4 * g + half])
                s = lax.dot_general(qg, k_ref[krows, :], (((1,), (1,)), ((), ())),
                                    preferred_element_type=_F32)
                s = jnp.where(mask, s, -jnp.inf)
                m = jnp.maximum(jnp.max(s, axis=1, keepdims=True), sink)
                pr = jnp.exp(s - m)
                denom = jnp.sum(pr, axis=1, keepdims=True) + jnp.exp(sink - m)
                o = o + _dot(pr.astype(_BF), v_ref[krows, :]) / denom
            o_scr[qrows, (2 * g) * LANES:(2 * g + 1) * LANES] = o[:WINDOW].astype(_BF)
            o_scr[qrows, (2 * g + 1) * LANES:(2 * g + 2) * LANES] = o[WINDOW:].astype(_BF)

    out_ref[...] = h_ref[...] + _dot(o_scr[...], wo_ref[...])


def _segment_copies(rows, src_ref, src_base, dst_ref, dst_base, sem, sizes=SEG_SIZES):
    out = []
    for sz in sizes:
        done = rows & (-2 * sz)
        src = src_ref.at[pl.ds(pl.multiple_of(src_base + done, SEG_ALIGN), sz)]
        dst = dst_ref.at[pl.ds(pl.multiple_of(dst_base + done, SEG_ALIGN), sz)]
        out.append(((rows & sz) != 0, pltpu.make_async_copy(src, dst, sem)))
    return out


def _run(copies, op):
    for cond, cp in copies:
        @pl.when(cond)
        def _(cp=cp):
            getattr(cp, op)()


def _block_copies(seg_ref, blk, hbm_ref, buf_ref, sem, to_hbm):
    cap = hbm_ref.shape[0] // N_EXPERTS
    copies = []
    off = jnp.int32(0)
    for e in range(N_EXPERTS):
        base = e * cap + seg_ref[blk * 2 * N_EXPERTS + e]
        rows = seg_ref[blk * 2 * N_EXPERTS + N_EXPERTS + e]
        if to_hbm:
            copies += _segment_copies(rows, buf_ref, off, hbm_ref, base, sem)
        else:
            copies += _segment_copies(rows, hbm_ref, base, buf_ref, off, sem)
        off = off + rows
    return copies


def _route_kernel(h_ref, ffn_norm, rw_hi, rw_lo, rb, xs_hbm, meta_ref, seg_ref, tot_ref,
                  comp_scr, zero_scr, cum_ref, sem):
    k = pl.program_id(0)
    bs = h_ref.shape[0]
    cap = xs_hbm.shape[0] // N_EXPERTS
    lane = lax.broadcasted_iota(jnp.int32, (1, ROUTER_PAD), 1)

    @pl.when(k == 0)
    def _():
        for e in range(N_EXPERTS):
            cum_ref[e] = 0

    u = _rms(h_ref[...], ffn_norm[...])
    u_hi = u.astype(_BF)
    u_lo = (u - u_hi.astype(_F32)).astype(_BF)
    logits = _dot(u_hi, rw_hi[...]) + _dot(u_lo, rw_hi[...]) + _dot(u_hi, rw_lo[...]) + rb[...]
    logits = jnp.where(lane < N_EXPERTS, logits, -jnp.inf)
    m1 = jnp.max(logits, axis=1, keepdims=True)
    i1 = jnp.min(jnp.where(logits == m1, lane, ROUTER_PAD), axis=1, keepdims=True)
    rest = jnp.where(lane == i1, -jnp.inf, logits)
    m2 = jnp.max(rest, axis=1, keepdims=True)
    i2 = jnp.min(jnp.where(rest == m2, lane, ROUTER_PAD), axis=1, keepdims=True)
    t = jnp.exp(m2 - m1)
    w1 = 1.0 / (1.0 + t)
    w2 = t * w1

    a1 = lane == i1
    a2 = lane == i2
    assigned = jnp.where(a1 | a2, 1.0, 0.0)
    before = (lax.broadcasted_iota(jnp.int32, (bs, bs), 1) < lax.broadcasted_iota(jnp.int32, (bs, bs), 0))
    rank = _dot(jnp.where(before, 1.0, 0.0).astype(_BF), assigned.astype(_BF))
    count = jnp.sum(assigned, axis=0, keepdims=True).astype(jnp.int32)
    seg_off = jnp.zeros((1, ROUTER_PAD), _F32)
    off = jnp.int32(0)
    for e in range(N_EXPERTS):
        rows = ((count[0, e] + (SEG_ALIGN - 1)) // SEG_ALIGN) * SEG_ALIGN
        seg_off = jnp.where(lane == e, off.astype(_F32), seg_off)
        base = cum_ref[e]
        seg_ref[k * 2 * N_EXPERTS + e] = base
        seg_ref[k * 2 * N_EXPERTS + N_EXPERTS + e] = rows
        cum_ref[e] = base + rows
        off = off + rows
    dst1 = jnp.sum(jnp.where(a1, rank + seg_off, 0.0), axis=1, keepdims=True)
    dst2 = jnp.sum(jnp.where(a2, rank + seg_off, 0.0), axis=1, keepdims=True)
    meta = jnp.where(lane == 0, dst1, jnp.where(lane == 1, dst2, jnp.where(lane == 2, w1, jnp.where(lane == 3, w2, 0.0))))
    meta_ref[...] = meta

    by_token = meta.T
    row = lax.broadcasted_iota(jnp.int32, (SEG_ROWS, 1), 0).astype(_F32)
    from1 = row == by_token[0:1, :]
    from2 = row == by_token[1:2, :]
    onehot = jnp.where(from1 | from2, 1.0, 0.0).astype(_BF)
    gate = jnp.sum(jnp.where(from1, by_token[2:3, :], 0.0) + jnp.where(from2, by_token[3:4, :], 0.0),
                   axis=1, keepdims=True)
    hi = gate.astype(_BF).astype(_F32)
    mid = (gate - hi).astype(_BF).astype(_F32)
    gate_parts = jnp.where(lane == 0, hi, jnp.where(lane == 1, mid, jnp.where(lane == 2, gate - hi - mid, 0.0)))

    slot = k % 2
    comp = comp_scr.at[slot]
    comp[:, :D_MODEL] = _dot(onehot, u_hi).astype(_BF)
    comp[:, D_MODEL:] = gate_parts.astype(_BF)
    _run(_block_copies(seg_ref, k, xs_hbm, comp, sem.at[slot], to_hbm=True), "start")

    @pl.when(k > 0)
    def _():
        _run(_block_copies(seg_ref, k - 1, xs_hbm, comp_scr.at[1 - slot], sem.at[1 - slot], to_hbm=True), "wait")

    @pl.when(k == pl.num_programs(0) - 1)
    def _():
        _run(_block_copies(seg_ref, k, xs_hbm, comp, sem.at[slot], to_hbm=True), "wait")
        zero_scr[...] = jnp.zeros_like(zero_scr)
        fills = []
        for e in range(N_EXPERTS):
            total = cum_ref[e]
            tot_ref[e] = total
            fills += _segment_copies((-total) & (EXPERT_TILE - 1), zero_scr, 0, xs_hbm, e * cap + total,
                                     sem.at[slot], sizes=FILL_SIZES)
        _run(fills, "start")
        _run(fills, "wait")


def _expert_kernel(te_ref, tb_ref, nt_ref, x_ref, wa_f32, wg_f32, wd_f32, y_ref, wa, wg, wd):
    i = pl.program_id(0)

    @pl.when((i == 0) | (te_ref[i] != te_ref[jnp.maximum(i - 1, 0)]))
    def _():
        wa[...] = wa_f32[...].astype(_BF)
        wg[...] = wg_f32[...].astype(_BF)
        wd[...] = wd_f32[...].astype(_BF)

    @pl.when(i < nt_ref[0])
    def _():
        x = x_ref[:, :D_MODEL]
        gate = jnp.sum(x_ref[:, D_MODEL:].astype(_F32), axis=1, keepdims=True)
        a = _dot(x, wa[...])
        g = _dot(x, wg[...])
        act = (a * _sigmoid(a) * g).astype(_BF)
        y_ref[...] = (gate * _dot(act, wd[...])).astype(_BF)


def _combine_kernel(seg_ref, h_ref, p_ref, meta_ref, ys_hbm, gate_norm, gate_w, ple_w, out_ref, ycat, sem):
    k = pl.program_id(0)
    slot = k % 2
    fetch = lambda blk, s, op: _run(_block_copies(seg_ref, blk, ys_hbm, ycat.at[s], sem.at[s], to_hbm=False), op)

    @pl.when(k == 0)
    def _():
        ycat[...] = jnp.zeros_like(ycat)
        fetch(k, slot, "start")

    @pl.when(k + 1 < pl.num_programs(0))
    def _():
        fetch(k + 1, 1 - slot, "start")

    fetch(k, slot, "wait")

    pos = lax.broadcasted_iota(jnp.int32, (1, SEG_ROWS), 1).astype(_F32)
    meta = meta_ref[...]
    onehot = jnp.where((pos == meta[:, 0:1]) | (pos == meta[:, 1:2]), 1.0, 0.0).astype(_BF)
    h = h_ref[...] + _dot(onehot, ycat[slot])
    gate = _sigmoid(_dot(_rms(h, gate_norm[...]).astype(_BF), gate_w[...]))
    out_ref[...] = h + gate * _dot(p_ref[...].astype(_BF), ple_w[...])


def _const(shape):
    return pl.BlockSpec(shape, lambda *_: (0,) * len(shape), pipeline_mode=pl.Buffered(1))


def _row(v):
    return v.reshape(1, -1).astype(_F32)


def kernel(x, p, pool_norm, pool_w, pool_scale, kv_norm, w_kv, k_norm, attn_norm, w_q, q_norm, sinks, w_o,
           ffn_norm, w_gu, w_down, router_w, router_b, we_gu, we_down, ple_gate_norm, ple_gate_w, ple_w):
    b, s, d = x.shape
    ts, tq = SEQ_TILE, ATTN_TILE

    inv = ROPE_THETA ** (-jnp.arange(0, HEAD_DIM, 2, dtype=_F32) / HEAD_DIM)
    ang = jnp.arange(s, dtype=_F32)[:, None] * inv[None, :]
    cos_t = jnp.tile(jnp.cos(ang), (1, LANES // (HEAD_DIM // 2)))
    sin_t = jnp.tile(jnp.concatenate([-jnp.sin(ang), jnp.sin(ang)], axis=1), (1, LANES // HEAD_DIM))
    hid = jnp.arange(KV_DIM) // HEAD_DIM
    hmat = ((hid[:, None] == hid[None, :]) * (1.0 / HEAD_DIM)).astype(_BF)

    q_gain = jnp.tile(q_norm[0].astype(_F32) * (HEAD_DIM ** -0.5), N_HEADS).reshape(1, d)
    k_gain = jnp.tile(k_norm.astype(_F32), N_KV_HEADS).reshape(1, KV_DIM)

    tps = s // ts
    n_tiles0 = b * tps
    first = lambda st: jnp.minimum(st, n_tiles0 - 1)
    third = lambda st: jnp.maximum(st - 2, 0)
    tile3 = lambda w: pl.BlockSpec((None, ts, w), lambda st: (third(st) // tps, third(st) % tps, 0))
    h1, q, k, v = pl.pallas_call(
        functools.partial(_layer0_kernel, tiles_per_seq=tps),
        grid=(n_tiles0 + 2,),
        in_specs=[
            pl.BlockSpec((None, ts, d), lambda st: (first(st) // tps, first(st) % tps, 0)),
            pl.BlockSpec((None, None, ts, PLE_DIM), lambda st: (0, third(st) // tps, third(st) % tps, 0)),
            pl.BlockSpec((ts, LANES), lambda st: (third(st) % tps, 0)),
            pl.BlockSpec((ts, LANES), lambda st: (third(st) % tps, 0)),
            _const((1, d)), _const((len(POOL_WINDOWS), POOL_GROUP, POOL_GROUP)), _const((1, d)),
            _const((1, d)), _const((d, 2 * D_FF)), _const((D_FF, d)),
            _const((1, d)), _const((d, d)), _const((PLE_DIM, d)),
            _const((1, d)), _const((d, d)), _const((1, d)),
            _const((1, d)), _const((d, 2 * KV_DIM)), _const((1, KV_DIM)), _const((KV_DIM, KV_DIM)),
        ],
        out_specs=[tile3(d), tile3(d), tile3(KV_DIM), tile3(KV_DIM)],
        out_shape=[jax.ShapeDtypeStruct((b, s, d), _F32), jax.ShapeDtypeStruct((b, s, d), _BF),
                   jax.ShapeDtypeStruct((b, s, KV_DIM), _BF), jax.ShapeDtypeStruct((b, s, KV_DIM), _BF)],
        scratch_shapes=[pltpu.VMEM((POOL_HALO + ts, d), _F32)] + [pltpu.VMEM((POOL_ROWS + 24, d), _F32)] * 2
                       + [pltpu.VMEM((ts, d), _F32), pltpu.VMEM((ts, d), _BF)] * 2 + [pltpu.VMEM((ts, d), _F32)]
                       + [pltpu.VMEM((ts, D_FF), _BF)],
        compiler_params=pltpu.CompilerParams(dimension_semantics=("arbitrary",), vmem_limit_bytes=VMEM_LIMIT),
        name="layer0",
    )(x, p, cos_t, sin_t, _row(pool_norm[0]), pool_w[0].astype(_BF), _row(pool_scale[0]),
      _row(ffn_norm[0]), w_gu[0].astype(_BF), w_down[0].astype(_BF), _row(ple_gate_norm[0]), ple_gate_w[0].astype(_BF), ple_w[0].astype(_BF),
      _row(attn_norm[0]), w_q[0].astype(_BF), q_gain, _row(kv_norm), w_kv.astype(_BF), k_gain, hmat)

    blocks_per_tile = tq // WINDOW
    prev = lambda bi, qi, *_: (bi, jnp.maximum(qi * blocks_per_tile - 1, 0), 0)
    cur = lambda bi, qi, *_: (bi, qi, 0)
    h2 = pl.pallas_call(
        _attn_kernel,
        grid_spec=pltpu.PrefetchScalarGridSpec(
            num_scalar_prefetch=1,
            grid=(b, s // tq),
            in_specs=[
                pl.BlockSpec((None, tq, d), cur),
                pl.BlockSpec((None, WINDOW, KV_DIM), prev), pl.BlockSpec((None, tq, KV_DIM), cur),
                pl.BlockSpec((None, WINDOW, KV_DIM), prev), pl.BlockSpec((None, tq, KV_DIM), cur),
                pl.BlockSpec((None, tq, d), cur),
                pl.BlockSpec((d, d), lambda *_: (0, 0), pipeline_mode=pl.Buffered(1)),
            ],
            out_specs=pl.BlockSpec((None, tq, d), cur),
            scratch_shapes=[pltpu.VMEM((WINDOW + tq, LANES), _BF)] * 4 + [pltpu.VMEM((tq, d), _BF)],
        ),
        out_shape=jax.ShapeDtypeStruct((b, s, d), _F32),
        compiler_params=pltpu.CompilerParams(dimension_semantics=("arbitrary", "arbitrary"),
                                             vmem_limit_bytes=VMEM_LIMIT),
        name="attn",
    )(sinks[0].astype(_F32), q, k, k, v, v, h1, w_o[0].astype(_BF))

    t = b * s
    rw = jnp.pad(router_w[0].astype(_F32), ((0, 0), (0, ROUTER_PAD - N_EXPERTS)))
    rw_hi = rw.astype(_BF)
    rw_lo = (rw - rw_hi.astype(_F32)).astype(_BF)
    rb = jnp.pad(router_b[0].astype(_F32), (0, ROUTER_PAD - N_EXPERTS)).reshape(1, ROUTER_PAD)

    bs, te = ROUTE_TILE, EXPERT_TILE
    nblk = t // bs
    cap = t
    smem = pl.BlockSpec(memory_space=pltpu.SMEM)
    tok = lambda w: pl.BlockSpec((bs, w), lambda ki, *_: (ki, 0))
    xs, meta, seg, totals = pl.pallas_call(
        _route_kernel,
        grid=(nblk,),
        in_specs=[tok(d), _const((1, d)), _const((d, ROUTER_PAD)), _const((d, ROUTER_PAD)), _const((1, ROUTER_PAD))],
        out_specs=[pl.BlockSpec(memory_space=pl.ANY), tok(ROUTER_PAD), smem, smem],
        out_shape=[jax.ShapeDtypeStruct((N_EXPERTS * cap, SORTED_WIDTH), _BF),
                   jax.ShapeDtypeStruct((t, ROUTER_PAD), _F32),
                   jax.ShapeDtypeStruct((nblk * 2 * N_EXPERTS,), jnp.int32),
                   jax.ShapeDtypeStruct((N_EXPERTS,), jnp.int32)],
        scratch_shapes=[pltpu.VMEM((2, SEG_ROWS, SORTED_WIDTH), _BF), pltpu.VMEM((EXPERT_TILE, SORTED_WIDTH), _BF),
                        pltpu.SMEM((N_EXPERTS,), jnp.int32), pltpu.SemaphoreType.DMA((2,))],
        compiler_params=pltpu.CompilerParams(dimension_semantics=("arbitrary",), vmem_limit_bytes=VMEM_LIMIT),
        name="route",
    )(h2.reshape(t, d), _row(ffn_norm[1]), rw_hi, rw_lo, rb)

    n_steps = (2 * t + nblk * N_EXPERTS * (SEG_ALIGN - 1)) // te + N_EXPERTS
    tiles_e = (totals + te - 1) // te
    tile_end = jnp.cumsum(tiles_e)
    n_tiles = tile_end[-1]
    step = jnp.minimum(jnp.arange(n_steps, dtype=jnp.int32), n_tiles - 1)
    tile_expert = jnp.sum(step[:, None] >= tile_end[None, :], axis=1).astype(jnp.int32)
    tile_block = (tile_expert * (cap // te) + step - (tile_end - tiles_e)[tile_expert]).astype(jnp.int32)

    xrow = lambda w: pl.BlockSpec((te, w), lambda i, te_r, tb_r, nt_r: (tb_r[i], 0))
    wexp = lambda r, c, cb: pl.BlockSpec((None, None, r, c), lambda i, te_r, tb_r, nt_r: (0, te_r[i], 0, cb))
    ys = pl.pallas_call(
        _expert_kernel,
        grid_spec=pltpu.PrefetchScalarGridSpec(
            num_scalar_prefetch=3,
            grid=(n_steps,),
            in_specs=[xrow(SORTED_WIDTH), wexp(d, D_FF_EXPERT, 0), wexp(d, D_FF_EXPERT, 1), wexp(D_FF_EXPERT, d, 0)],
            out_specs=xrow(d),
            scratch_shapes=[pltpu.VMEM((d, D_FF_EXPERT), _BF)] * 2 + [pltpu.VMEM((D_FF_EXPERT, d), _BF)],
        ),
        out_shape=jax.ShapeDtypeStruct((N_EXPERTS * cap, d), _BF),
        compiler_params=pltpu.CompilerParams(dimension_semantics=("arbitrary",), vmem_limit_bytes=VMEM_LIMIT),
        name="experts",
    )(tile_expert, tile_block, n_tiles.reshape(1).astype(jnp.int32), xs, we_gu, we_gu, we_down)

    out = pl.pallas_call(
        _combine_kernel,
        grid_spec=pltpu.PrefetchScalarGridSpec(
            num_scalar_prefetch=1,
            grid=(nblk,),
            in_specs=[tok(d),
                      pl.BlockSpec((None, None, bs, PLE_DIM), lambda ki, *_: (1, ki // (s // bs), ki % (s // bs), 0)),
                      tok(ROUTER_PAD), pl.BlockSpec(memory_space=pl.ANY),
                      _const((1, d)), _const((d, d)), _const((PLE_DIM, d))],
            out_specs=tok(d),
            scratch_shapes=[pltpu.VMEM((2, SEG_ROWS, d), _BF), pltpu.SemaphoreType.DMA((2,))],
        ),
        out_shape=jax.ShapeDtypeStruct((t, d), _F32),
        compiler_params=pltpu.CompilerParams(dimension_semantics=("arbitrary",), vmem_limit_bytes=VMEM_LIMIT),
        name="combine",
    )(seg, h2.reshape(t, d), p, meta, ys, _row(ple_gate_norm[1]),
      ple_gate_w[1].astype(_BF), ple_w[1].astype(_BF))
    return out.reshape(b, s, d)
```

```python
import functools

import jax
import jax.numpy as jnp
from jax import lax
from jax.experimental import pallas as pl
from jax.experimental.pallas import tpu as pltpu

D_MODEL = 1024
PLE_DIM = 256
POOL_WINDOWS = (2, 4, 8, 16)
POOL_GROUP = D_MODEL // len(POOL_WINDOWS)
POOL_HALO = 32
POOL_ROWS = 64
HEAD_DIM = 64
N_HEADS = 16
N_KV_HEADS = 4
KV_DIM = N_KV_HEADS * HEAD_DIM
WINDOW = 128
ROPE_THETA = 10000.0
D_FF = 2816
FF_CHUNK = 256
N_EXPERTS = 8
D_FF_EXPERT = 1024
EPS = 1e-6
LANES = 128
ROUTER_PAD = LANES

SEQ_TILE = 512
ATTN_TILE = 1024
ROUTE_TILE = 512
EXPERT_TILE = 1024
EXPERT_SUB = 256
SEG_ALIGN = 16
SEG_SIZES = tuple(ROUTE_TILE >> i for i in range((ROUTE_TILE // SEG_ALIGN).bit_length()))
FILL_SIZES = tuple((EXPERT_TILE // 2) >> i for i in range((EXPERT_TILE // 2 // SEG_ALIGN).bit_length()))
SEG_ROWS = 2 * ROUTE_TILE + N_EXPERTS * SEG_ALIGN
GATE_LANES = LANES
SORTED_WIDTH = D_MODEL + GATE_LANES
VMEM_LIMIT = 60 * 1024 * 1024

_BF = jnp.bfloat16
_F32 = jnp.float32


def _rms(x, g):
    return x * lax.rsqrt(jnp.mean(x * x, axis=-1, keepdims=True) + EPS) * g


def _sigmoid(x):
    return 0.5 * jnp.tanh(0.5 * x) + 0.5


def _dot(a, b):
    return jnp.dot(a, b, preferred_element_type=_F32)


def _head_meansq(x, hmat):
    x2 = (x * x).astype(_BF)
    width = hmat.shape[0]
    return jnp.concatenate([_dot(x2[:, c:c + width], hmat) for c in range(0, x.shape[1], width)], axis=1)


def _rope(x, cos, sin_signed):
    n = x.shape[1]
    reps = n // LANES
    lane = lax.broadcasted_iota(jnp.int32, (1, n), 1)
    first_half = (lane % HEAD_DIM) < (HEAD_DIM // 2)
    partner = jnp.where(first_half, pltpu.roll(x, n - HEAD_DIM // 2, 1), pltpu.roll(x, HEAD_DIM // 2, 1))
    c = jnp.concatenate([cos] * reps, axis=1)
    s = jnp.concatenate([sin_signed] * reps, axis=1)
    return x * c + partner * s


def _layer0_kernel(x_ref, p_ref, cos_ref, sin_ref, pool_norm, pool_w, pool_scale, ffn_norm, wgu, wd,
                   gate_norm, gate_w, ple_w, attn_norm, wq, q_gain, kv_norm, wkv, k_gain, hmat,
                   h_out, q_out, k_out, v_out, ubuf, sum_a, sum_b, h1_cur, uf_cur, h1_next, uf_next, h2_buf,
                   act_scr, *, tiles_per_seq):
    step = pl.program_id(0)
    si = jnp.minimum(step, pl.num_programs(0) - 3) % tiles_per_seq
    ts = x_ref.shape[0]
    halo, end = POOL_HALO, POOL_HALO + ts
    n_chunks = D_FF // FF_CHUNK
    assert ts % POOL_ROWS == 0 and ts // POOL_ROWS <= n_chunks

    @pl.when(step == 0)
    def _():
        h1_cur[...] = jnp.zeros_like(h1_cur)
        uf_cur[...] = jnp.zeros_like(uf_cur)
        h2_buf[...] = jnp.zeros_like(h2_buf)

    @pl.when(si == 0)
    def _():
        ubuf[0:halo, :] = jnp.zeros((halo, D_MODEL), _F32)

    def first_stage_rows(r0, n):
        a, b = halo + r0, halo + r0 + n
        xr = x_ref[r0:r0 + n, :]
        u = _rms(xr, pool_norm[...])
        ubuf[a:b, :] = u
        g1, g2, g3 = POOL_GROUP, 2 * POOL_GROUP, 3 * POOL_GROUP
        m = n + 24
        s2 = ubuf[a - 24:b, :] + ubuf[a - 25:b - 1, :]
        sum_a[0:m, g1:] = s2[:, g1:]
        s4 = sum_a[8:m, g1:] + sum_a[6:m - 2, g1:]
        sum_b[8:m, g2:] = s4[:, g1:]
        s8 = sum_b[16:m, g2:] + sum_b[12:m - 4, g2:]
        sum_a[16:m, g3:] = s8[:, g1:]
        s16 = sum_a[24:m, g3:] + sum_a[16:m - 8, g3:]
        window_sums = (s2[24:, :g1], s4[16:, :g1], s8[8:, :g1], s16)
        pos = (si * ts + r0 + 1 + lax.broadcasted_iota(jnp.int32, (n, 1), 0)).astype(_F32)
        mixed = []
        for g, win in enumerate(POOL_WINDOWS):
            ug = u[:, g * POOL_GROUP:(g + 1) * POOL_GROUP]
            d = window_sums[g] / jnp.minimum(pos, float(win)) - ug
            mixed.append(_dot(d.astype(_BF), pool_w[g]))
        h_mixed = xr + jnp.concatenate(mixed, axis=1) * pool_scale[...]
        h1_next[r0:r0 + n, :] = h_mixed
        uf_next[r0:r0 + n, :] = _rms(h_mixed, ffn_norm[...]).astype(_BF)

    t = {}

    def embed_gate_matmul():
        t["h"] = h2_buf[...]
        t["gate_pre"] = _dot(_rms(t["h"], gate_norm[...]).astype(_BF), gate_w[...])

    def embed_and_unit_norm():
        h = t["h"] + _sigmoid(t["gate_pre"]) * _dot(p_ref[...].astype(_BF), ple_w[...])
        h_out[...] = h
        t["unit"] = h * lax.rsqrt(jnp.mean(h * h, axis=-1, keepdims=True) + EPS)

    def q_matmul():
        t["q"] = _dot((t["unit"] * attn_norm[...]).astype(_BF), wq[...])

    def q_norm_rope():
        q = t["q"] * lax.rsqrt(_head_meansq(t["q"], hmat[...]) + EPS) * q_gain[...]
        q_out[...] = _rope(q, cos_ref[...], sin_ref[...]).astype(_BF)

    def kv_matmul():
        t["kv"] = _dot((t["unit"] * kv_norm[...]).astype(_BF), wkv[...])

    def k_norm_rope():
        k = t["kv"][:, :KV_DIM]
        k = k * lax.rsqrt(_head_meansq(k, hmat[...]) + EPS) * k_gain[...]
        k_out[...] = _rope(k, cos_ref[...], sin_ref[...]).astype(_BF)
        v_out[...] = t["kv"][:, KV_DIM:].astype(_BF)

    third_stage = (embed_gate_matmul, embed_and_unit_norm, q_matmul, q_norm_rope, kv_matmul, k_norm_rope)
    assert len(third_stage) <= n_chunks

    uf = uf_cur[...]
    for c in range(n_chunks):
        c0 = c * FF_CHUNK
        a = _dot(uf, wgu[:, c0:c0 + FF_CHUNK])
        g = _dot(uf, wgu[:, D_FF + c0:D_FF + c0 + FF_CHUNK])
        act_scr[:, c0:c0 + FF_CHUNK] = (a * _sigmoid(a) * g).astype(_BF)
        if c < len(third_stage):
            third_stage[c]()
        if c * POOL_ROWS < ts:
            first_stage_rows(c * POOL_ROWS, POOL_ROWS)
    acc = _dot(act_scr[...], wd[...])

    h2_buf[...] = h1_cur[...] + acc
    ubuf[0:halo, :] = ubuf[ts:end, :]
    h1_cur[...] = h1_next[...]
    uf_cur[...] = uf_next[...]


def _attn_kernel(sinks_ref, q_ref, kp_ref, kc_ref, vp_ref, vc_ref, h_ref, wo_ref, out_ref,
                 klo, khi, vlo, vhi, o_scr):
    i = pl.program_id(1)
    tq = q_ref.shape[0]
    nblk = tq // WINDOW
    lane = lax.broadcasted_iota(jnp.int32, (1, LANES), 1)
    low = lane < HEAD_DIM

    qi = lax.broadcasted_iota(jnp.int32, (2 * WINDOW, 2 * WINDOW), 0) % WINDOW
    kj = lax.broadcasted_iota(jnp.int32, (2 * WINDOW, 2 * WINDOW), 1)
    in_window = (kj > qi) & (kj <= qi + WINDOW)
    upper_rows = lax.broadcasted_iota(jnp.int32, (2 * WINDOW, 1), 0) >= WINDOW

    for g in range(N_KV_HEADS):
        slab = slice((g // 2) * LANES, (g // 2 + 1) * LANES)
        for src_p, src_c, lo_ref, hi_ref in ((kp_ref, kc_ref, klo, khi), (vp_ref, vc_ref, vlo, vhi)):
            t = jnp.concatenate([src_p[:, slab], src_c[:, slab]], axis=0).astype(_F32)
            r = pltpu.roll(t, HEAD_DIM, 1)
            in_low, in_high = (t, r) if g % 2 == 0 else (r, t)
            lo_ref[...] = jnp.where(low, in_low, 0.0).astype(_BF)
            hi_ref[...] = jnp.where(low, 0.0, in_high).astype(_BF)

        for j in range(nblk):
            qrows = pl.ds(j * WINDOW, WINDOW)
            krows = pl.ds(j * WINDOW, 2 * WINDOW)
            qg = jnp.concatenate([q_ref[qrows, (2 * g) * LANES:(2 * g + 1) * LANES],
                                  q_ref[qrows, (2 * g + 1) * LANES:(2 * g + 2) * LANES]], axis=0)
            mask = in_window & (kj >= jnp.where(i > 0, 0, WINDOW)) if j == 0 else in_window
            o = jnp.zeros((2 * WINDOW, LANES), _F32)
            for half, (k_ref, v_ref) in enumerate(((klo, vlo), (khi, vhi))):
                sink = jnp.where(upper_rows, sinks_ref[4 * g + 2 + half], sinks_ref[4 * g + half])
                s = lax.dot_general(qg, k_ref[krows, :], (((1,), (1,)), ((), ())),
                                    preferred_element_type=_F32)
                s = jnp.where(mask, s, -jnp.inf)
                m = jnp.maximum(jnp.max(s, axis=1, keepdims=True), sink)
                pr = jnp.exp(s - m)
                denom = jnp.sum(pr, axis=1, keepdims=True) + jnp.exp(sink - m)
                o = o + _dot(pr.astype(_BF), v_ref[krows, :]) / denom
            o_scr[qrows, (2 * g) * LANES:(2 * g + 1) * LANES] = o[:WINDOW].astype(_BF)
            o_scr[qrows, (2 * g + 1) * LANES:(2 * g + 2) * LANES] = o[WINDOW:].astype(_BF)

    out_ref[...] = h_ref[...] + _dot(o_scr[...], wo_ref[...])


def _segment_copies(rows, src_ref, src_base, dst_ref, dst_base, sem, sizes=SEG_SIZES):
    out = []
    for sz in sizes:
        done = rows & (-2 * sz)
        src = src_ref.at[pl.ds(pl.multiple_of(src_base + done, SEG_ALIGN), sz)]
        dst = dst_ref.at[pl.ds(pl.multiple_of(dst_base + done, SEG_ALIGN), sz)]
        out.append(((rows & sz) != 0, pltpu.make_async_copy(src, dst, sem)))
    return out


def _run(copies, op):
    for cond, cp in copies:
        @pl.when(cond)
        def _(cp=cp):
            getattr(cp, op)()


def _block_copies(seg_ref, blk, hbm_ref, buf_ref, sem, to_hbm):
    cap = hbm_ref.shape[0] // N_EXPERTS
    copies = []
    off = jnp.int32(0)
    for e in range(N_EXPERTS):
        base = e * cap + seg_ref[blk * 2 * N_EXPERTS + e]
        rows = seg_ref[blk * 2 * N_EXPERTS + N_EXPERTS + e]
        if to_hbm:
            copies += _segment_copies(rows, buf_ref, off, hbm_ref, base, sem)
        else:
            copies += _segment_copies(rows, hbm_ref, base, buf_ref, off, sem)
        off = off + rows
    return copies


def _route_kernel(h_ref, ffn_norm, rw_hi, rw_lo, rb, xs_hbm, meta_ref, seg_ref, tot_ref,
                  comp_scr, zero_scr, cum_ref, sem):
    k = pl.program_id(0)
    bs = h_ref.shape[0]
    cap = xs_hbm.shape[0] // N_EXPERTS
    lane = lax.broadcasted_iota(jnp.int32, (1, ROUTER_PAD), 1)

    @pl.when(k == 0)
    def _():
        for e in range(N_EXPERTS):
            cum_ref[e] = 0

    u = _rms(h_ref[...], ffn_norm[...])
    u_hi = u.astype(_BF)
    u_lo = (u - u_hi.astype(_F32)).astype(_BF)
    logits = _dot(u_hi, rw_hi[...]) + _dot(u_lo, rw_hi[...]) + _dot(u_hi, rw_lo[...]) + rb[...]
    logits = jnp.where(lane < N_EXPERTS, logits, -jnp.inf)
    m1 = jnp.max(logits, axis=1, keepdims=True)
    i1 = jnp.min(jnp.where(logits == m1, lane, ROUTER_PAD), axis=1, keepdims=True)
    rest = jnp.where(lane == i1, -jnp.inf, logits)
    m2 = jnp.max(rest, axis=1, keepdims=True)
    i2 = jnp.min(jnp.where(rest == m2, lane, ROUTER_PAD), axis=1, keepdims=True)
    t = jnp.exp(m2 - m1)
    w1 = 1.0 / (1.0 + t)
    w2 = t * w1

    a1 = lane == i1
    a2 = lane == i2
    assigned = jnp.where(a1 | a2, 1.0, 0.0)
    before = (lax.broadcasted_iota(jnp.int32, (bs, bs), 1) < lax.broadcasted_iota(jnp.int32, (bs, bs), 0))
    rank = _dot(jnp.where(before, 1.0, 0.0).astype(_BF), assigned.astype(_BF))
    count = jnp.sum(assigned, axis=0, keepdims=True).astype(jnp.int32)
    seg_off = jnp.zeros((1, ROUTER_PAD), _F32)
    off = jnp.int32(0)
    for e in range(N_EXPERTS):
        rows = ((count[0, e] + (SEG_ALIGN - 1)) // SEG_ALIGN) * SEG_ALIGN
        seg_off = jnp.where(lane == e, off.astype(_F32), seg_off)
        base = cum_ref[e]
        seg_ref[k * 2 * N_EXPERTS + e] = base
        seg_ref[k * 2 * N_EXPERTS + N_EXPERTS + e] = rows
        cum_ref[e] = base + rows
        off = off + rows
    dst1 = jnp.sum(jnp.where(a1, rank + seg_off, 0.0), axis=1, keepdims=True)
    dst2 = jnp.sum(jnp.where(a2, rank + seg_off, 0.0), axis=1, keepdims=True)
    meta = jnp.where(lane == 0, dst1, jnp.where(lane == 1, dst2, jnp.where(lane == 2, w1, jnp.where(lane == 3, w2, 0.0))))
    meta_ref[...] = meta

    by_token = meta.T
    row = lax.broadcasted_iota(jnp.int32, (SEG_ROWS, 1), 0).astype(_F32)
    from1 = row == by_token[0:1, :]
    from2 = row == by_token[1:2, :]
    onehot = jnp.where(from1 | from2, 1.0, 0.0).astype(_BF)
    gate = jnp.sum(jnp.where(from1, by_token[2:3, :], 0.0) + jnp.where(from2, by_token[3:4, :], 0.0),
                   axis=1, keepdims=True)
    hi = gate.astype(_BF).astype(_F32)
    mid = (gate - hi).astype(_BF).astype(_F32)
    gate_parts = jnp.where(lane == 0, hi, jnp.where(lane == 1, mid, jnp.where(lane == 2, gate - hi - mid, 0.0)))

    slot = k % 2
    comp = comp_scr.at[slot]
    comp[:, :D_MODEL] = _dot(onehot, u_hi).astype(_BF)
    comp[:, D_MODEL:] = gate_parts.astype(_BF)
    _run(_block_copies(seg_ref, k, xs_hbm, comp, sem.at[slot], to_hbm=True), "start")

    @pl.when(k > 0)
    def _():
        _run(_block_copies(seg_ref, k - 1, xs_hbm, comp_scr.at[1 - slot], sem.at[1 - slot], to_hbm=True), "wait")

    @pl.when(k == pl.num_programs(0) - 1)
    def _():
        _run(_block_copies(seg_ref, k, xs_hbm, comp, sem.at[slot], to_hbm=True), "wait")
        zero_scr[...] = jnp.zeros_like(zero_scr)
        fills = []
        for e in range(N_EXPERTS):
            total = cum_ref[e]
            tot_ref[e] = total
            fills += _segment_copies((-total) & (EXPERT_TILE - 1), zero_scr, 0, xs_hbm, e * cap + total,
                                     sem.at[slot], sizes=FILL_SIZES)
        _run(fills, "start")
        _run(fills, "wait")


def _expert_kernel(te_ref, tb_ref, rows_ref, x_ref, wa_f32, wg_f32, wd_f32, y_ref, wa, wg, wd):
    i = pl.program_id(0)

    @pl.when((i == 0) | (te_ref[i] != te_ref[jnp.maximum(i - 1, 0)]))
    def _():
        wa[...] = wa_f32[...].astype(_BF)
        wg[...] = wg_f32[...].astype(_BF)
        wd[...] = wd_f32[...].astype(_BF)

    for r0 in range(0, x_ref.shape[0], EXPERT_SUB):
        @pl.when(rows_ref[i] > r0)
        def _(r0=r0):
            x = x_ref[r0:r0 + EXPERT_SUB, :D_MODEL]
            gate = jnp.sum(x_ref[r0:r0 + EXPERT_SUB, D_MODEL:].astype(_F32), axis=1, keepdims=True)
            a = _dot(x, wa[...])
            g = _dot(x, wg[...])
            act = (a * _sigmoid(a) * g).astype(_BF)
            y_ref[r0:r0 + EXPERT_SUB, :] = (gate * _dot(act, wd[...])).astype(_BF)


def _combine_kernel(seg_ref, h_ref, p_ref, meta_ref, ys_hbm, gate_norm, gate_w, ple_w, out_ref, ycat, sem):
    k = pl.program_id(0)
    slot = k % 2
    fetch = lambda blk, s, op: _run(_block_copies(seg_ref, blk, ys_hbm, ycat.at[s], sem.at[s], to_hbm=False), op)

    @pl.when(k == 0)
    def _():
        ycat[...] = jnp.zeros_like(ycat)
        fetch(k, slot, "start")

    @pl.when(k + 1 < pl.num_programs(0))
    def _():
        fetch(k + 1, 1 - slot, "start")

    fetch(k, slot, "wait")

    pos = lax.broadcasted_iota(jnp.int32, (1, SEG_ROWS), 1).astype(_F32)
    meta = meta_ref[...]
    onehot = jnp.where((pos == meta[:, 0:1]) | (pos == meta[:, 1:2]), 1.0, 0.0).astype(_BF)
    h = h_ref[...] + _dot(onehot, ycat[slot])
    gate = _sigmoid(_dot(_rms(h, gate_norm[...]).astype(_BF), gate_w[...]))
    out_ref[...] = h + gate * _dot(p_ref[...].astype(_BF), ple_w[...])


def _const(shape):
    return pl.BlockSpec(shape, lambda *_: (0,) * len(shape), pipeline_mode=pl.Buffered(1))


def _row(v):
    return v.reshape(1, -1).astype(_F32)


def kernel(x, p, pool_norm, pool_w, pool_scale, kv_norm, w_kv, k_norm, attn_norm, w_q, q_norm, sinks, w_o,
           ffn_norm, w_gu, w_down, router_w, router_b, we_gu, we_down, ple_gate_norm, ple_gate_w, ple_w):
    b, s, d = x.shape
    ts, tq = SEQ_TILE, ATTN_TILE

    inv = ROPE_THETA ** (-jnp.arange(0, HEAD_DIM, 2, dtype=_F32) / HEAD_DIM)
    ang = jnp.arange(s, dtype=_F32)[:, None] * inv[None, :]
    cos_t = jnp.tile(jnp.cos(ang), (1, LANES // (HEAD_DIM // 2)))
    sin_t = jnp.tile(jnp.concatenate([-jnp.sin(ang), jnp.sin(ang)], axis=1), (1, LANES // HEAD_DIM))
    hid = jnp.arange(KV_DIM) // HEAD_DIM
    hmat = ((hid[:, None] == hid[None, :]) * (1.0 / HEAD_DIM)).astype(_BF)

    q_gain = jnp.tile(q_norm[0].astype(_F32) * (HEAD_DIM ** -0.5), N_HEADS).reshape(1, d)
    k_gain = jnp.tile(k_norm.astype(_F32), N_KV_HEADS).reshape(1, KV_DIM)

    tps = s // ts
    n_tiles0 = b * tps
    first = lambda st: jnp.minimum(st, n_tiles0 - 1)
    third = lambda st: jnp.maximum(st - 2, 0)
    tile3 = lambda w: pl.BlockSpec((None, ts, w), lambda st: (third(st) // tps, third(st) % tps, 0))
    h1, q, k, v = pl.pallas_call(
        functools.partial(_layer0_kernel, tiles_per_seq=tps),
        grid=(n_tiles0 + 2,),
        in_specs=[
            pl.BlockSpec((None, ts, d), lambda st: (first(st) // tps, first(st) % tps, 0)),
            pl.BlockSpec((None, None, ts, PLE_DIM), lambda st: (0, third(st) // tps, third(st) % tps, 0)),
            pl.BlockSpec((ts, LANES), lambda st: (third(st) % tps, 0)),
            pl.BlockSpec((ts, LANES), lambda st: (third(st) % tps, 0)),
            _const((1, d)), _const((len(POOL_WINDOWS), POOL_GROUP, POOL_GROUP)), _const((1, d)),
            _const((1, d)), _const((d, 2 * D_FF)), _const((D_FF, d)),
            _const((1, d)), _const((d, d)), _const((PLE_DIM, d)),
            _const((1, d)), _const((d, d)), _const((1, d)),
            _const((1, d)), _const((d, 2 * KV_DIM)), _const((1, KV_DIM)), _const((KV_DIM, KV_DIM)),
        ],
        out_specs=[tile3(d), tile3(d), tile3(KV_DIM), tile3(KV_DIM)],
        out_shape=[jax.ShapeDtypeStruct((b, s, d), _F32), jax.ShapeDtypeStruct((b, s, d), _BF),
                   jax.ShapeDtypeStruct((b, s, KV_DIM), _BF), jax.ShapeDtypeStruct((b, s, KV_DIM), _BF)],
        scratch_shapes=[pltpu.VMEM((POOL_HALO + ts, d), _F32)] + [pltpu.VMEM((POOL_ROWS + 24, d), _F32)] * 2
                       + [pltpu.VMEM((ts, d), _F32), pltpu.VMEM((ts, d), _BF)] * 2 + [pltpu.VMEM((ts, d), _F32)]
                       + [pltpu.VMEM((ts, D_FF), _BF)],
        compiler_params=pltpu.CompilerParams(dimension_semantics=("arbitrary",), vmem_limit_bytes=VMEM_LIMIT),
        name="layer0",
    )(x, p, cos_t, sin_t, _row(pool_norm[0]), pool_w[0].astype(_BF), _row(pool_scale[0]),
      _row(ffn_norm[0]), w_gu[0].astype(_BF), w_down[0].astype(_BF), _row(ple_gate_norm[0]), ple_gate_w[0].astype(_BF), ple_w[0].astype(_BF),
      _row(attn_norm[0]), w_q[0].astype(_BF), q_gain, _row(kv_norm), w_kv.astype(_BF), k_gain, hmat)

    blocks_per_tile = tq // WINDOW
    prev = lambda bi, qi, *_: (bi, jnp.maximum(qi * blocks_per_tile - 1, 0), 0)
    cur = lambda bi, qi, *_: (bi, qi, 0)
    h2 = pl.pallas_call(
        _attn_kernel,
        grid_spec=pltpu.PrefetchScalarGridSpec(
            num_scalar_prefetch=1,
            grid=(b, s // tq),
            in_specs=[
                pl.BlockSpec((None, tq, d), cur),
                pl.BlockSpec((None, WINDOW, KV_DIM), prev), pl.BlockSpec((None, tq, KV_DIM), cur),
                pl.BlockSpec((None, WINDOW, KV_DIM), prev), pl.BlockSpec((None, tq, KV_DIM), cur),
                pl.BlockSpec((None, tq, d), cur),
                pl.BlockSpec((d, d), lambda *_: (0, 0), pipeline_mode=pl.Buffered(1)),
            ],
            out_specs=pl.BlockSpec((None, tq, d), cur),
            scratch_shapes=[pltpu.VMEM((WINDOW + tq, LANES), _BF)] * 4 + [pltpu.VMEM((tq, d), _BF)],
        ),
        out_shape=jax.ShapeDtypeStruct((b, s, d), _F32),
        compiler_params=pltpu.CompilerParams(dimension_semantics=("arbitrary", "arbitrary"),
                                             vmem_limit_bytes=VMEM_LIMIT),
        name="attn",
    )(sinks[0].astype(_F32), q, k, k, v, v, h1, w_o[0].astype(_BF))

    t = b * s
    rw = jnp.pad(router_w[0].astype(_F32), ((0, 0), (0, ROUTER_PAD - N_EXPERTS)))
    rw_hi = rw.astype(_BF)
    rw_lo = (rw - rw_hi.astype(_F32)).astype(_BF)
    rb = jnp.pad(router_b[0].astype(_F32), (0, ROUTER_PAD - N_EXPERTS)).reshape(1, ROUTER_PAD)

    bs, te = ROUTE_TILE, EXPERT_TILE
    nblk = t // bs
    cap = t
    smem = pl.BlockSpec(memory_space=pltpu.SMEM)
    tok = lambda w: pl.BlockSpec((bs, w), lambda ki, *_: (ki, 0))
    xs, meta, seg, totals = pl.pallas_call(
        _route_kernel,
        grid=(nblk,),
        in_specs=[tok(d), _const((1, d)), _const((d, ROUTER_PAD)), _const((d, ROUTER_PAD)), _const((1, ROUTER_PAD))],
        out_specs=[pl.BlockSpec(memory_space=pl.ANY), tok(ROUTER_PAD), smem, smem],
        out_shape=[jax.ShapeDtypeStruct((N_EXPERTS * cap, SORTED_WIDTH), _BF),
                   jax.ShapeDtypeStruct((t, ROUTER_PAD), _F32),
                   jax.ShapeDtypeStruct((nblk * 2 * N_EXPERTS,), jnp.int32),
                   jax.ShapeDtypeStruct((N_EXPERTS,), jnp.int32)],
        scratch_shapes=[pltpu.VMEM((2, SEG_ROWS, SORTED_WIDTH), _BF), pltpu.VMEM((EXPERT_TILE, SORTED_WIDTH), _BF),
                        pltpu.SMEM((N_EXPERTS,), jnp.int32), pltpu.SemaphoreType.DMA((2,))],
        compiler_params=pltpu.CompilerParams(dimension_semantics=("arbitrary",), vmem_limit_bytes=VMEM_LIMIT),
        name="route",
    )(h2.reshape(t, d), _row(ffn_norm[1]), rw_hi, rw_lo, rb)

    n_steps = (2 * t + nblk * N_EXPERTS * (SEG_ALIGN - 1)) // te + N_EXPERTS
    tiles_e = (totals + te - 1) // te
    tile_end = jnp.cumsum(tiles_e)
    n_tiles = tile_end[-1]
    step = jnp.minimum(jnp.arange(n_steps, dtype=jnp.int32), n_tiles - 1)
    tile_expert = jnp.sum(step[:, None] >= tile_end[None, :], axis=1).astype(jnp.int32)
    local_tile = step - (tile_end - tiles_e)[tile_expert]
    tile_block = (tile_expert * (cap // te) + local_tile).astype(jnp.int32)
    tile_rows = jnp.where(jnp.arange(n_steps) < n_tiles,
                          jnp.clip(totals[tile_expert] - local_tile * te, 0, te), 0).astype(jnp.int32)

    xrow = lambda w: pl.BlockSpec((te, w), lambda i, te_r, tb_r, nt_r: (tb_r[i], 0))
    wexp = lambda r, c, cb: pl.BlockSpec((None, None, r, c), lambda i, te_r, tb_r, nt_r: (0, te_r[i], 0, cb))
    ys = pl.pallas_call(
        _expert_kernel,
        grid_spec=pltpu.PrefetchScalarGridSpec(
            num_scalar_prefetch=3,
            grid=(n_steps,),
            in_specs=[xrow(SORTED_WIDTH), wexp(d, D_FF_EXPERT, 0), wexp(d, D_FF_EXPERT, 1), wexp(D_FF_EXPERT, d, 0)],
            out_specs=xrow(d),
            scratch_shapes=[pltpu.VMEM((d, D_FF_EXPERT), _BF)] * 2 + [pltpu.VMEM((D_FF_EXPERT, d), _BF)],
        ),
        out_shape=jax.ShapeDtypeStruct((N_EXPERTS * cap, d), _BF),
        compiler_params=pltpu.CompilerParams(dimension_semantics=("arbitrary",), vmem_limit_bytes=VMEM_LIMIT),
        name="experts",
    )(tile_expert, tile_block, tile_rows, xs, we_gu, we_gu, we_down)

    out = pl.pallas_call(
        _combine_kernel,
        grid_spec=pltpu.PrefetchScalarGridSpec(
            num_scalar_prefetch=1,
            grid=(nblk,),
            in_specs=[tok(d),
                      pl.BlockSpec((None, None, bs, PLE_DIM), lambda ki, *_: (1, ki // (s // bs), ki % (s // bs), 0)),
                      tok(ROUTER_PAD), pl.BlockSpec(memory_space=pl.ANY),
                      _const((1, d)), _const((d, d)), _const((PLE_DIM, d))],
            out_specs=tok(d),
            scratch_shapes=[pltpu.VMEM((2, SEG_ROWS, d), _BF), pltpu.SemaphoreType.DMA((2,))],
        ),
        out_shape=jax.ShapeDtypeStruct((t, d), _F32),
        compiler_params=pltpu.CompilerParams(dimension_semantics=("arbitrary",), vmem_limit_bytes=VMEM_LIMIT),
        name="combine",
    )(seg, h2.reshape(t, d), p, meta, ys, _row(ple_gate_norm[1]),
      ple_gate_w[1].astype(_BF), ple_w[1].astype(_BF))
    return out.reshape(b, s, d)
```

```python
import functools

import jax
import jax.numpy as jnp
from jax import lax
from jax.experimental import pallas as pl
from jax.experimental.pallas import tpu as pltpu

D_MODEL = 1024
PLE_DIM = 256
POOL_WINDOWS = (2, 4, 8, 16)
POOL_GROUP = D_MODEL // len(POOL_WINDOWS)
POOL_HALO = 32
POOL_ROWS = 256
HEAD_DIM = 64
N_HEADS = 16
N_KV_HEADS = 4
KV_DIM = N_KV_HEADS * HEAD_DIM
WINDOW = 128
ROPE_THETA = 10000.0
D_FF = 2816
FF_CHUNK = 256
N_EXPERTS = 8
D_FF_EXPERT = 1024
EPS = 1e-6
LANES = 128
ROUTER_PAD = LANES

SEQ_TILE = 512
ATTN_TILE = 1024
ROUTE_TILE = 512
EXPERT_TILE = 1024
SEG_ALIGN = 16
SEG_SIZES = tuple(ROUTE_TILE >> i for i in range((ROUTE_TILE // SEG_ALIGN).bit_length()))
FILL_SIZES = tuple((EXPERT_TILE // 2) >> i for i in range((EXPERT_TILE // 2 // SEG_ALIGN).bit_length()))
SEG_ROWS = 2 * ROUTE_TILE + N_EXPERTS * SEG_ALIGN
GATE_LANES = LANES
SORTED_WIDTH = D_MODEL + GATE_LANES
VMEM_LIMIT = 60 * 1024 * 1024

_BF = jnp.bfloat16
_F32 = jnp.float32


def _rms(x, g):
    return x * lax.rsqrt(jnp.mean(x * x, axis=-1, keepdims=True) + EPS) * g


def _sigmoid(x):
    return 0.5 * jnp.tanh(0.5 * x) + 0.5


def _dot(a, b):
    return jnp.dot(a, b, preferred_element_type=_F32)


def _head_meansq(x, hmat):
    x2 = (x * x).astype(_BF)
    width = hmat.shape[0]
    return jnp.concatenate([_dot(x2[:, c:c + width], hmat) for c in range(0, x.shape[1], width)], axis=1)


def _rope(x, cos, sin_signed):
    n = x.shape[1]
    reps = n // LANES
    lane = lax.broadcasted_iota(jnp.int32, (1, n), 1)
    first_half = (lane % HEAD_DIM) < (HEAD_DIM // 2)
    partner = jnp.where(first_half, pltpu.roll(x, n - HEAD_DIM // 2, 1), pltpu.roll(x, HEAD_DIM // 2, 1))
    c = jnp.concatenate([cos] * reps, axis=1)
    s = jnp.concatenate([sin_signed] * reps, axis=1)
    return x * c + partner * s


def _layer0_kernel(x_ref, p_ref, cos_ref, sin_ref, pool_norm, pool_w, pool_scale, ffn_norm, wgu, wd,
                   gate_norm, gate_w, ple_w, attn_norm, wq, q_gain, kv_norm, wkv, k_gain, hmat,
                   h_out, q_out, k_out, v_out, ubuf, sum_a, sum_b, h1_cur, uf_cur, h1_next, uf_next, h2_buf,
                   act_scr, *, tiles_per_seq):
    step = pl.program_id(0)
    si = jnp.minimum(step, pl.num_programs(0) - 3) % tiles_per_seq
    ts = x_ref.shape[0]
    halo, end = POOL_HALO, POOL_HALO + ts
    n_chunks = D_FF // FF_CHUNK
    assert ts % POOL_ROWS == 0 and ts // POOL_ROWS <= n_chunks

    @pl.when(step == 0)
    def _():
        h1_cur[...] = jnp.zeros_like(h1_cur)
        uf_cur[...] = jnp.zeros_like(uf_cur)
        h2_buf[...] = jnp.zeros_like(h2_buf)

    @pl.when(si == 0)
    def _():
        ubuf[0:halo, :] = jnp.zeros((halo, D_MODEL), _F32)

    def first_stage_rows(r0, n):
        a, b = halo + r0, halo + r0 + n
        xr = x_ref[r0:r0 + n, :]
        u = _rms(xr, pool_norm[...])
        ubuf[a:b, :] = u
        g1, g2, g3 = POOL_GROUP, 2 * POOL_GROUP, 3 * POOL_GROUP
        m = n + 24
        s2 = ubuf[a - 24:b, :] + ubuf[a - 25:b - 1, :]
        sum_a[0:m, g1:] = s2[:, g1:]
        s4 = sum_a[8:m, g1:] + sum_a[6:m - 2, g1:]
        sum_b[8:m, g2:] = s4[:, g1:]
        s8 = sum_b[16:m, g2:] + sum_b[12:m - 4, g2:]
        sum_a[16:m, g3:] = s8[:, g1:]
        s16 = sum_a[24:m, g3:] + sum_a[16:m - 8, g3:]
        window_sums = (s2[24:, :g1], s4[16:, :g1], s8[8:, :g1], s16)
        pos = (si * ts + r0 + 1 + lax.broadcasted_iota(jnp.int32, (n, 1), 0)).astype(_F32)
        mixed = []
        for g, win in enumerate(POOL_WINDOWS):
            ug = u[:, g * POOL_GROUP:(g + 1) * POOL_GROUP]
            d = window_sums[g] / jnp.minimum(pos, float(win)) - ug
            mixed.append(_dot(d.astype(_BF), pool_w[g]))
        h_mixed = xr + jnp.concatenate(mixed, axis=1) * pool_scale[...]
        h1_next[r0:r0 + n, :] = h_mixed
        uf_next[r0:r0 + n, :] = _rms(h_mixed, ffn_norm[...]).astype(_BF)

    t = {}

    def embed_gate_matmul():
        t["h"] = h2_buf[...]
        t["gate_pre"] = _dot(_rms(t["h"], gate_norm[...]).astype(_BF), gate_w[...])

    def embed_and_unit_norm():
        h = t["h"] + _sigmoid(t["gate_pre"]) * _dot(p_ref[...].astype(_BF), ple_w[...])
        h_out[...] = h
        t["unit"] = h * lax.rsqrt(jnp.mean(h * h, axis=-1, keepdims=True) + EPS)

    def q_matmul():
        t["q"] = _dot((t["unit"] * attn_norm[...]).astype(_BF), wq[...])

    def q_norm_rope():
        q = t["q"] * lax.rsqrt(_head_meansq(t["q"], hmat[...]) + EPS) * q_gain[...]
        q_out[...] = _rope(q, cos_ref[...], sin_ref[...]).astype(_BF)

    def kv_matmul():
        t["kv"] = _dot((t["unit"] * kv_norm[...]).astype(_BF), wkv[...])

    def k_norm_rope():
        k = t["kv"][:, :KV_DIM]
        k = k * lax.rsqrt(_head_meansq(k, hmat[...]) + EPS) * k_gain[...]
        k_out[...] = _rope(k, cos_ref[...], sin_ref[...]).astype(_BF)
        v_out[...] = t["kv"][:, KV_DIM:].astype(_BF)

    third_stage = (embed_gate_matmul, embed_and_unit_norm, q_matmul, q_norm_rope, kv_matmul, k_norm_rope)
    assert len(third_stage) <= n_chunks

    uf = uf_cur[...]
    for c in range(n_chunks):
        c0 = c * FF_CHUNK
        a = _dot(uf, wgu[:, c0:c0 + FF_CHUNK])
        g = _dot(uf, wgu[:, D_FF + c0:D_FF + c0 + FF_CHUNK])
        act_scr[:, c0:c0 + FF_CHUNK] = (a * _sigmoid(a) * g).astype(_BF)
        if c < len(third_stage):
            third_stage[c]()
        if c * POOL_ROWS < ts:
            first_stage_rows(c * POOL_ROWS, POOL_ROWS)
    acc = _dot(act_scr[...], wd[...])

    h2_buf[...] = h1_cur[...] + acc
    ubuf[0:halo, :] = ubuf[ts:end, :]
    h1_cur[...] = h1_next[...]
    uf_cur[...] = uf_next[...]


def _attn_kernel(sinks_ref, q_ref, kp_ref, kc_ref, vp_ref, vc_ref, h_ref, wo_ref, out_ref,
                 klo, khi, vlo, vhi, o_scr):
    i = pl.program_id(1)
    tq = q_ref.shape[0]
    nblk = tq // WINDOW
    lane = lax.broadcasted_iota(jnp.int32, (1, LANES), 1)
    low = lane < HEAD_DIM

    qi = lax.broadcasted_iota(jnp.int32, (2 * WINDOW, 2 * WINDOW), 0) % WINDOW
    kj = lax.broadcasted_iota(jnp.int32, (2 * WINDOW, 2 * WINDOW), 1)
    in_window = (kj > qi) & (kj <= qi + WINDOW)
    upper_rows = lax.broadcasted_iota(jnp.int32, (2 * WINDOW, 1), 0) >= WINDOW

    for g in range(N_KV_HEADS):
        slab = slice((g // 2) * LANES, (g // 2 + 1) * LANES)
        for src_p, src_c, lo_ref, hi_ref in ((kp_ref, kc_ref, klo, khi), (vp_ref, vc_ref, vlo, vhi)):
            t = jnp.concatenate([src_p[:, slab], src_c[:, slab]], axis=0).astype(_F32)
            r = pltpu.roll(t, HEAD_DIM, 1)
            in_low, in_high = (t, r) if g % 2 == 0 else (r, t)
            lo_ref[...] = jnp.where(low, in_low, 0.0).astype(_BF)
            hi_ref[...] = jnp.where(low, 0.0, in_high).astype(_BF)

        for j in range(nblk):
            qrows = pl.ds(j * WINDOW, WINDOW)
            krows = pl.ds(j * WINDOW, 2 * WINDOW)
            qg = jnp.concatenate([q_ref[qrows, (2 * g) * LANES:(2 * g + 1) * LANES],
                                  q_ref[qrows, (2 * g + 1) * LANES:(2 * g + 2) * LANES]], axis=0)
            mask = in_window & (kj >= jnp.where(i > 0, 0, WINDOW)) if j == 0 else in_window
            o = jnp.zeros((2 * WINDOW, LANES), _F32)
            for half, (k_ref, v_ref) in enumerate(((klo, vlo), (khi, vhi))):
                sink = jnp.where(upper_rows, sinks_ref[4 * g + 2 + half], sinks_ref[4 * g + half])
                s = lax.dot_general(qg, k_ref[krows, :], (((1,), (1,)), ((), ())),
                                    preferred_element_type=_F32)
                s = jnp.where(mask, s, -jnp.inf)
                m = jnp.maximum(jnp.max(s, axis=1, keepdims=True), sink)
                pr = jnp.exp(s - m)
                denom = jnp.sum(pr, axis=1, keepdims=True) + jnp.exp(sink - m)
                o = o + _dot(pr.astype(_BF), v_ref[krows, :]) / denom
            o_scr[qrows, (2 * g) * LANES:(2 * g + 1) * LANES] = o[:WINDOW].astype(_BF)
            o_scr[qrows, (2 * g + 1) * LANES:(2 * g + 2) * LANES] = o[WINDOW:].astype(_BF)

    out_ref[...] = h_ref[...] + _dot(o_scr[...], wo_ref[...])


def _segment_copies(rows, src_ref, src_base, dst_ref, dst_base, sem, sizes=SEG_SIZES):
    out = []
    for sz in sizes:
        done = rows & (-2 * sz)
        src = src_ref.at[pl.ds(pl.multiple_of(src_base + done, SEG_ALIGN), sz)]
        dst = dst_ref.at[pl.ds(pl.multiple_of(dst_base + done, SEG_ALIGN), sz)]
        out.append(((rows & sz) != 0, pltpu.make_async_copy(src, dst, sem)))
    return out


def _run(copies, op):
    for cond, cp in copies:
        @pl.when(cond)
        def _(cp=cp):
            getattr(cp, op)()


def _block_copies(seg_ref, blk, hbm_ref, buf_ref, sem, to_hbm):
    cap = hbm_ref.shape[0] // N_EXPERTS
    copies = []
    off = jnp.int32(0)
    for e in range(N_EXPERTS):
        base = e * cap + seg_ref[blk * 2 * N_EXPERTS + e]
        rows = seg_ref[blk * 2 * N_EXPERTS + N_EXPERTS + e]
        if to_hbm:
            copies += _segment_copies(rows, buf_ref, off, hbm_ref, base, sem)
        else:
            copies += _segment_copies(rows, hbm_ref, base, buf_ref, off, sem)
        off = off + rows
    return copies


def _route_kernel(h_ref, ffn_norm, rw_hi, rw_lo, rb, xs_hbm, meta_ref, seg_ref, tot_ref,
                  comp_scr, zero_scr, cum_ref, sem):
    k = pl.program_id(0)
    bs = h_ref.shape[0]
    cap = xs_hbm.shape[0] // N_EXPERTS
    lane = lax.broadcasted_iota(jnp.int32, (1, ROUTER_PAD), 1)

    @pl.when(k == 0)
    def _():
        for e in range(N_EXPERTS):
            cum_ref[e] = 0

    u = _rms(h_ref[...], ffn_norm[...])
    u_hi = u.astype(_BF)
    u_lo = (u - u_hi.astype(_F32)).astype(_BF)
    logits = _dot(u_hi, rw_hi[...]) + _dot(u_lo, rw_hi[...]) + _dot(u_hi, rw_lo[...]) + rb[...]
    logits = jnp.where(lane < N_EXPERTS, logits, -jnp.inf)
    m1 = jnp.max(logits, axis=1, keepdims=True)
    i1 = jnp.min(jnp.where(logits == m1, lane, ROUTER_PAD), axis=1, keepdims=True)
    rest = jnp.where(lane == i1, -jnp.inf, logits)
    m2 = jnp.max(rest, axis=1, keepdims=True)
    i2 = jnp.min(jnp.where(rest == m2, lane, ROUTER_PAD), axis=1, keepdims=True)
    t = jnp.exp(m2 - m1)
    w1 = 1.0 / (1.0 + t)
    w2 = t * w1

    a1 = lane == i1
    a2 = lane == i2
    assigned = jnp.where(a1 | a2, 1.0, 0.0)
    before = (lax.broadcasted_iota(jnp.int32, (bs, bs), 1) < lax.broadcasted_iota(jnp.int32, (bs, bs), 0))
    rank = _dot(jnp.where(before, 1.0, 0.0).astype(_BF), assigned.astype(_BF))
    count = jnp.sum(assigned, axis=0, keepdims=True).astype(jnp.int32)
    seg_off = jnp.zeros((1, ROUTER_PAD), _F32)
    off = jnp.int32(0)
    for e in range(N_EXPERTS):
        rows = ((count[0, e] + (SEG_ALIGN - 1)) // SEG_ALIGN) * SEG_ALIGN
        seg_off = jnp.where(lane == e, off.astype(_F32), seg_off)
        base = cum_ref[e]
        seg_ref[k * 2 * N_EXPERTS + e] = base
        seg_ref[k * 2 * N_EXPERTS + N_EXPERTS + e] = rows
        cum_ref[e] = base + rows
        off = off + rows
    dst1 = jnp.sum(jnp.where(a1, rank + seg_off, 0.0), axis=1, keepdims=True)
    dst2 = jnp.sum(jnp.where(a2, rank + seg_off, 0.0), axis=1, keepdims=True)
    meta = jnp.where(lane == 0, dst1, jnp.where(lane == 1, dst2, jnp.where(lane == 2, w1, jnp.where(lane == 3, w2, 0.0))))
    meta_ref[...] = meta

    by_token = meta.T
    row = lax.broadcasted_iota(jnp.int32, (SEG_ROWS, 1), 0).astype(_F32)
    from1 = row == by_token[0:1, :]
    from2 = row == by_token[1:2, :]
    onehot = jnp.where(from1 | from2, 1.0, 0.0).astype(_BF)
    gate = jnp.sum(jnp.where(from1, by_token[2:3, :], 0.0) + jnp.where(from2, by_token[3:4, :], 0.0),
                   axis=1, keepdims=True)
    hi = gate.astype(_BF).astype(_F32)
    mid = (gate - hi).astype(_BF).astype(_F32)
    gate_parts = jnp.where(lane == 0, hi, jnp.where(lane == 1, mid, jnp.where(lane == 2, gate - hi - mid, 0.0)))

    slot = k % 2
    comp = comp_scr.at[slot]
    comp[:, :D_MODEL] = _dot(onehot, u_hi).astype(_BF)
    comp[:, D_MODEL:] = gate_parts.astype(_BF)
    _run(_block_copies(seg_ref, k, xs_hbm, comp, sem.at[slot], to_hbm=True), "start")

    @pl.when(k > 0)
    def _():
        _run(_block_copies(seg_ref, k - 1, xs_hbm, comp_scr.at[1 - slot], sem.at[1 - slot], to_hbm=True), "wait")

    @pl.when(k == pl.num_programs(0) - 1)
    def _():
        _run(_block_copies(seg_ref, k, xs_hbm, comp, sem.at[slot], to_hbm=True), "wait")
        zero_scr[...] = jnp.zeros_like(zero_scr)
        fills = []
        for e in range(N_EXPERTS):
            total = cum_ref[e]
            tot_ref[e] = total
            fills += _segment_copies((-total) & (EXPERT_TILE - 1), zero_scr, 0, xs_hbm, e * cap + total,
                                     sem.at[slot], sizes=FILL_SIZES)
        _run(fills, "start")
        _run(fills, "wait")


def _expert_kernel(te_ref, tb_ref, nt_ref, x_ref, wa_f32, wg_f32, wd_f32, y_ref, wa, wg, wd):
    i = pl.program_id(0)

    @pl.when((i == 0) | (te_ref[i] != te_ref[jnp.maximum(i - 1, 0)]))
    def _():
        wa[...] = wa_f32[...].astype(_BF)
        wg[...] = wg_f32[...].astype(_BF)
        wd[...] = wd_f32[...].astype(_BF)

    @pl.when(i < nt_ref[0])
    def _():
        x = x_ref[:, :D_MODEL]
        gate = jnp.sum(x_ref[:, D_MODEL:].astype(_F32), axis=1, keepdims=True)
        a = _dot(x, wa[...])
        g = _dot(x, wg[...])
        act = (a * _sigmoid(a) * g).astype(_BF)
        y_ref[...] = (gate * _dot(act, wd[...])).astype(_BF)


def _combine_kernel(seg_ref, h_ref, p_ref, meta_ref, ys_hbm, gate_norm, gate_w, ple_w, out_ref, ycat, sem):
    k = pl.program_id(0)
    slot = k % 2
    fetch = lambda blk, s, op: _run(_block_copies(seg_ref, blk, ys_hbm, ycat.at[s], sem.at[s], to_hbm=False), op)

    @pl.when(k == 0)
    def _():
        ycat[...] = jnp.zeros_like(ycat)
        fetch(k, slot, "start")

    @pl.when(k + 1 < pl.num_programs(0))
    def _():
        fetch(k + 1, 1 - slot, "start")

    fetch(k, slot, "wait")

    pos = lax.broadcasted_iota(jnp.int32, (1, SEG_ROWS), 1).astype(_F32)
    meta = meta_ref[...]
    onehot = jnp.where((pos == meta[:, 0:1]) | (pos == meta[:, 1:2]), 1.0, 0.0).astype(_BF)
    h = h_ref[...] + _dot(onehot, ycat[slot])
    gate = _sigmoid(_dot(_rms(h, gate_norm[...]).astype(_BF), gate_w[...]))
    out_ref[...] = h + gate * _dot(p_ref[...].astype(_BF), ple_w[...])


def _const(shape):
    return pl.BlockSpec(shape, lambda *_: (0,) * len(shape), pipeline_mode=pl.Buffered(1))


def _row(v):
    return v.reshape(1, -1).astype(_F32)


def kernel(x, p, pool_norm, pool_w, pool_scale, kv_norm, w_kv, k_norm, attn_norm, w_q, q_norm, sinks, w_o,
           ffn_norm, w_gu, w_down, router_w, router_b, we_gu, we_down, ple_gate_norm, ple_gate_w, ple_w):
    b, s, d = x.shape
    ts, tq = SEQ_TILE, ATTN_TILE

    inv = ROPE_THETA ** (-jnp.arange(0, HEAD_DIM, 2, dtype=_F32) / HEAD_DIM)
    ang = jnp.arange(s, dtype=_F32)[:, None] * inv[None, :]
    cos_t = jnp.tile(jnp.cos(ang), (1, LANES // (HEAD_DIM // 2)))
    sin_t = jnp.tile(jnp.concatenate([-jnp.sin(ang), jnp.sin(ang)], axis=1), (1, LANES // HEAD_DIM))
    hid = jnp.arange(KV_DIM) // HEAD_DIM
    hmat = ((hid[:, None] == hid[None, :]) * (1.0 / HEAD_DIM)).astype(_BF)

    q_gain = jnp.tile(q_norm[0].astype(_F32) * (HEAD_DIM ** -0.5), N_HEADS).reshape(1, d)
    k_gain = jnp.tile(k_norm.astype(_F32), N_KV_HEADS).reshape(1, KV_DIM)

    tps = s // ts
    n_tiles0 = b * tps
    first = lambda st: jnp.minimum(st, n_tiles0 - 1)
    third = lambda st: jnp.maximum(st - 2, 0)
    tile3 = lambda w: pl.BlockSpec((None, ts, w), lambda st: (third(st) // tps, third(st) % tps, 0))
    h1, q, k, v = pl.pallas_call(
        functools.partial(_layer0_kernel, tiles_per_seq=tps),
        grid=(n_tiles0 + 2,),
        in_specs=[
            pl.BlockSpec((None, ts, d), lambda st: (first(st) // tps, first(st) % tps, 0)),
            pl.BlockSpec((None, None, ts, PLE_DIM), lambda st: (0, third(st) // tps, third(st) % tps, 0)),
            pl.BlockSpec((ts, LANES), lambda st: (third(st) % tps, 0)),
            pl.BlockSpec((ts, LANES), lambda st: (third(st) % tps, 0)),
            _const((1, d)), _const((len(POOL_WINDOWS), POOL_GROUP, POOL_GROUP)), _const((1, d)),
            _const((1, d)), _const((d, 2 * D_FF)), _const((D_FF, d)),
            _const((1, d)), _const((d, d)), _const((PLE_DIM, d)),
            _const((1, d)), _const((d, d)), _const((1, d)),
            _const((1, d)), _const((d, 2 * KV_DIM)), _const((1, KV_DIM)), _const((KV_DIM, KV_DIM)),
        ],
        out_specs=[tile3(d), tile3(d), tile3(KV_DIM), tile3(KV_DIM)],
        out_shape=[jax.ShapeDtypeStruct((b, s, d), _F32), jax.ShapeDtypeStruct((b, s, d), _BF),
                   jax.ShapeDtypeStruct((b, s, KV_DIM), _BF), jax.ShapeDtypeStruct((b, s, KV_DIM), _BF)],
        scratch_shapes=[pltpu.VMEM((POOL_HALO + ts, d), _F32)] + [pltpu.VMEM((POOL_ROWS + 24, d), _F32)] * 2
                       + [pltpu.VMEM((ts, d), _F32), pltpu.VMEM((ts, d), _BF)] * 2 + [pltpu.VMEM((ts, d), _F32)]
                       + [pltpu.VMEM((ts, D_FF), _BF)],
        compiler_params=pltpu.CompilerParams(dimension_semantics=("arbitrary",), vmem_limit_bytes=VMEM_LIMIT),
        name="layer0",
    )(x, p, cos_t, sin_t, _row(pool_norm[0]), pool_w[0].astype(_BF), _row(pool_scale[0]),
      _row(ffn_norm[0]), w_gu[0].astype(_BF), w_down[0].astype(_BF), _row(ple_gate_norm[0]), ple_gate_w[0].astype(_BF), ple_w[0].astype(_BF),
      _row(attn_norm[0]), w_q[0].astype(_BF), q_gain, _row(kv_norm), w_kv.astype(_BF), k_gain, hmat)

    blocks_per_tile = tq // WINDOW
    prev = lambda bi, qi, *_: (bi, jnp.maximum(qi * blocks_per_tile - 1, 0), 0)
    cur = lambda bi, qi, *_: (bi, qi, 0)
    h2 = pl.pallas_call(
        _attn_kernel,
        grid_spec=pltpu.PrefetchScalarGridSpec(
            num_scalar_prefetch=1,
            grid=(b, s // tq),
            in_specs=[
                pl.BlockSpec((None, tq, d), cur),
                pl.BlockSpec((None, WINDOW, KV_DIM), prev), pl.BlockSpec((None, tq, KV_DIM), cur),
                pl.BlockSpec((None, WINDOW, KV_DIM), prev), pl.BlockSpec((None, tq, KV_DIM), cur),
                pl.BlockSpec((None, tq, d), cur),
                pl.BlockSpec((d, d), lambda *_: (0, 0), pipeline_mode=pl.Buffered(1)),
            ],
            out_specs=pl.BlockSpec((None, tq, d), cur),
            scratch_shapes=[pltpu.VMEM((WINDOW + tq, LANES), _BF)] * 4 + [pltpu.VMEM((tq, d), _BF)],
        ),
        out_shape=jax.ShapeDtypeStruct((b, s, d), _F32),
        compiler_params=pltpu.CompilerParams(dimension_semantics=("arbitrary", "arbitrary"),
                                             vmem_limit_bytes=VMEM_LIMIT),
        name="attn",
    )(sinks[0].astype(_F32), q, k, k, v, v, h1, w_o[0].astype(_BF))

    t = b * s
    rw = jnp.pad(router_w[0].astype(_F32), ((0, 0), (0, ROUTER_PAD - N_EXPERTS)))
    rw_hi = rw.astype(_BF)
    rw_lo = (rw - rw_hi.astype(_F32)).astype(_BF)
    rb = jnp.pad(router_b[0].astype(_F32), (0, ROUTER_PAD - N_EXPERTS)).reshape(1, ROUTER_PAD)

    bs, te = ROUTE_TILE, EXPERT_TILE
    nblk = t // bs
    cap = t
    smem = pl.BlockSpec(memory_space=pltpu.SMEM)
    tok = lambda w: pl.BlockSpec((bs, w), lambda ki, *_: (ki, 0))
    xs, meta, seg, totals = pl.pallas_call(
        _route_kernel,
        grid=(nblk,),
        in_specs=[tok(d), _const((1, d)), _const((d, ROUTER_PAD)), _const((d, ROUTER_PAD)), _const((1, ROUTER_PAD))],
        out_specs=[pl.BlockSpec(memory_space=pl.ANY), tok(ROUTER_PAD), smem, smem],
        out_shape=[jax.ShapeDtypeStruct((N_EXPERTS * cap, SORTED_WIDTH), _BF),
                   jax.ShapeDtypeStruct((t, ROUTER_PAD), _F32),
                   jax.ShapeDtypeStruct((nblk * 2 * N_EXPERTS,), jnp.int32),
                   jax.ShapeDtypeStruct((N_EXPERTS,), jnp.int32)],
        scratch_shapes=[pltpu.VMEM((2, SEG_ROWS, SORTED_WIDTH), _BF), pltpu.VMEM((EXPERT_TILE, SORTED_WIDTH), _BF),
                        pltpu.SMEM((N_EXPERTS,), jnp.int32), pltpu.SemaphoreType.DMA((2,))],
        compiler_params=pltpu.CompilerParams(dimension_semantics=("arbitrary",), vmem_limit_bytes=VMEM_LIMIT),
        name="route",
    )(h2.reshape(t, d), _row(ffn_norm[1]), rw_hi, rw_lo, rb)

    n_steps = (2 * t + nblk * N_EXPERTS * (SEG_ALIGN - 1)) // te + N_EXPERTS
    tiles_e = (totals + te - 1) // te
    tile_end = jnp.cumsum(tiles_e)
    n_tiles = tile_end[-1]
    step = jnp.minimum(jnp.arange(n_steps, dtype=jnp.int32), n_tiles - 1)
    tile_expert = jnp.sum(step[:, None] >= tile_end[None, :], axis=1).astype(jnp.int32)
    tile_block = (tile_expert * (cap // te) + step - (tile_end - tiles_e)[tile_expert]).astype(jnp.int32)

    xrow = lambda w: pl.BlockSpec((te, w), lambda i, te_r, tb_r, nt_r: (tb_r[i], 0))
    wexp = lambda r, c, cb: pl.BlockSpec((None, None, r, c), lambda i, te_r, tb_r, nt_r: (0, te_r[i], 0, cb))
    ys = pl.pallas_call(
        _expert_kernel,
        grid_spec=pltpu.PrefetchScalarGridSpec(
            num_scalar_prefetch=3,
            grid=(n_steps,),
            in_specs=[xrow(SORTED_WIDTH), wexp(d, D_FF_EXPERT, 0), wexp(d, D_FF_EXPERT, 1), wexp(D_FF_EXPERT, d, 0)],
            out_specs=xrow(d),
            scratch_shapes=[pltpu.VMEM((d, D_FF_EXPERT), _BF)] * 2 + [pltpu.VMEM((D_FF_EXPERT, d), _BF)],
        ),
        out_shape=jax.ShapeDtypeStruct((N_EXPERTS * cap, d), _BF),
        compiler_params=pltpu.CompilerParams(dimension_semantics=("arbitrary",), vmem_limit_bytes=VMEM_LIMIT),
        name="experts",
    )(tile_expert, tile_block, n_tiles.reshape(1).astype(jnp.int32), xs, we_gu, we_gu, we_down)

    out = pl.pallas_call(
        _combine_kernel,
        grid_spec=pltpu.PrefetchScalarGridSpec(
            num_scalar_prefetch=1,
            grid=(nblk,),
            in_specs=[tok(d),
                      pl.BlockSpec((None, None, bs, PLE_DIM), lambda ki, *_: (1, ki // (s // bs), ki % (s // bs), 0)),
                      tok(ROUTER_PAD), pl.BlockSpec(memory_space=pl.ANY),
                      _const((1, d)), _const((d, d)), _const((PLE_DIM, d))],
            out_specs=tok(d),
            scratch_shapes=[pltpu.VMEM((2, SEG_ROWS, d), _BF), pltpu.SemaphoreType.DMA((2,))],
        ),
        out_shape=jax.ShapeDtypeStruct((t, d), _F32),
        compiler_params=pltpu.CompilerParams(dimension_semantics=("arbitrary",), vmem_limit_bytes=VMEM_LIMIT),
        name="combine",
    )(seg, h2.reshape(t, d), p, meta, ys, _row(ple_gate_norm[1]),
      ple_gate_w[1].astype(_BF), ple_w[1].astype(_BF))
    return out.reshape(b, s, d)
```

```python
import functools

import jax
import jax.numpy as jnp
from jax import lax
from jax.experimental import pallas as pl
from jax.experimental.pallas import tpu as pltpu

D_MODEL = 1024
PLE_DIM = 256
POOL_WINDOWS = (2, 4, 8, 16)
POOL_GROUP = D_MODEL // len(POOL_WINDOWS)
POOL_HALO = 32
POOL_ROWS = 512
HEAD_DIM = 64
N_HEADS = 16
N_KV_HEADS = 4
KV_DIM = N_KV_HEADS * HEAD_DIM
WINDOW = 128
ROPE_THETA = 10000.0
D_FF = 2816
FF_CHUNK = 256
N_EXPERTS = 8
D_FF_EXPERT = 1024
EPS = 1e-6
LANES = 128
ROUTER_PAD = LANES

SEQ_TILE = 512
ATTN_TILE = 1024
ROUTE_TILE = 512
EXPERT_TILE = 1024
SEG_ALIGN = 16
SEG_SIZES = tuple(ROUTE_TILE >> i for i in range((ROUTE_TILE // SEG_ALIGN).bit_length()))
FILL_SIZES = tuple((EXPERT_TILE // 2) >> i for i in range((EXPERT_TILE // 2 // SEG_ALIGN).bit_length()))
SEG_ROWS = 2 * ROUTE_TILE + N_EXPERTS * SEG_ALIGN
GATE_LANES = LANES
SORTED_WIDTH = D_MODEL + GATE_LANES
VMEM_LIMIT = 60 * 1024 * 1024

_BF = jnp.bfloat16
_F32 = jnp.float32


def _rms(x, g):
    return x * lax.rsqrt(jnp.mean(x * x, axis=-1, keepdims=True) + EPS) * g


def _sigmoid(x):
    return 0.5 * jnp.tanh(0.5 * x) + 0.5


def _dot(a, b):
    return jnp.dot(a, b, preferred_element_type=_F32)


def _head_meansq(x, hmat):
    x2 = (x * x).astype(_BF)
    width = hmat.shape[0]
    return jnp.concatenate([_dot(x2[:, c:c + width], hmat) for c in range(0, x.shape[1], width)], axis=1)


def _rope(x, cos, sin_signed):
    n = x.shape[1]
    reps = n // LANES
    lane = lax.broadcasted_iota(jnp.int32, (1, n), 1)
    first_half = (lane % HEAD_DIM) < (HEAD_DIM // 2)
    partner = jnp.where(first_half, pltpu.roll(x, n - HEAD_DIM // 2, 1), pltpu.roll(x, HEAD_DIM // 2, 1))
    c = jnp.concatenate([cos] * reps, axis=1)
    s = jnp.concatenate([sin_signed] * reps, axis=1)
    return x * c + partner * s


def _layer0_kernel(x_ref, p_ref, cos_ref, sin_ref, pool_norm, pool_w, pool_scale, ffn_norm, wgu, wd,
                   gate_norm, gate_w, ple_w, attn_norm, wq, q_gain, kv_norm, wkv, k_gain, hmat,
                   h_out, q_out, k_out, v_out, ubuf, sum_a, sum_b, h1_cur, uf_cur, h1_next, uf_next, h2_buf,
                   act_scr, *, tiles_per_seq):
    step = pl.program_id(0)
    si = jnp.minimum(step, pl.num_programs(0) - 3) % tiles_per_seq
    ts = x_ref.shape[0]
    halo, end = POOL_HALO, POOL_HALO + ts
    n_chunks = D_FF // FF_CHUNK
    assert ts % POOL_ROWS == 0 and ts // POOL_ROWS <= n_chunks

    @pl.when(step == 0)
    def _():
        h1_cur[...] = jnp.zeros_like(h1_cur)
        uf_cur[...] = jnp.zeros_like(uf_cur)
        h2_buf[...] = jnp.zeros_like(h2_buf)

    @pl.when(si == 0)
    def _():
        ubuf[0:halo, :] = jnp.zeros((halo, D_MODEL), _F32)

    def first_stage_rows(r0, n):
        a, b = halo + r0, halo + r0 + n
        xr = x_ref[r0:r0 + n, :]
        u = _rms(xr, pool_norm[...])
        ubuf[a:b, :] = u
        g1, g2, g3 = POOL_GROUP, 2 * POOL_GROUP, 3 * POOL_GROUP
        m = n + 24
        s2 = ubuf[a - 24:b, :] + ubuf[a - 25:b - 1, :]
        sum_a[0:m, g1:] = s2[:, g1:]
        s4 = sum_a[8:m, g1:] + sum_a[6:m - 2, g1:]
        sum_b[8:m, g2:] = s4[:, g1:]
        s8 = sum_b[16:m, g2:] + sum_b[12:m - 4, g2:]
        sum_a[16:m, g3:] = s8[:, g1:]
        s16 = sum_a[24:m, g3:] + sum_a[16:m - 8, g3:]
        window_sums = (s2[24:, :g1], s4[16:, :g1], s8[8:, :g1], s16)
        pos = (si * ts + r0 + 1 + lax.broadcasted_iota(jnp.int32, (n, 1), 0)).astype(_F32)
        mixed = []
        for g, win in enumerate(POOL_WINDOWS):
            ug = u[:, g * POOL_GROUP:(g + 1) * POOL_GROUP]
            d = window_sums[g] / jnp.minimum(pos, float(win)) - ug
            mixed.append(_dot(d.astype(_BF), pool_w[g]))
        h_mixed = xr + jnp.concatenate(mixed, axis=1) * pool_scale[...]
        h1_next[r0:r0 + n, :] = h_mixed
        uf_next[r0:r0 + n, :] = _rms(h_mixed, ffn_norm[...]).astype(_BF)

    t = {}

    def embed_gate_matmul():
        t["h"] = h2_buf[...]
        t["gate_pre"] = _dot(_rms(t["h"], gate_norm[...]).astype(_BF), gate_w[...])

    def embed_and_unit_norm():
        h = t["h"] + _sigmoid(t["gate_pre"]) * _dot(p_ref[...].astype(_BF), ple_w[...])
        h_out[...] = h
        t["unit"] = h * lax.rsqrt(jnp.mean(h * h, axis=-1, keepdims=True) + EPS)

    def q_matmul():
        t["q"] = _dot((t["unit"] * attn_norm[...]).astype(_BF), wq[...])

    def q_norm_rope():
        q = t["q"] * lax.rsqrt(_head_meansq(t["q"], hmat[...]) + EPS) * q_gain[...]
        q_out[...] = _rope(q, cos_ref[...], sin_ref[...]).astype(_BF)

    def kv_matmul():
        t["kv"] = _dot((t["unit"] * kv_norm[...]).astype(_BF), wkv[...])

    def k_norm_rope():
        k = t["kv"][:, :KV_DIM]
        k = k * lax.rsqrt(_head_meansq(k, hmat[...]) + EPS) * k_gain[...]
        k_out[...] = _rope(k, cos_ref[...], sin_ref[...]).astype(_BF)
        v_out[...] = t["kv"][:, KV_DIM:].astype(_BF)

    third_stage = (embed_gate_matmul, embed_and_unit_norm, q_matmul, q_norm_rope, kv_matmul, k_norm_rope)
    assert len(third_stage) <= n_chunks

    uf = uf_cur[...]
    for c in range(n_chunks):
        c0 = c * FF_CHUNK
        a = _dot(uf, wgu[:, c0:c0 + FF_CHUNK])
        g = _dot(uf, wgu[:, D_FF + c0:D_FF + c0 + FF_CHUNK])
        act_scr[:, c0:c0 + FF_CHUNK] = (a * _sigmoid(a) * g).astype(_BF)
        if c < len(third_stage):
            third_stage[c]()
        if c * POOL_ROWS < ts:
            first_stage_rows(c * POOL_ROWS, POOL_ROWS)
    acc = _dot(act_scr[...], wd[...])

    h2_buf[...] = h1_cur[...] + acc
    ubuf[0:halo, :] = ubuf[ts:end, :]
    h1_cur[...] = h1_next[...]
    uf_cur[...] = uf_next[...]


def _attn_kernel(sinks_ref, q_ref, kp_ref, kc_ref, vp_ref, vc_ref, h_ref, wo_ref, out_ref,
                 klo, khi, vlo, vhi, o_scr):
    i = pl.program_id(1)
    tq = q_ref.shape[0]
    nblk = tq // WINDOW
    lane = lax.broadcasted_iota(jnp.int32, (1, LANES), 1)
    low = lane < HEAD_DIM

    qi = lax.broadcasted_iota(jnp.int32, (2 * WINDOW, 2 * WINDOW), 0) % WINDOW
    kj = lax.broadcasted_iota(jnp.int32, (2 * WINDOW, 2 * WINDOW), 1)
    in_window = (kj > qi) & (kj <= qi + WINDOW)
    upper_rows = lax.broadcasted_iota(jnp.int32, (2 * WINDOW, 1), 0) >= WINDOW

    for g in range(N_KV_HEADS):
        slab = slice((g // 2) * LANES, (g // 2 + 1) * LANES)
        for src_p, src_c, lo_ref, hi_ref in ((kp_ref, kc_ref, klo, khi), (vp_ref, vc_ref, vlo, vhi)):
            t = jnp.concatenate([src_p[:, slab], src_c[:, slab]], axis=0).astype(_F32)
            r = pltpu.roll(t, HEAD_DIM, 1)
            in_low, in_high = (t, r) if g % 2 == 0 else (r, t)
            lo_ref[...] = jnp.where(low, in_low, 0.0).astype(_BF)
            hi_ref[...] = jnp.where(low, 0.0, in_high).astype(_BF)

        for j in range(nblk):
            qrows = pl.ds(j * WINDOW, WINDOW)
            krows = pl.ds(j * WINDOW, 2 * WINDOW)
            qg = jnp.concatenate([q_ref[qrows, (2 * g) * LANES:(2 * g + 1) * LANES],
                                  q_ref[qrows, (2 * g + 1) * LANES:(2 * g + 2) * LANES]], axis=0)
            mask = in_window & (kj >= jnp.where(i > 0, 0, WINDOW)) if j == 0 else in_window
            o = jnp.zeros((2 * WINDOW, LANES), _F32)
            for half, (k_ref, v_ref) in enumerate(((klo, vlo), (khi, vhi))):
                sink = jnp.where(upper_rows, sinks_ref[4 * g + 2 + half], sinks_ref[4 * g + half])
                s = lax.dot_general(qg, k_ref[krows, :], (((1,), (1,)), ((), ())),
                                    preferred_element_type=_F32)
                s = jnp.where(mask, s, -jnp.inf)
                m = jnp.maximum(jnp.max(s, axis=1, keepdims=True), sink)
                pr = jnp.exp(s - m)
                denom = jnp.sum(pr, axis=1, keepdims=True) + jnp.exp(sink - m)
                o = o + _dot(pr.astype(_BF), v_ref[krows, :]) / denom
            o_scr[qrows, (2 * g) * LANES:(2 * g + 1) * LANES] = o[:WINDOW].astype(_BF)
            o_scr[qrows, (2 * g + 1) * LANES:(2 * g + 2) * LANES] = o[WINDOW:].astype(_BF)

    out_ref[...] = h_ref[...] + _dot(o_scr[...], wo_ref[...])


def _segment_copies(rows, src_ref, src_base, dst_ref, dst_base, sem, sizes=SEG_SIZES):
    out = []
    for sz in sizes:
        done = rows & (-2 * sz)
        src = src_ref.at[pl.ds(pl.multiple_of(src_base + done, SEG_ALIGN), sz)]
        dst = dst_ref.at[pl.ds(pl.multiple_of(dst_base + done, SEG_ALIGN), sz)]
        out.append(((rows & sz) != 0, pltpu.make_async_copy(src, dst, sem)))
    return out


def _run(copies, op):
    for cond, cp in copies:
        @pl.when(cond)
        def _(cp=cp):
            getattr(cp, op)()


def _block_copies(seg_ref, blk, hbm_ref, buf_ref, sem, to_hbm):
    cap = hbm_ref.shape[0] // N_EXPERTS
    copies = []
    off = jnp.int32(0)
    for e in range(N_EXPERTS):
        base = e * cap + seg_ref[blk * 2 * N_EXPERTS + e]
        rows = seg_ref[blk * 2 * N_EXPERTS + N_EXPERTS + e]
        if to_hbm:
            copies += _segment_copies(rows, buf_ref, off, hbm_ref, base, sem)
        else:
            copies += _segment_copies(rows, hbm_ref, base, buf_ref, off, sem)
        off = off + rows
    return copies


def _route_kernel(h_ref, ffn_norm, rw_hi, rw_lo, rb, xs_hbm, meta_ref, seg_ref, tot_ref,
                  comp_scr, zero_scr, cum_ref, sem):
    k = pl.program_id(0)
    bs = h_ref.shape[0]
    cap = xs_hbm.shape[0] // N_EXPERTS
    lane = lax.broadcasted_iota(jnp.int32, (1, ROUTER_PAD), 1)

    @pl.when(k == 0)
    def _():
        for e in range(N_EXPERTS):
            cum_ref[e] = 0

    u = _rms(h_ref[...], ffn_norm[...])
    u_hi = u.astype(_BF)
    u_lo = (u - u_hi.astype(_F32)).astype(_BF)
    logits = _dot(u_hi, rw_hi[...]) + _dot(u_lo, rw_hi[...]) + _dot(u_hi, rw_lo[...]) + rb[...]
    logits = jnp.where(lane < N_EXPERTS, logits, -jnp.inf)
    m1 = jnp.max(logits, axis=1, keepdims=True)
    i1 = jnp.min(jnp.where(logits == m1, lane, ROUTER_PAD), axis=1, keepdims=True)
    rest = jnp.where(lane == i1, -jnp.inf, logits)
    m2 = jnp.max(rest, axis=1, keepdims=True)
    i2 = jnp.min(jnp.where(rest == m2, lane, ROUTER_PAD), axis=1, keepdims=True)
    t = jnp.exp(m2 - m1)
    w1 = 1.0 / (1.0 + t)
    w2 = t * w1

    a1 = lane == i1
    a2 = lane == i2
    assigned = jnp.where(a1 | a2, 1.0, 0.0)
    before = (lax.broadcasted_iota(jnp.int32, (bs, bs), 1) < lax.broadcasted_iota(jnp.int32, (bs, bs), 0))
    rank = _dot(jnp.where(before, 1.0, 0.0).astype(_BF), assigned.astype(_BF))
    count = jnp.sum(assigned, axis=0, keepdims=True).astype(jnp.int32)
    seg_off = jnp.zeros((1, ROUTER_PAD), _F32)
    off = jnp.int32(0)
    for e in range(N_EXPERTS):
        rows = ((count[0, e] + (SEG_ALIGN - 1)) // SEG_ALIGN) * SEG_ALIGN
        seg_off = jnp.where(lane == e, off.astype(_F32), seg_off)
        base = cum_ref[e]
        seg_ref[k * 2 * N_EXPERTS + e] = base
        seg_ref[k * 2 * N_EXPERTS + N_EXPERTS + e] = rows
        cum_ref[e] = base + rows
        off = off + rows
    dst1 = jnp.sum(jnp.where(a1, rank + seg_off, 0.0), axis=1, keepdims=True)
    dst2 = jnp.sum(jnp.where(a2, rank + seg_off, 0.0), axis=1, keepdims=True)
    meta = jnp.where(lane == 0, dst1, jnp.where(lane == 1, dst2, jnp.where(lane == 2, w1, jnp.where(lane == 3, w2, 0.0))))
    meta_ref[...] = meta

    by_token = meta.T
    row = lax.broadcasted_iota(jnp.int32, (SEG_ROWS, 1), 0).astype(_F32)
    from1 = row == by_token[0:1, :]
    from2 = row == by_token[1:2, :]
    onehot = jnp.where(from1 | from2, 1.0, 0.0).astype(_BF)
    gate = jnp.sum(jnp.where(from1, by_token[2:3, :], 0.0) + jnp.where(from2, by_token[3:4, :], 0.0),
                   axis=1, keepdims=True)
    hi = gate.astype(_BF).astype(_F32)
    mid = (gate - hi).astype(_BF).astype(_F32)
    gate_parts = jnp.where(lane == 0, hi, jnp.where(lane == 1, mid, jnp.where(lane == 2, gate - hi - mid, 0.0)))

    slot = k % 2
    comp = comp_scr.at[slot]
    comp[:, :D_MODEL] = _dot(onehot, u_hi).astype(_BF)
    comp[:, D_MODEL:] = gate_parts.astype(_BF)
    _run(_block_copies(seg_ref, k, xs_hbm, comp, sem.at[slot], to_hbm=True), "start")

    @pl.when(k > 0)
    def _():
        _run(_block_copies(seg_ref, k - 1, xs_hbm, comp_scr.at[1 - slot], sem.at[1 - slot], to_hbm=True), "wait")

    @pl.when(k == pl.num_programs(0) - 1)
    def _():
        _run(_block_copies(seg_ref, k, xs_hbm, comp, sem.at[slot], to_hbm=True), "wait")
        zero_scr[...] = jnp.zeros_like(zero_scr)
        fills = []
        for e in range(N_EXPERTS):
            total = cum_ref[e]
            tot_ref[e] = total
            fills += _segment_copies((-total) & (EXPERT_TILE - 1), zero_scr, 0, xs_hbm, e * cap + total,
                                     sem.at[slot], sizes=FILL_SIZES)
        _run(fills, "start")
        _run(fills, "wait")


def _expert_kernel(te_ref, tb_ref, nt_ref, x_ref, wa_f32, wg_f32, wd_f32, y_ref, wa, wg, wd):
    i = pl.program_id(0)

    @pl.when((i == 0) | (te_ref[i] != te_ref[jnp.maximum(i - 1, 0)]))
    def _():
        wa[...] = wa_f32[...].astype(_BF)
        wg[...] = wg_f32[...].astype(_BF)
        wd[...] = wd_f32[...].astype(_BF)

    @pl.when(i < nt_ref[0])
    def _():
        x = x_ref[:, :D_MODEL]
        gate = jnp.sum(x_ref[:, D_MODEL:].astype(_F32), axis=1, keepdims=True)
        a = _dot(x, wa[...])
        g = _dot(x, wg[...])
        act = (a * _sigmoid(a) * g).astype(_BF)
        y_ref[...] = (gate * _dot(act, wd[...])).astype(_BF)


def _combine_kernel(seg_ref, h_ref, p_ref, meta_ref, ys_hbm, gate_norm, gate_w, ple_w, out_ref, ycat, sem):
    k = pl.program_id(0)
    slot = k % 2
    fetch = lambda blk, s, op: _run(_block_copies(seg_ref, blk, ys_hbm, ycat.at[s], sem.at[s], to_hbm=False), op)

    @pl.when(k == 0)
    def _():
        ycat[...] = jnp.zeros_like(ycat)
        fetch(k, slot, "start")

    @pl.when(k + 1 < pl.num_programs(0))
    def _():
        fetch(k + 1, 1 - slot, "start")

    fetch(k, slot, "wait")

    pos = lax.broadcasted_iota(jnp.int32, (1, SEG_ROWS), 1).astype(_F32)
    meta = meta_ref[...]
    onehot = jnp.where((pos == meta[:, 0:1]) | (pos == meta[:, 1:2]), 1.0, 0.0).astype(_BF)
    h = h_ref[...] + _dot(onehot, ycat[slot])
    gate = _sigmoid(_dot(_rms(h, gate_norm[...]).astype(_BF), gate_w[...]))
    out_ref[...] = h + gate * _dot(p_ref[...].astype(_BF), ple_w[...])


def _const(shape):
    return pl.BlockSpec(shape, lambda *_: (0,) * len(shape), pipeline_mode=pl.Buffered(1))


def _row(v):
    return v.reshape(1, -1).astype(_F32)


def kernel(x, p, pool_norm, pool_w, pool_scale, kv_norm, w_kv, k_norm, attn_norm, w_q, q_norm, sinks, w_o,
           ffn_norm, w_gu, w_down, router_w, router_b, we_gu, we_down, ple_gate_norm, ple_gate_w, ple_w):
    b, s, d = x.shape
    ts, tq = SEQ_TILE, ATTN_TILE

    inv = ROPE_THETA ** (-jnp.arange(0, HEAD_DIM, 2, dtype=_F32) / HEAD_DIM)
    ang = jnp.arange(s, dtype=_F32)[:, None] * inv[None, :]
    cos_t = jnp.tile(jnp.cos(ang), (1, LANES // (HEAD_DIM // 2)))
    sin_t = jnp.tile(jnp.concatenate([-jnp.sin(ang), jnp.sin(ang)], axis=1), (1, LANES // HEAD_DIM))
    hid = jnp.arange(KV_DIM) // HEAD_DIM
    hmat = ((hid[:, None] == hid[None, :]) * (1.0 / HEAD_DIM)).astype(_BF)

    q_gain = jnp.tile(q_norm[0].astype(_F32) * (HEAD_DIM ** -0.5), N_HEADS).reshape(1, d)
    k_gain = jnp.tile(k_norm.astype(_F32), N_KV_HEADS).reshape(1, KV_DIM)

    tps = s // ts
    n_tiles0 = b * tps
    first = lambda st: jnp.minimum(st, n_tiles0 - 1)
    third = lambda st: jnp.maximum(st - 2, 0)
    tile3 = lambda w: pl.BlockSpec((None, ts, w), lambda st: (third(st) // tps, third(st) % tps, 0))
    h1, q, k, v = pl.pallas_call(
        functools.partial(_layer0_kernel, tiles_per_seq=tps),
        grid=(n_tiles0 + 2,),
        in_specs=[
            pl.BlockSpec((None, ts, d), lambda st: (first(st) // tps, first(st) % tps, 0)),
            pl.BlockSpec((None, None, ts, PLE_DIM), lambda st: (0, third(st) // tps, third(st) % tps, 0)),
            pl.BlockSpec((ts, LANES), lambda st: (third(st) % tps, 0)),
            pl.BlockSpec((ts, LANES), lambda st: (third(st) % tps, 0)),
            _const((1, d)), _const((len(POOL_WINDOWS), POOL_GROUP, POOL_GROUP)), _const((1, d)),
            _const((1, d)), _const((d, 2 * D_FF)), _const((D_FF, d)),
            _const((1, d)), _const((d, d)), _const((PLE_DIM, d)),
            _const((1, d)), _const((d, d)), _const((1, d)),
            _const((1, d)), _const((d, 2 * KV_DIM)), _const((1, KV_DIM)), _const((KV_DIM, KV_DIM)),
        ],
        out_specs=[tile3(d), tile3(d), tile3(KV_DIM), tile3(KV_DIM)],
        out_shape=[jax.ShapeDtypeStruct((b, s, d), _F32), jax.ShapeDtypeStruct((b, s, d), _BF),
                   jax.ShapeDtypeStruct((b, s, KV_DIM), _BF), jax.ShapeDtypeStruct((b, s, KV_DIM), _BF)],
        scratch_shapes=[pltpu.VMEM((POOL_HALO + ts, d), _F32)] + [pltpu.VMEM((POOL_ROWS + 24, d), _F32)] * 2
                       + [pltpu.VMEM((ts, d), _F32), pltpu.VMEM((ts, d), _BF)] * 2 + [pltpu.VMEM((ts, d), _F32)]
                       + [pltpu.VMEM((ts, D_FF), _BF)],
        compiler_params=pltpu.CompilerParams(dimension_semantics=("arbitrary",), vmem_limit_bytes=VMEM_LIMIT),
        name="layer0",
    )(x, p, cos_t, sin_t, _row(pool_norm[0]), pool_w[0].astype(_BF), _row(pool_scale[0]),
      _row(ffn_norm[0]), w_gu[0].astype(_BF), w_down[0].astype(_BF), _row(ple_gate_norm[0]), ple_gate_w[0].astype(_BF), ple_w[0].astype(_BF),
      _row(attn_norm[0]), w_q[0].astype(_BF), q_gain, _row(kv_norm), w_kv.astype(_BF), k_gain, hmat)

    blocks_per_tile = tq // WINDOW
    prev = lambda bi, qi, *_: (bi, jnp.maximum(qi * blocks_per_tile - 1, 0), 0)
    cur = lambda bi, qi, *_: (bi, qi, 0)
    h2 = pl.pallas_call(
        _attn_kernel,
        grid_spec=pltpu.PrefetchScalarGridSpec(
            num_scalar_prefetch=1,
            grid=(b, s // tq),
            in_specs=[
                pl.BlockSpec((None, tq, d), cur),
                pl.BlockSpec((None, WINDOW, KV_DIM), prev), pl.BlockSpec((None, tq, KV_DIM), cur),
                pl.BlockSpec((None, WINDOW, KV_DIM), prev), pl.BlockSpec((None, tq, KV_DIM), cur),
                pl.BlockSpec((None, tq, d), cur),
                pl.BlockSpec((d, d), lambda *_: (0, 0), pipeline_mode=pl.Buffered(1)),
            ],
            out_specs=pl.BlockSpec((None, tq, d), cur),
            scratch_shapes=[pltpu.VMEM((WINDOW + tq, LANES), _BF)] * 4 + [pltpu.VMEM((tq, d), _BF)],
        ),
        out_shape=jax.ShapeDtypeStruct((b, s, d), _F32),
        compiler_params=pltpu.CompilerParams(dimension_semantics=("arbitrary", "arbitrary"),
                                             vmem_limit_bytes=VMEM_LIMIT),
        name="attn",
    )(sinks[0].astype(_F32), q, k, k, v, v, h1, w_o[0].astype(_BF))

    t = b * s
    rw = jnp.pad(router_w[0].astype(_F32), ((0, 0), (0, ROUTER_PAD - N_EXPERTS)))
    rw_hi = rw.astype(_BF)
    rw_lo = (rw - rw_hi.astype(_F32)).astype(_BF)
    rb = jnp.pad(router_b[0].astype(_F32), (0, ROUTER_PAD - N_EXPERTS)).reshape(1, ROUTER_PAD)

    bs, te = ROUTE_TILE, EXPERT_TILE
    nblk = t // bs
    cap = t
    smem = pl.BlockSpec(memory_space=pltpu.SMEM)
    tok = lambda w: pl.BlockSpec((bs, w), lambda ki, *_: (ki, 0))
    xs, meta, seg, totals = pl.pallas_call(
        _route_kernel,
        grid=(nblk,),
        in_specs=[tok(d), _const((1, d)), _const((d, ROUTER_PAD)), _const((d, ROUTER_PAD)), _const((1, ROUTER_PAD))],
        out_specs=[pl.BlockSpec(memory_space=pl.ANY), tok(ROUTER_PAD), smem, smem],
        out_shape=[jax.ShapeDtypeStruct((N_EXPERTS * cap, SORTED_WIDTH), _BF),
                   jax.ShapeDtypeStruct((t, ROUTER_PAD), _F32),
                   jax.ShapeDtypeStruct((nblk * 2 * N_EXPERTS,), jnp.int32),
                   jax.ShapeDtypeStruct((N_EXPERTS,), jnp.int32)],
        scratch_shapes=[pltpu.VMEM((2, SEG_ROWS, SORTED_WIDTH), _BF), pltpu.VMEM((EXPERT_TILE, SORTED_WIDTH), _BF),
                        pltpu.SMEM((N_EXPERTS,), jnp.int32), pltpu.SemaphoreType.DMA((2,))],
        compiler_params=pltpu.CompilerParams(dimension_semantics=("arbitrary",), vmem_limit_bytes=VMEM_LIMIT),
        name="route",
    )(h2.reshape(t, d), _row(ffn_norm[1]), rw_hi, rw_lo, rb)

    n_steps = (2 * t + nblk * N_EXPERTS * (SEG_ALIGN - 1)) // te + N_EXPERTS
    tiles_e = (totals + te - 1) // te
    tile_end = jnp.cumsum(tiles_e)
    n_tiles = tile_end[-1]
    step = jnp.minimum(jnp.arange(n_steps, dtype=jnp.int32), n_tiles - 1)
    tile_expert = jnp.sum(step[:, None] >= tile_end[None, :], axis=1).astype(jnp.int32)
    tile_block = (tile_expert * (cap // te) + step - (tile_end - tiles_e)[tile_expert]).astype(jnp.int32)

    xrow = lambda w: pl.BlockSpec((te, w), lambda i, te_r, tb_r, nt_r: (tb_r[i], 0))
    wexp = lambda r, c, cb: pl.BlockSpec((None, None, r, c), lambda i, te_r, tb_r, nt_r: (0, te_r[i], 0, cb))
    ys = pl.pallas_call(
        _expert_kernel,
        grid_spec=pltpu.PrefetchScalarGridSpec(
            num_scalar_prefetch=3,
            grid=(n_steps,),
            in_specs=[xrow(SORTED_WIDTH), wexp(d, D_FF_EXPERT, 0), wexp(d, D_FF_EXPERT, 1), wexp(D_FF_EXPERT, d, 0)],
            out_specs=xrow(d),
            scratch_shapes=[pltpu.VMEM((d, D_FF_EXPERT), _BF)] * 2 + [pltpu.VMEM((D_FF_EXPERT, d), _BF)],
        ),
        out_shape=jax.ShapeDtypeStruct((N_EXPERTS * cap, d), _BF),
        compiler_params=pltpu.CompilerParams(dimension_semantics=("arbitrary",), vmem_limit_bytes=VMEM_LIMIT),
        name="experts",
    )(tile_expert, tile_block, n_tiles.reshape(1).astype(jnp.int32), xs, we_gu, we_gu, we_down)

    out = pl.pallas_call(
        _combine_kernel,
        grid_spec=pltpu.PrefetchScalarGridSpec(
            num_scalar_prefetch=1,
            grid=(nblk,),
            in_specs=[tok(d),
                      pl.BlockSpec((None, None, bs, PLE_DIM), lambda ki, *_: (1, ki // (s // bs), ki % (s // bs), 0)),
                      tok(ROUTER_PAD), pl.BlockSpec(memory_space=pl.ANY),
                      _const((1, d)), _const((d, d)), _const((PLE_DIM, d))],
            out_specs=tok(d),
            scratch_shapes=[pltpu.VMEM((2, SEG_ROWS, d), _BF), pltpu.SemaphoreType.DMA((2,))],
        ),
        out_shape=jax.ShapeDtypeStruct((t, d), _F32),
        compiler_params=pltpu.CompilerParams(dimension_semantics=("arbitrary",), vmem_limit_bytes=VMEM_LIMIT),
        name="combine",
    )(seg, h2.reshape(t, d), p, meta, ys, _row(ple_gate_norm[1]),
      ple_gate_w[1].astype(_BF), ple_w[1].astype(_BF))
    return out.reshape(b, s, d)
```

```python
import functools

import jax
import jax.numpy as jnp
from jax import lax
from jax.experimental import pallas as pl
from jax.experimental.pallas import tpu as pltpu

D_MODEL = 1024
PLE_DIM = 256
POOL_WINDOWS = (2, 4, 8, 16)
POOL_GROUP = D_MODEL // len(POOL_WINDOWS)
POOL_HALO = 32
POOL_ROWS = 256
HEAD_DIM = 64
N_HEADS = 16
N_KV_HEADS = 4
KV_DIM = N_KV_HEADS * HEAD_DIM
WINDOW = 128
ROPE_THETA = 10000.0
D_FF = 2816
FF_CHUNK = 256
N_EXPERTS = 8
D_FF_EXPERT = 1024
EPS = 1e-6
LANES = 128
ROUTER_PAD = LANES

SEQ_TILE = 512
ATTN_TILE = 1024
ROUTE_TILE = 512
EXPERT_TILE = 1024
SEG_ALIGN = 16
SEG_SIZES = tuple(ROUTE_TILE >> i for i in range((ROUTE_TILE // SEG_ALIGN).bit_length()))
FILL_SIZES = tuple((EXPERT_TILE // 2) >> i for i in range((EXPERT_TILE // 2 // SEG_ALIGN).bit_length()))
SEG_ROWS = 2 * ROUTE_TILE + N_EXPERTS * SEG_ALIGN
GATE_LANES = LANES
SORTED_WIDTH = D_MODEL + GATE_LANES
VMEM_LIMIT = 60 * 1024 * 1024

_BF = jnp.bfloat16
_F32 = jnp.float32


def _rms(x, g):
    return x * lax.rsqrt(jnp.mean(x * x, axis=-1, keepdims=True) + EPS) * g


def _sigmoid(x):
    return 0.5 * jnp.tanh(0.5 * x) + 0.5


def _dot(a, b):
    return jnp.dot(a, b, preferred_element_type=_F32)


def _head_meansq(x, hmat):
    x2 = (x * x).astype(_BF)
    width = hmat.shape[0]
    return jnp.concatenate([_dot(x2[:, c:c + width], hmat) for c in range(0, x.shape[1], width)], axis=1)


def _rope(x, cos, sin_signed):
    n = x.shape[1]
    reps = n // LANES
    lane = lax.broadcasted_iota(jnp.int32, (1, n), 1)
    first_half = (lane % HEAD_DIM) < (HEAD_DIM // 2)
    partner = jnp.where(first_half, pltpu.roll(x, n - HEAD_DIM // 2, 1), pltpu.roll(x, HEAD_DIM // 2, 1))
    c = jnp.concatenate([cos] * reps, axis=1)
    s = jnp.concatenate([sin_signed] * reps, axis=1)
    return x * c + partner * s


def _layer0_kernel(x_ref, p_ref, cos_ref, sin_ref, pool_norm, pool_w, pool_scale, ffn_norm, wgu, wd,
                   gate_norm, gate_w, ple_w, attn_norm, wq, q_gain, kv_norm, wkv, k_gain, hmat,
                   h_out, q_out, k_out, v_out, ubuf, sum_a, sum_b, h1_cur, uf_cur, h1_next, uf_next, h2_buf,
                   act_scr, *, tiles_per_seq):
    step = pl.program_id(0)
    si = jnp.minimum(step, pl.num_programs(0) - 3) % tiles_per_seq
    ts = x_ref.shape[0]
    halo, end = POOL_HALO, POOL_HALO + ts
    n_chunks = D_FF // FF_CHUNK
    assert ts % POOL_ROWS == 0 and ts // POOL_ROWS <= n_chunks

    @pl.when(step == 0)
    def _():
        h1_cur[...] = jnp.zeros_like(h1_cur)
        uf_cur[...] = jnp.zeros_like(uf_cur)
        h2_buf[...] = jnp.zeros_like(h2_buf)

    @pl.when(si == 0)
    def _():
        ubuf[0:halo, :] = jnp.zeros((halo, D_MODEL), _F32)

    def first_stage_rows(r0, n):
        a, b = halo + r0, halo + r0 + n
        xr = x_ref[r0:r0 + n, :]
        u = _rms(xr, pool_norm[...])
        ubuf[a:b, :] = u
        g1, g2, g3 = POOL_GROUP, 2 * POOL_GROUP, 3 * POOL_GROUP
        m = n + 24
        s2 = ubuf[a - 24:b, :] + ubuf[a - 25:b - 1, :]
        sum_a[0:m, g1:] = s2[:, g1:]
        s4 = sum_a[8:m, g1:] + sum_a[6:m - 2, g1:]
        sum_b[8:m, g2:] = s4[:, g1:]
        s8 = sum_b[16:m, g2:] + sum_b[12:m - 4, g2:]
        sum_a[16:m, g3:] = s8[:, g1:]
        s16 = sum_a[24:m, g3:] + sum_a[16:m - 8, g3:]
        window_sums = (s2[24:, :g1], s4[16:, :g1], s8[8:, :g1], s16)
        pos = (si * ts + r0 + 1 + lax.broadcasted_iota(jnp.int32, (n, 1), 0)).astype(_F32)
        mixed = []
        for g, win in enumerate(POOL_WINDOWS):
            ug = u[:, g * POOL_GROUP:(g + 1) * POOL_GROUP]
            d = window_sums[g] / jnp.minimum(pos, float(win)) - ug
            mixed.append(_dot(d.astype(_BF), pool_w[g]))
        h_mixed = xr + jnp.concatenate(mixed, axis=1) * pool_scale[...]
        h1_next[r0:r0 + n, :] = h_mixed
        uf_next[r0:r0 + n, :] = _rms(h_mixed, ffn_norm[...]).astype(_BF)

    t = {}

    def embed_gate_matmul():
        t["h"] = h2_buf[...]
        t["gate_pre"] = _dot(_rms(t["h"], gate_norm[...]).astype(_BF), gate_w[...])

    def embed_and_unit_norm():
        h = t["h"] + _sigmoid(t["gate_pre"]) * _dot(p_ref[...].astype(_BF), ple_w[...])
        h_out[...] = h
        t["unit"] = h * lax.rsqrt(jnp.mean(h * h, axis=-1, keepdims=True) + EPS)

    def q_matmul():
        t["q"] = _dot((t["unit"] * attn_norm[...]).astype(_BF), wq[...])

    def q_norm_rope():
        q = t["q"] * lax.rsqrt(_head_meansq(t["q"], hmat[...]) + EPS) * q_gain[...]
        q_out[...] = _rope(q, cos_ref[...], sin_ref[...]).astype(_BF)

    def kv_matmul():
        t["kv"] = _dot((t["unit"] * kv_norm[...]).astype(_BF), wkv[...])

    def k_norm_rope():
        k = t["kv"][:, :KV_DIM]
        k = k * lax.rsqrt(_head_meansq(k, hmat[...]) + EPS) * k_gain[...]
        k_out[...] = _rope(k, cos_ref[...], sin_ref[...]).astype(_BF)
        v_out[...] = t["kv"][:, KV_DIM:].astype(_BF)

    third_stage = (embed_gate_matmul, embed_and_unit_norm, q_matmul, q_norm_rope, kv_matmul, k_norm_rope)
    assert len(third_stage) <= n_chunks

    uf = uf_cur[...]
    for c in range(n_chunks):
        c0 = c * FF_CHUNK
        a = _dot(uf, wgu[:, c0:c0 + FF_CHUNK])
        g = _dot(uf, wgu[:, D_FF + c0:D_FF + c0 + FF_CHUNK])
        act_scr[:, c0:c0 + FF_CHUNK] = (a * _sigmoid(a) * g).astype(_BF)
        if c < len(third_stage):
            third_stage[c]()
        if c * POOL_ROWS < ts:
            first_stage_rows(c * POOL_ROWS, POOL_ROWS)
    acc = _dot(act_scr[...], wd[...])

    h2_buf[...] = h1_cur[...] + acc
    ubuf[0:halo, :] = ubuf[ts:end, :]
    h1_cur[...] = h1_next[...]
    uf_cur[...] = uf_next[...]


def _attn_kernel(sinks_ref, q_ref, kp_ref, kc_ref, vp_ref, vc_ref, h_ref, wo_ref, out_ref,
                 klo, khi, vlo, vhi, o_scr):
    i = pl.program_id(1)
    tq = q_ref.shape[0]
    nblk = tq // WINDOW
    lane = lax.broadcasted_iota(jnp.int32, (1, LANES), 1)
    low = lane < HEAD_DIM

    qi = lax.broadcasted_iota(jnp.int32, (2 * WINDOW, 2 * WINDOW), 0) % WINDOW
    kj = lax.broadcasted_iota(jnp.int32, (2 * WINDOW, 2 * WINDOW), 1)
    in_window = (kj > qi) & (kj <= qi + WINDOW)
    upper_rows = lax.broadcasted_iota(jnp.int32, (2 * WINDOW, 1), 0) >= WINDOW

    for g in range(N_KV_HEADS):
        slab = slice((g // 2) * LANES, (g // 2 + 1) * LANES)
        for src_p, src_c, lo_ref, hi_ref in ((kp_ref, kc_ref, klo, khi), (vp_ref, vc_ref, vlo, vhi)):
            t = jnp.concatenate([src_p[:, slab], src_c[:, slab]], axis=0).astype(_F32)
            r = pltpu.roll(t, HEAD_DIM, 1)
            in_low, in_high = (t, r) if g % 2 == 0 else (r, t)
            lo_ref[...] = jnp.where(low, in_low, 0.0).astype(_BF)
            hi_ref[...] = jnp.where(low, 0.0, in_high).astype(_BF)

        for j in range(nblk):
            qrows = pl.ds(j * WINDOW, WINDOW)
            krows = pl.ds(j * WINDOW, 2 * WINDOW)
            qg = jnp.concatenate([q_ref[qrows, (2 * g) * LANES:(2 * g + 1) * LANES],
                                  q_ref[qrows, (2 * g + 1) * LANES:(2 * g + 2) * LANES]], axis=0)
            mask = in_window & (kj >= jnp.where(i > 0, 0, WINDOW)) if j == 0 else in_window
            o = jnp.zeros((2 * WINDOW, LANES), _F32)
            for half, (k_ref, v_ref) in enumerate(((klo, vlo), (khi, vhi))):
                sink = jnp.where(upper_rows, sinks_ref[4 * g + 2 + half], sinks_ref[4 * g + half])
                s = lax.dot_general(qg, k_ref[krows, :], (((1,), (1,)), ((), ())),
                                    preferred_element_type=_F32)
                s = jnp.where(mask, s, -jnp.inf)
                m = jnp.maximum(jnp.max(s, axis=1, keepdims=True), sink)
                pr = jnp.exp(s - m)
                denom = jnp.sum(pr, axis=1, keepdims=True) + jnp.exp(sink - m)
                o = o + _dot(pr.astype(_BF), v_ref[krows, :]) / denom
            o_scr[qrows, (2 * g) * LANES:(2 * g + 1) * LANES] = o[:WINDOW].astype(_BF)
            o_scr[qrows, (2 * g + 1) * LANES:(2 * g + 2) * LANES] = o[WINDOW:].astype(_BF)

    out_ref[...] = h_ref[...] + _dot(o_scr[...], wo_ref[...])


def _segment_copies(rows, src_ref, src_base, dst_ref, dst_base, sem, sizes=SEG_SIZES):
    out = []
    for sz in sizes:
        done = rows & (-2 * sz)
        src = src_ref.at[pl.ds(pl.multiple_of(src_base + done, SEG_ALIGN), sz)]
        dst = dst_ref.at[pl.ds(pl.multiple_of(dst_base + done, SEG_ALIGN), sz)]
        out.append(((rows & sz) != 0, pltpu.make_async_copy(src, dst, sem)))
    return out


def _run(copies, op):
    for cond, cp in copies:
        @pl.when(cond)
        def _(cp=cp):
            getattr(cp, op)()


def _block_copies(seg_ref, blk, hbm_ref, buf_ref, sem, to_hbm):
    cap = hbm_ref.shape[0] // N_EXPERTS
    copies = []
    off = jnp.int32(0)
    for e in range(N_EXPERTS):
        base = e * cap + seg_ref[blk * 2 * N_EXPERTS + e]
        rows = seg_ref[blk * 2 * N_EXPERTS + N_EXPERTS + e]
        if to_hbm:
            copies += _segment_copies(rows, buf_ref, off, hbm_ref, base, sem)
        else:
            copies += _segment_copies(rows, hbm_ref, base, buf_ref, off, sem)
        off = off + rows
    return copies


def _route_kernel(h_ref, ffn_norm, rw_both, rb, xs_hbm, meta_ref, seg_ref, tot_ref,
                  comp_scr, zero_scr, cum_ref, sem):
    k = pl.program_id(0)
    bs = h_ref.shape[0]
    cap = xs_hbm.shape[0] // N_EXPERTS
    lane = lax.broadcasted_iota(jnp.int32, (1, ROUTER_PAD), 1)

    @pl.when(k == 0)
    def _():
        for e in range(N_EXPERTS):
            cum_ref[e] = 0

    u = _rms(h_ref[...], ffn_norm[...])
    u_hi = u.astype(_BF)
    u_lo = (u - u_hi.astype(_F32)).astype(_BF)
    both = _dot(u_hi, rw_both[...])
    logits = both[:, :ROUTER_PAD] + _dot(u_lo, rw_both[:, :ROUTER_PAD]) + both[:, ROUTER_PAD:] + rb[...]
    logits = jnp.where(lane < N_EXPERTS, logits, -jnp.inf)
    m1 = jnp.max(logits, axis=1, keepdims=True)
    i1 = jnp.min(jnp.where(logits == m1, lane, ROUTER_PAD), axis=1, keepdims=True)
    rest = jnp.where(lane == i1, -jnp.inf, logits)
    m2 = jnp.max(rest, axis=1, keepdims=True)
    i2 = jnp.min(jnp.where(rest == m2, lane, ROUTER_PAD), axis=1, keepdims=True)
    t = jnp.exp(m2 - m1)
    w1 = 1.0 / (1.0 + t)
    w2 = t * w1

    a1 = lane == i1
    a2 = lane == i2
    assigned = jnp.where(a1 | a2, 1.0, 0.0)
    before = (lax.broadcasted_iota(jnp.int32, (bs, bs), 1) < lax.broadcasted_iota(jnp.int32, (bs, bs), 0))
    rank = _dot(jnp.where(before, 1.0, 0.0).astype(_BF), assigned.astype(_BF))
    count = jnp.sum(assigned, axis=0, keepdims=True).astype(jnp.int32)
    seg_off = jnp.zeros((1, ROUTER_PAD), _F32)
    off = jnp.int32(0)
    for e in range(N_EXPERTS):
        rows = ((count[0, e] + (SEG_ALIGN - 1)) // SEG_ALIGN) * SEG_ALIGN
        seg_off = jnp.where(lane == e, off.astype(_F32), seg_off)
        base = cum_ref[e]
        seg_ref[k * 2 * N_EXPERTS + e] = base
        seg_ref[k * 2 * N_EXPERTS + N_EXPERTS + e] = rows
        cum_ref[e] = base + rows
        off = off + rows
    dst1 = jnp.sum(jnp.where(a1, rank + seg_off, 0.0), axis=1, keepdims=True)
    dst2 = jnp.sum(jnp.where(a2, rank + seg_off, 0.0), axis=1, keepdims=True)
    meta = jnp.where(lane == 0, dst1, jnp.where(lane == 1, dst2, jnp.where(lane == 2, w1, jnp.where(lane == 3, w2, 0.0))))
    meta_ref[...] = meta

    by_token = meta.T
    row = lax.broadcasted_iota(jnp.int32, (SEG_ROWS, 1), 0).astype(_F32)
    from1 = row == by_token[0:1, :]
    from2 = row == by_token[1:2, :]
    onehot = jnp.where(from1 | from2, 1.0, 0.0).astype(_BF)
    gate = jnp.sum(jnp.where(from1, by_token[2:3, :], 0.0) + jnp.where(from2, by_token[3:4, :], 0.0),
                   axis=1, keepdims=True)
    hi = gate.astype(_BF).astype(_F32)
    mid = (gate - hi).astype(_BF).astype(_F32)
    gate_parts = jnp.where(lane == 0, hi, jnp.where(lane == 1, mid, jnp.where(lane == 2, gate - hi - mid, 0.0)))

    slot = k % 2
    comp = comp_scr.at[slot]
    comp[:, :D_MODEL] = _dot(onehot, u_hi).astype(_BF)
    comp[:, D_MODEL:] = gate_parts.astype(_BF)
    _run(_block_copies(seg_ref, k, xs_hbm, comp, sem.at[slot], to_hbm=True), "start")

    @pl.when(k > 0)
    def _():
        _run(_block_copies(seg_ref, k - 1, xs_hbm, comp_scr.at[1 - slot], sem.at[1 - slot], to_hbm=True), "wait")

    @pl.when(k == pl.num_programs(0) - 1)
    def _():
        _run(_block_copies(seg_ref, k, xs_hbm, comp, sem.at[slot], to_hbm=True), "wait")
        zero_scr[...] = jnp.zeros_like(zero_scr)
        fills = []
        for e in range(N_EXPERTS):
            total = cum_ref[e]
            tot_ref[e] = total
            fills += _segment_copies((-total) & (EXPERT_TILE - 1), zero_scr, 0, xs_hbm, e * cap + total,
                                     sem.at[slot], sizes=FILL_SIZES)
        _run(fills, "start")
        _run(fills, "wait")


def _expert_kernel(te_ref, tb_ref, nt_ref, x_ref, wa_f32, wg_f32, wd_f32, y_ref, wa, wg, wd):
    i = pl.program_id(0)

    @pl.when((i == 0) | (te_ref[i] != te_ref[jnp.maximum(i - 1, 0)]))
    def _():
        wa[...] = wa_f32[...].astype(_BF)
        wg[...] = wg_f32[...].astype(_BF)
        wd[...] = wd_f32[...].astype(_BF)

    @pl.when(i < nt_ref[0])
    def _():
        x = x_ref[:, :D_MODEL]
        gate = jnp.sum(x_ref[:, D_MODEL:].astype(_F32), axis=1, keepdims=True)
        a = _dot(x, wa[...])
        g = _dot(x, wg[...])
        act = (a * _sigmoid(a) * g).astype(_BF)
        y_ref[...] = (gate * _dot(act, wd[...])).astype(_BF)


def _combine_kernel(seg_ref, h_ref, p_ref, meta_ref, ys_hbm, gate_norm, gate_w, ple_w, out_ref, ycat, sem):
    k = pl.program_id(0)
    slot = k % 2
    fetch = lambda blk, s, op: _run(_block_copies(seg_ref, blk, ys_hbm, ycat.at[s], sem.at[s], to_hbm=False), op)

    @pl.when(k == 0)
    def _():
        ycat[...] = jnp.zeros_like(ycat)
        fetch(k, slot, "start")

    @pl.when(k + 1 < pl.num_programs(0))
    def _():
        fetch(k + 1, 1 - slot, "start")

    fetch(k, slot, "wait")

    pos = lax.broadcasted_iota(jnp.int32, (1, SEG_ROWS), 1).astype(_F32)
    meta = meta_ref[...]
    onehot = jnp.where((pos == meta[:, 0:1]) | (pos == meta[:, 1:2]), 1.0, 0.0).astype(_BF)
    h = h_ref[...] + _dot(onehot, ycat[slot])
    gate = _sigmoid(_dot(_rms(h, gate_norm[...]).astype(_BF), gate_w[...]))
    out_ref[...] = h + gate * _dot(p_ref[...].astype(_BF), ple_w[...])


def _const(shape):
    return pl.BlockSpec(shape, lambda *_: (0,) * len(shape), pipeline_mode=pl.Buffered(1))


def _row(v):
    return v.reshape(1, -1).astype(_F32)


def kernel(x, p, pool_norm, pool_w, pool_scale, kv_norm, w_kv, k_norm, attn_norm, w_q, q_norm, sinks, w_o,
           ffn_norm, w_gu, w_down, router_w, router_b, we_gu, we_down, ple_gate_norm, ple_gate_w, ple_w):
    b, s, d = x.shape
    ts, tq = SEQ_TILE, ATTN_TILE

    inv = ROPE_THETA ** (-jnp.arange(0, HEAD_DIM, 2, dtype=_F32) / HEAD_DIM)
    ang = jnp.arange(s, dtype=_F32)[:, None] * inv[None, :]
    cos_t = jnp.tile(jnp.cos(ang), (1, LANES // (HEAD_DIM // 2)))
    sin_t = jnp.tile(jnp.concatenate([-jnp.sin(ang), jnp.sin(ang)], axis=1), (1, LANES // HEAD_DIM))
    hid = jnp.arange(KV_DIM) // HEAD_DIM
    hmat = ((hid[:, None] == hid[None, :]) * (1.0 / HEAD_DIM)).astype(_BF)

    q_gain = jnp.tile(q_norm[0].astype(_F32) * (HEAD_DIM ** -0.5), N_HEADS).reshape(1, d)
    k_gain = jnp.tile(k_norm.astype(_F32), N_KV_HEADS).reshape(1, KV_DIM)

    tps = s // ts
    n_tiles0 = b * tps
    first = lambda st: jnp.minimum(st, n_tiles0 - 1)
    third = lambda st: jnp.maximum(st - 2, 0)
    tile3 = lambda w: pl.BlockSpec((None, ts, w), lambda st: (third(st) // tps, third(st) % tps, 0))
    h1, q, k, v = pl.pallas_call(
        functools.partial(_layer0_kernel, tiles_per_seq=tps),
        grid=(n_tiles0 + 2,),
        in_specs=[
            pl.BlockSpec((None, ts, d), lambda st: (first(st) // tps, first(st) % tps, 0)),
            pl.BlockSpec((None, None, ts, PLE_DIM), lambda st: (0, third(st) // tps, third(st) % tps, 0)),
            pl.BlockSpec((ts, LANES), lambda st: (third(st) % tps, 0)),
            pl.BlockSpec((ts, LANES), lambda st: (third(st) % tps, 0)),
            _const((1, d)), _const((len(POOL_WINDOWS), POOL_GROUP, POOL_GROUP)), _const((1, d)),
            _const((1, d)), _const((d, 2 * D_FF)), _const((D_FF, d)),
            _const((1, d)), _const((d, d)), _const((PLE_DIM, d)),
            _const((1, d)), _const((d, d)), _const((1, d)),
            _const((1, d)), _const((d, 2 * KV_DIM)), _const((1, KV_DIM)), _const((KV_DIM, KV_DIM)),
        ],
        out_specs=[tile3(d), tile3(d), tile3(KV_DIM), tile3(KV_DIM)],
        out_shape=[jax.ShapeDtypeStruct((b, s, d), _F32), jax.ShapeDtypeStruct((b, s, d), _BF),
                   jax.ShapeDtypeStruct((b, s, KV_DIM), _BF), jax.ShapeDtypeStruct((b, s, KV_DIM), _BF)],
        scratch_shapes=[pltpu.VMEM((POOL_HALO + ts, d), _F32)] + [pltpu.VMEM((POOL_ROWS + 24, d), _F32)] * 2
                       + [pltpu.VMEM((ts, d), _F32), pltpu.VMEM((ts, d), _BF)] * 2 + [pltpu.VMEM((ts, d), _F32)]
                       + [pltpu.VMEM((ts, D_FF), _BF)],
        compiler_params=pltpu.CompilerParams(dimension_semantics=("arbitrary",), vmem_limit_bytes=VMEM_LIMIT),
        name="layer0",
    )(x, p, cos_t, sin_t, _row(pool_norm[0]), pool_w[0].astype(_BF), _row(pool_scale[0]),
      _row(ffn_norm[0]), w_gu[0].astype(_BF), w_down[0].astype(_BF), _row(ple_gate_norm[0]), ple_gate_w[0].astype(_BF), ple_w[0].astype(_BF),
      _row(attn_norm[0]), w_q[0].astype(_BF), q_gain, _row(kv_norm), w_kv.astype(_BF), k_gain, hmat)

    blocks_per_tile = tq // WINDOW
    prev = lambda bi, qi, *_: (bi, jnp.maximum(qi * blocks_per_tile - 1, 0), 0)
    cur = lambda bi, qi, *_: (bi, qi, 0)
    h2 = pl.pallas_call(
        _attn_kernel,
        grid_spec=pltpu.PrefetchScalarGridSpec(
            num_scalar_prefetch=1,
            grid=(b, s // tq),
            in_specs=[
                pl.BlockSpec((None, tq, d), cur),
                pl.BlockSpec((None, WINDOW, KV_DIM), prev), pl.BlockSpec((None, tq, KV_DIM), cur),
                pl.BlockSpec((None, WINDOW, KV_DIM), prev), pl.BlockSpec((None, tq, KV_DIM), cur),
                pl.BlockSpec((None, tq, d), cur),
                pl.BlockSpec((d, d), lambda *_: (0, 0), pipeline_mode=pl.Buffered(1)),
            ],
            out_specs=pl.BlockSpec((None, tq, d), cur),
            scratch_shapes=[pltpu.VMEM((WINDOW + tq, LANES), _BF)] * 4 + [pltpu.VMEM((tq, d), _BF)],
        ),
        out_shape=jax.ShapeDtypeStruct((b, s, d), _F32),
        compiler_params=pltpu.CompilerParams(dimension_semantics=("arbitrary", "arbitrary"),
                                             vmem_limit_bytes=VMEM_LIMIT),
        name="attn",
    )(sinks[0].astype(_F32), q, k, k, v, v, h1, w_o[0].astype(_BF))

    t = b * s
    rw = jnp.pad(router_w[0].astype(_F32), ((0, 0), (0, ROUTER_PAD - N_EXPERTS)))
    rw_hi = rw.astype(_BF)
    rw_lo = (rw - rw_hi.astype(_F32)).astype(_BF)
    rb = jnp.pad(router_b[0].astype(_F32), (0, ROUTER_PAD - N_EXPERTS)).reshape(1, ROUTER_PAD)

    bs, te = ROUTE_TILE, EXPERT_TILE
    nblk = t // bs
    cap = t
    smem = pl.BlockSpec(memory_space=pltpu.SMEM)
    tok = lambda w: pl.BlockSpec((bs, w), lambda ki, *_: (ki, 0))
    xs, meta, seg, totals = pl.pallas_call(
        _route_kernel,
        grid=(nblk,),
        in_specs=[tok(d), _const((1, d)), _const((d, 2 * ROUTER_PAD)), _const((1, ROUTER_PAD))],
        out_specs=[pl.BlockSpec(memory_space=pl.ANY), tok(ROUTER_PAD), smem, smem],
        out_shape=[jax.ShapeDtypeStruct((N_EXPERTS * cap, SORTED_WIDTH), _BF),
                   jax.ShapeDtypeStruct((t, ROUTER_PAD), _F32),
                   jax.ShapeDtypeStruct((nblk * 2 * N_EXPERTS,), jnp.int32),
                   jax.ShapeDtypeStruct((N_EXPERTS,), jnp.int32)],
        scratch_shapes=[pltpu.VMEM((2, SEG_ROWS, SORTED_WIDTH), _BF), pltpu.VMEM((EXPERT_TILE, SORTED_WIDTH), _BF),
                        pltpu.SMEM((N_EXPERTS,), jnp.int32), pltpu.SemaphoreType.DMA((2,))],
        compiler_params=pltpu.CompilerParams(dimension_semantics=("arbitrary",), vmem_limit_bytes=VMEM_LIMIT),
        name="route",
    )(h2.reshape(t, d), _row(ffn_norm[1]), jnp.concatenate([rw_hi, rw_lo], axis=1), rb)

    n_steps = (2 * t + nblk * N_EXPERTS * (SEG_ALIGN - 1)) // te + N_EXPERTS
    tiles_e = (totals + te - 1) // te
    tile_end = jnp.cumsum(tiles_e)
    n_tiles = tile_end[-1]
    step = jnp.minimum(jnp.arange(n_steps, dtype=jnp.int32), n_tiles - 1)
    tile_expert = jnp.sum(step[:, None] >= tile_end[None, :], axis=1).astype(jnp.int32)
    tile_block = (tile_expert * (cap // te) + step - (tile_end - tiles_e)[tile_expert]).astype(jnp.int32)

    xrow = lambda w: pl.BlockSpec((te, w), lambda i, te_r, tb_r, nt_r: (tb_r[i], 0))
    wexp = lambda r, c, cb: pl.BlockSpec((None, None, r, c), lambda i, te_r, tb_r, nt_r: (0, te_r[i], 0, cb))
    ys = pl.pallas_call(
        _expert_kernel,
        grid_spec=pltpu.PrefetchScalarGridSpec(
            num_scalar_prefetch=3,
            grid=(n_steps,),
            in_specs=[xrow(SORTED_WIDTH), wexp(d, D_FF_EXPERT, 0), wexp(d, D_FF_EXPERT, 1), wexp(D_FF_EXPERT, d, 0)],
            out_specs=xrow(d),
            scratch_shapes=[pltpu.VMEM((d, D_FF_EXPERT), _BF)] * 2 + [pltpu.VMEM((D_FF_EXPERT, d), _BF)],
        ),
        out_shape=jax.ShapeDtypeStruct((N_EXPERTS * cap, d), _BF),
        compiler_params=pltpu.CompilerParams(dimension_semantics=("arbitrary",), vmem_limit_bytes=VMEM_LIMIT),
        name="experts",
    )(tile_expert, tile_block, n_tiles.reshape(1).astype(jnp.int32), xs, we_gu, we_gu, we_down)

    out = pl.pallas_call(
        _combine_kernel,
        grid_spec=pltpu.PrefetchScalarGridSpec(
            num_scalar_prefetch=1,
            grid=(nblk,),
            in_specs=[tok(d),
                      pl.BlockSpec((None, None, bs, PLE_DIM), lambda ki, *_: (1, ki // (s // bs), ki % (s // bs), 0)),
                      tok(ROUTER_PAD), pl.BlockSpec(memory_space=pl.ANY),
                      _const((1, d)), _const((d, d)), _const((PLE_DIM, d))],
            out_specs=tok(d),
            scratch_shapes=[pltpu.VMEM((2, SEG_ROWS, d), _BF), pltpu.SemaphoreType.DMA((2,))],
        ),
        out_shape=jax.ShapeDtypeStruct((t, d), _F32),
        compiler_params=pltpu.CompilerParams(dimension_semantics=("arbitrary",), vmem_limit_bytes=VMEM_LIMIT),
        name="combine",
    )(seg, h2.reshape(t, d), p, meta, ys, _row(ple_gate_norm[1]),
      ple_gate_w[1].astype(_BF), ple_w[1].astype(_BF))
    return out.reshape(b, s, d)
```

```python
import functools

import jax
import jax.numpy as jnp
from jax import lax
from jax.experimental import pallas as pl
from jax.experimental.pallas import tpu as pltpu

D_MODEL = 1024
PLE_DIM = 256
POOL_WINDOWS = (2, 4, 8, 16)
POOL_GROUP = D_MODEL // len(POOL_WINDOWS)
POOL_HALO = 32
POOL_ROWS = 256
HEAD_DIM = 64
N_HEADS = 16
N_KV_HEADS = 4
KV_DIM = N_KV_HEADS * HEAD_DIM
WINDOW = 128
ROPE_THETA = 10000.0
D_FF = 2816
FF_CHUNK = 256
N_EXPERTS = 8
D_FF_EXPERT = 1024
EPS = 1e-6
LANES = 128
ROUTER_PAD = LANES

SEQ_TILE = 512
ATTN_TILE = 1024
ROUTE_TILE = 512
EXPERT_TILE = 1024
SEG_ALIGN = 16
SEG_SIZES = tuple(ROUTE_TILE >> i for i in range((ROUTE_TILE // SEG_ALIGN).bit_length()))
FILL_SIZES = tuple((EXPERT_TILE // 2) >> i for i in range((EXPERT_TILE // 2 // SEG_ALIGN).bit_length()))
SEG_ROWS = 2 * ROUTE_TILE + N_EXPERTS * SEG_ALIGN
GATE_LANES = LANES
SORTED_WIDTH = D_MODEL + GATE_LANES
VMEM_LIMIT = 60 * 1024 * 1024

_BF = jnp.bfloat16
_F32 = jnp.float32


def _rms(x, g):
    return x * lax.rsqrt(jnp.mean(x * x, axis=-1, keepdims=True) + EPS) * g


def _sigmoid(x):
    return 0.5 * jnp.tanh(0.5 * x) + 0.5


def _dot(a, b):
    return jnp.dot(a, b, preferred_element_type=_F32)


def _head_meansq(x, hmat):
    x2 = (x * x).astype(_BF)
    width = hmat.shape[0]
    return jnp.concatenate([_dot(x2[:, c:c + width], hmat) for c in range(0, x.shape[1], width)], axis=1)


def _rope(x, cos, sin_signed):
    n = x.shape[1]
    reps = n // LANES
    lane = lax.broadcasted_iota(jnp.int32, (1, n), 1)
    first_half = (lane % HEAD_DIM) < (HEAD_DIM // 2)
    partner = jnp.where(first_half, pltpu.roll(x, n - HEAD_DIM // 2, 1), pltpu.roll(x, HEAD_DIM // 2, 1))
    c = jnp.concatenate([cos] * reps, axis=1)
    s = jnp.concatenate([sin_signed] * reps, axis=1)
    return x * c + partner * s


def _layer0_kernel(x_ref, p_ref, cos_ref, sin_ref, pool_norm, pool_w, pool_scale, ffn_norm, wgu, wd,
                   gate_norm, gate_w, ple_w, attn_norm, wq, q_gain, kv_norm, wkv, k_gain, hmat,
                   h_out, q_out, k_out, v_out, ubuf, sum_a, sum_b, h1_cur, uf_cur, h1_next, uf_next, h2_buf,
                   act_scr, *, tiles_per_seq):
    step = pl.program_id(0)
    si = jnp.minimum(step, pl.num_programs(0) - 3) % tiles_per_seq
    ts = x_ref.shape[0]
    halo, end = POOL_HALO, POOL_HALO + ts
    n_chunks = D_FF // FF_CHUNK
    assert ts % POOL_ROWS == 0 and ts // POOL_ROWS <= n_chunks

    @pl.when(step == 0)
    def _():
        h1_cur[...] = jnp.zeros_like(h1_cur)
        uf_cur[...] = jnp.zeros_like(uf_cur)
        h2_buf[...] = jnp.zeros_like(h2_buf)

    @pl.when(si == 0)
    def _():
        ubuf[0:halo, :] = jnp.zeros((halo, D_MODEL), _F32)

    def first_stage_rows(r0, n):
        a, b = halo + r0, halo + r0 + n
        xr = x_ref[r0:r0 + n, :]
        u = _rms(xr, pool_norm[...])
        ubuf[a:b, :] = u
        g1, g2, g3 = POOL_GROUP, 2 * POOL_GROUP, 3 * POOL_GROUP
        m = n + 24
        s2 = ubuf[a - 24:b, :] + ubuf[a - 25:b - 1, :]
        sum_a[0:m, g1:] = s2[:, g1:]
        s4 = sum_a[8:m, g1:] + sum_a[6:m - 2, g1:]
        sum_b[8:m, g2:] = s4[:, g1:]
        s8 = sum_b[16:m, g2:] + sum_b[12:m - 4, g2:]
        sum_a[16:m, g3:] = s8[:, g1:]
        s16 = sum_a[24:m, g3:] + sum_a[16:m - 8, g3:]
        window_sums = (s2[24:, :g1], s4[16:, :g1], s8[8:, :g1], s16)
        pos = (si * ts + r0 + 1 + lax.broadcasted_iota(jnp.int32, (n, 1), 0)).astype(_F32)
        mixed = []
        for g, win in enumerate(POOL_WINDOWS):
            ug = u[:, g * POOL_GROUP:(g + 1) * POOL_GROUP]
            d = window_sums[g] / jnp.minimum(pos, float(win)) - ug
            mixed.append(_dot(d.astype(_BF), pool_w[g]))
        h_mixed = xr + jnp.concatenate(mixed, axis=1) * pool_scale[...]
        h1_next[r0:r0 + n, :] = h_mixed
        uf_next[r0:r0 + n, :] = _rms(h_mixed, ffn_norm[...]).astype(_BF)

    t = {}

    def embed_gate_matmul():
        t["h"] = h2_buf[...]
        t["gate_pre"] = _dot(_rms(t["h"], gate_norm[...]).astype(_BF), gate_w[...])

    def embed_and_unit_norm():
        h = t["h"] + _sigmoid(t["gate_pre"]) * _dot(p_ref[...].astype(_BF), ple_w[...])
        h_out[...] = h
        t["unit"] = h * lax.rsqrt(jnp.mean(h * h, axis=-1, keepdims=True) + EPS)

    def q_matmul():
        t["q"] = _dot((t["unit"] * attn_norm[...]).astype(_BF), wq[...])

    def q_norm_rope():
        q = t["q"] * lax.rsqrt(_head_meansq(t["q"], hmat[...]) + EPS) * q_gain[...]
        q_out[...] = _rope(q, cos_ref[...], sin_ref[...]).astype(_BF)

    def kv_matmul():
        t["kv"] = _dot((t["unit"] * kv_norm[...]).astype(_BF), wkv[...])

    def k_norm_rope():
        k = t["kv"][:, :KV_DIM]
        k = k * lax.rsqrt(_head_meansq(k, hmat[...]) + EPS) * k_gain[...]
        k_out[...] = _rope(k, cos_ref[...], sin_ref[...]).astype(_BF)
        v_out[...] = t["kv"][:, KV_DIM:].astype(_BF)

    third_stage = (embed_gate_matmul, embed_and_unit_norm, q_matmul, q_norm_rope, kv_matmul, k_norm_rope)
    assert len(third_stage) <= n_chunks

    uf = uf_cur[...]
    for c in range(n_chunks):
        c0 = c * FF_CHUNK
        a = _dot(uf, wgu[:, c0:c0 + FF_CHUNK])
        g = _dot(uf, wgu[:, D_FF + c0:D_FF + c0 + FF_CHUNK])
        act_scr[:, c0:c0 + FF_CHUNK] = (a * _sigmoid(a) * g).astype(_BF)
        if c < len(third_stage):
            third_stage[c]()
        if c * POOL_ROWS < ts:
            first_stage_rows(c * POOL_ROWS, POOL_ROWS)
    acc = _dot(act_scr[...], wd[...])

    h2_buf[...] = h1_cur[...] + acc
    ubuf[0:halo, :] = ubuf[ts:end, :]
    h1_cur[...] = h1_next[...]
    uf_cur[...] = uf_next[...]


def _attn_kernel(sinks_ref, q_ref, kp_ref, kc_ref, vp_ref, vc_ref, h_ref, wo_ref, out_ref,
                 klo, khi, vlo, vhi, o_scr):
    i = pl.program_id(1)
    tq = q_ref.shape[0]
    nblk = tq // WINDOW
    lane = lax.broadcasted_iota(jnp.int32, (1, LANES), 1)
    low = lane < HEAD_DIM

    qi = lax.broadcasted_iota(jnp.int32, (2 * WINDOW, 2 * WINDOW), 0) % WINDOW
    kj = lax.broadcasted_iota(jnp.int32, (2 * WINDOW, 2 * WINDOW), 1)
    in_window = (kj > qi) & (kj <= qi + WINDOW)
    upper_rows = lax.broadcasted_iota(jnp.int32, (2 * WINDOW, 1), 0) >= WINDOW

    for g in range(N_KV_HEADS):
        slab = slice((g // 2) * LANES, (g // 2 + 1) * LANES)
        for src_p, src_c, lo_ref, hi_ref in ((kp_ref, kc_ref, klo, khi), (vp_ref, vc_ref, vlo, vhi)):
            t = jnp.concatenate([src_p[:, slab], src_c[:, slab]], axis=0).astype(_F32)
            r = pltpu.roll(t, HEAD_DIM, 1)
            in_low, in_high = (t, r) if g % 2 == 0 else (r, t)
            lo_ref[...] = jnp.where(low, in_low, 0.0).astype(_BF)
            hi_ref[...] = jnp.where(low, 0.0, in_high).astype(_BF)

        for j in range(nblk):
            qrows = pl.ds(j * WINDOW, WINDOW)
            krows = pl.ds(j * WINDOW, 2 * WINDOW)
            qg = jnp.concatenate([q_ref[qrows, (2 * g) * LANES:(2 * g + 1) * LANES],
                                  q_ref[qrows, (2 * g + 1) * LANES:(2 * g + 2) * LANES]], axis=0)
            mask = in_window & (kj >= jnp.where(i > 0, 0, WINDOW)) if j == 0 else in_window
            o = jnp.zeros((2 * WINDOW, LANES), _F32)
            for half, (k_ref, v_ref) in enumerate(((klo, vlo), (khi, vhi))):
                sink = jnp.where(upper_rows, sinks_ref[4 * g + 2 + half], sinks_ref[4 * g + half])
                s = lax.dot_general(qg, k_ref[krows, :], (((1,), (1,)), ((), ())),
                                    preferred_element_type=_F32)
                s = jnp.where(mask, s, -jnp.inf)
                m = jnp.max(s, axis=1, keepdims=True)
                pr = jnp.exp(s - m)
                denom = jnp.sum(pr, axis=1, keepdims=True) + jnp.exp(sink - m)
                o = o + _dot(pr.astype(_BF), v_ref[krows, :]) / denom
            o_scr[qrows, (2 * g) * LANES:(2 * g + 1) * LANES] = o[:WINDOW].astype(_BF)
            o_scr[qrows, (2 * g + 1) * LANES:(2 * g + 2) * LANES] = o[WINDOW:].astype(_BF)

    out_ref[...] = h_ref[...] + _dot(o_scr[...], wo_ref[...])


def _segment_copies(rows, src_ref, src_base, dst_ref, dst_base, sem, sizes=SEG_SIZES):
    out = []
    for sz in sizes:
        done = rows & (-2 * sz)
        src = src_ref.at[pl.ds(pl.multiple_of(src_base + done, SEG_ALIGN), sz)]
        dst = dst_ref.at[pl.ds(pl.multiple_of(dst_base + done, SEG_ALIGN), sz)]
        out.append(((rows & sz) != 0, pltpu.make_async_copy(src, dst, sem)))
    return out


def _run(copies, op):
    for cond, cp in copies:
        @pl.when(cond)
        def _(cp=cp):
            getattr(cp, op)()


def _block_copies(seg_ref, blk, hbm_ref, buf_ref, sem, to_hbm):
    cap = hbm_ref.shape[0] // N_EXPERTS
    copies = []
    off = jnp.int32(0)
    for e in range(N_EXPERTS):
        base = e * cap + seg_ref[blk * 2 * N_EXPERTS + e]
        rows = seg_ref[blk * 2 * N_EXPERTS + N_EXPERTS + e]
        if to_hbm:
            copies += _segment_copies(rows, buf_ref, off, hbm_ref, base, sem)
        else:
            copies += _segment_copies(rows, hbm_ref, base, buf_ref, off, sem)
        off = off + rows
    return copies


def _route_kernel(h_ref, ffn_norm, rw_both, rb, xs_hbm, meta_ref, seg_ref, tot_ref,
                  comp_scr, zero_scr, cum_ref, sem):
    k = pl.program_id(0)
    bs = h_ref.shape[0]
    cap = xs_hbm.shape[0] // N_EXPERTS
    lane = lax.broadcasted_iota(jnp.int32, (1, ROUTER_PAD), 1)

    @pl.when(k == 0)
    def _():
        for e in range(N_EXPERTS):
            cum_ref[e] = 0

    u = _rms(h_ref[...], ffn_norm[...])
    u_hi = u.astype(_BF)
    u_lo = (u - u_hi.astype(_F32)).astype(_BF)
    both = _dot(u_hi, rw_both[...])
    logits = both[:, :ROUTER_PAD] + _dot(u_lo, rw_both[:, :ROUTER_PAD]) + both[:, ROUTER_PAD:] + rb[...]
    logits = jnp.where(lane < N_EXPERTS, logits, -jnp.inf)
    m1 = jnp.max(logits, axis=1, keepdims=True)
    i1 = jnp.min(jnp.where(logits == m1, lane, ROUTER_PAD), axis=1, keepdims=True)
    rest = jnp.where(lane == i1, -jnp.inf, logits)
    m2 = jnp.max(rest, axis=1, keepdims=True)
    i2 = jnp.min(jnp.where(rest == m2, lane, ROUTER_PAD), axis=1, keepdims=True)
    t = jnp.exp(m2 - m1)
    w1 = 1.0 / (1.0 + t)
    w2 = t * w1

    a1 = lane == i1
    a2 = lane == i2
    assigned = jnp.where(a1 | a2, 1.0, 0.0)
    before = (lax.broadcasted_iota(jnp.int32, (bs, bs), 1) < lax.broadcasted_iota(jnp.int32, (bs, bs), 0))
    rank = _dot(jnp.where(before, 1.0, 0.0).astype(_BF), assigned.astype(_BF))
    count = jnp.sum(assigned, axis=0, keepdims=True).astype(jnp.int32)
    seg_off = jnp.zeros((1, ROUTER_PAD), _F32)
    off = jnp.int32(0)
    for e in range(N_EXPERTS):
        rows = ((count[0, e] + (SEG_ALIGN - 1)) // SEG_ALIGN) * SEG_ALIGN
        seg_off = jnp.where(lane == e, off.astype(_F32), seg_off)
        base = cum_ref[e]
        seg_ref[k * 2 * N_EXPERTS + e] = base
        seg_ref[k * 2 * N_EXPERTS + N_EXPERTS + e] = rows
        cum_ref[e] = base + rows
        off = off + rows
    dst1 = jnp.sum(jnp.where(a1, rank + seg_off, 0.0), axis=1, keepdims=True)
    dst2 = jnp.sum(jnp.where(a2, rank + seg_off, 0.0), axis=1, keepdims=True)
    meta = jnp.where(lane == 0, dst1, jnp.where(lane == 1, dst2, jnp.where(lane == 2, w1, jnp.where(lane == 3, w2, 0.0))))
    meta_ref[...] = meta

    by_token = meta.T
    row = lax.broadcasted_iota(jnp.int32, (SEG_ROWS, 1), 0).astype(_F32)
    from1 = row == by_token[0:1, :]
    from2 = row == by_token[1:2, :]
    onehot = jnp.where(from1 | from2, 1.0, 0.0).astype(_BF)
    gate = jnp.sum(jnp.where(from1, by_token[2:3, :], 0.0) + jnp.where(from2, by_token[3:4, :], 0.0),
                   axis=1, keepdims=True)
    hi = gate.astype(_BF).astype(_F32)
    mid = (gate - hi).astype(_BF).astype(_F32)
    gate_parts = jnp.where(lane == 0, hi, jnp.where(lane == 1, mid, jnp.where(lane == 2, gate - hi - mid, 0.0)))

    slot = k % 2
    comp = comp_scr.at[slot]
    comp[:, :D_MODEL] = _dot(onehot, u_hi).astype(_BF)
    comp[:, D_MODEL:] = gate_parts.astype(_BF)
    _run(_block_copies(seg_ref, k, xs_hbm, comp, sem.at[slot], to_hbm=True), "start")

    @pl.when(k > 0)
    def _():
        _run(_block_copies(seg_ref, k - 1, xs_hbm, comp_scr.at[1 - slot], sem.at[1 - slot], to_hbm=True), "wait")

    @pl.when(k == pl.num_programs(0) - 1)
    def _():
        _run(_block_copies(seg_ref, k, xs_hbm, comp, sem.at[slot], to_hbm=True), "wait")
        zero_scr[...] = jnp.zeros_like(zero_scr)
        fills = []
        for e in range(N_EXPERTS):
            total = cum_ref[e]
            tot_ref[e] = total
            fills += _segment_copies((-total) & (EXPERT_TILE - 1), zero_scr, 0, xs_hbm, e * cap + total,
                                     sem.at[slot], sizes=FILL_SIZES)
        _run(fills, "start")
        _run(fills, "wait")


def _expert_kernel(te_ref, tb_ref, nt_ref, x_ref, wa_f32, wg_f32, wd_f32, y_ref, wa, wg, wd):
    i = pl.program_id(0)

    @pl.when((i == 0) | (te_ref[i] != te_ref[jnp.maximum(i - 1, 0)]))
    def _():
        wa[...] = wa_f32[...].astype(_BF)
        wg[...] = wg_f32[...].astype(_BF)
        wd[...] = wd_f32[...].astype(_BF)

    @pl.when(i < nt_ref[0])
    def _():
        x = x_ref[:, :D_MODEL]
        gate = jnp.sum(x_ref[:, D_MODEL:].astype(_F32), axis=1, keepdims=True)
        a = _dot(x, wa[...])
        g = _dot(x, wg[...])
        act = (a * _sigmoid(a) * g).astype(_BF)
        y_ref[...] = (gate * _dot(act, wd[...])).astype(_BF)


def _combine_kernel(seg_ref, h_ref, p_ref, meta_ref, ys_hbm, gate_norm, gate_w, ple_w, out_ref, ycat, sem):
    k = pl.program_id(0)
    slot = k % 2
    fetch = lambda blk, s, op: _run(_block_copies(seg_ref, blk, ys_hbm, ycat.at[s], sem.at[s], to_hbm=False), op)

    @pl.when(k == 0)
    def _():
        ycat[...] = jnp.zeros_like(ycat)
        fetch(k, slot, "start")

    @pl.when(k + 1 < pl.num_programs(0))
    def _():
        fetch(k + 1, 1 - slot, "start")

    fetch(k, slot, "wait")

    pos = lax.broadcasted_iota(jnp.int32, (1, SEG_ROWS), 1).astype(_F32)
    meta = meta_ref[...]
    onehot = jnp.where((pos == meta[:, 0:1]) | (pos == meta[:, 1:2]), 1.0, 0.0).astype(_BF)
    h = h_ref[...] + _dot(onehot, ycat[slot])
    gate = _sigmoid(_dot(_rms(h, gate_norm[...]).astype(_BF), gate_w[...]))
    out_ref[...] = h + gate * _dot(p_ref[...].astype(_BF), ple_w[...])


def _const(shape):
    return pl.BlockSpec(shape, lambda *_: (0,) * len(shape), pipeline_mode=pl.Buffered(1))


def _row(v):
    return v.reshape(1, -1).astype(_F32)


def kernel(x, p, pool_norm, pool_w, pool_scale, kv_norm, w_kv, k_norm, attn_norm, w_q, q_norm, sinks, w_o,
           ffn_norm, w_gu, w_down, router_w, router_b, we_gu, we_down, ple_gate_norm, ple_gate_w, ple_w):
    b, s, d = x.shape
    ts, tq = SEQ_TILE, ATTN_TILE

    inv = ROPE_THETA ** (-jnp.arange(0, HEAD_DIM, 2, dtype=_F32) / HEAD_DIM)
    ang = jnp.arange(s, dtype=_F32)[:, None] * inv[None, :]
    cos_t = jnp.tile(jnp.cos(ang), (1, LANES // (HEAD_DIM // 2)))
    sin_t = jnp.tile(jnp.concatenate([-jnp.sin(ang), jnp.sin(ang)], axis=1), (1, LANES // HEAD_DIM))
    hid = jnp.arange(KV_DIM) // HEAD_DIM
    hmat = ((hid[:, None] == hid[None, :]) * (1.0 / HEAD_DIM)).astype(_BF)

    q_gain = jnp.tile(q_norm[0].astype(_F32) * (HEAD_DIM ** -0.5), N_HEADS).reshape(1, d)
    k_gain = jnp.tile(k_norm.astype(_F32), N_KV_HEADS).reshape(1, KV_DIM)

    tps = s // ts
    n_tiles0 = b * tps
    first = lambda st: jnp.minimum(st, n_tiles0 - 1)
    third = lambda st: jnp.maximum(st - 2, 0)
    tile3 = lambda w: pl.BlockSpec((None, ts, w), lambda st: (third(st) // tps, third(st) % tps, 0))
    h1, q, k, v = pl.pallas_call(
        functools.partial(_layer0_kernel, tiles_per_seq=tps),
        grid=(n_tiles0 + 2,),
        in_specs=[
            pl.BlockSpec((None, ts, d), lambda st: (first(st) // tps, first(st) % tps, 0)),
            pl.BlockSpec((None, None, ts, PLE_DIM), lambda st: (0, third(st) // tps, third(st) % tps, 0)),
            pl.BlockSpec((ts, LANES), lambda st: (third(st) % tps, 0)),
            pl.BlockSpec((ts, LANES), lambda st: (third(st) % tps, 0)),
            _const((1, d)), _const((len(POOL_WINDOWS), POOL_GROUP, POOL_GROUP)), _const((1, d)),
            _const((1, d)), _const((d, 2 * D_FF)), _const((D_FF, d)),
            _const((1, d)), _const((d, d)), _const((PLE_DIM, d)),
            _const((1, d)), _const((d, d)), _const((1, d)),
            _const((1, d)), _const((d, 2 * KV_DIM)), _const((1, KV_DIM)), _const((KV_DIM, KV_DIM)),
        ],
        out_specs=[tile3(d), tile3(d), tile3(KV_DIM), tile3(KV_DIM)],
        out_shape=[jax.ShapeDtypeStruct((b, s, d), _F32), jax.ShapeDtypeStruct((b, s, d), _BF),
                   jax.ShapeDtypeStruct((b, s, KV_DIM), _BF), jax.ShapeDtypeStruct((b, s, KV_DIM), _BF)],
        scratch_shapes=[pltpu.VMEM((POOL_HALO + ts, d), _F32)] + [pltpu.VMEM((POOL_ROWS + 24, d), _F32)] * 2
                       + [pltpu.VMEM((ts, d), _F32), pltpu.VMEM((ts, d), _BF)] * 2 + [pltpu.VMEM((ts, d), _F32)]
                       + [pltpu.VMEM((ts, D_FF), _BF)],
        compiler_params=pltpu.CompilerParams(dimension_semantics=("arbitrary",), vmem_limit_bytes=VMEM_LIMIT),
        name="layer0",
    )(x, p, cos_t, sin_t, _row(pool_norm[0]), pool_w[0].astype(_BF), _row(pool_scale[0]),
      _row(ffn_norm[0]), w_gu[0].astype(_BF), w_down[0].astype(_BF), _row(ple_gate_norm[0]), ple_gate_w[0].astype(_BF), ple_w[0].astype(_BF),
      _row(attn_norm[0]), w_q[0].astype(_BF), q_gain, _row(kv_norm), w_kv.astype(_BF), k_gain, hmat)

    blocks_per_tile = tq // WINDOW
    prev = lambda bi, qi, *_: (bi, jnp.maximum(qi * blocks_per_tile - 1, 0), 0)
    cur = lambda bi, qi, *_: (bi, qi, 0)
    h2 = pl.pallas_call(
        _attn_kernel,
        grid_spec=pltpu.PrefetchScalarGridSpec(
            num_scalar_prefetch=1,
            grid=(b, s // tq),
            in_specs=[
                pl.BlockSpec((None, tq, d), cur),
                pl.BlockSpec((None, WINDOW, KV_DIM), prev), pl.BlockSpec((None, tq, KV_DIM), cur),
                pl.BlockSpec((None, WINDOW, KV_DIM), prev), pl.BlockSpec((None, tq, KV_DIM), cur),
                pl.BlockSpec((None, tq, d), cur),
                pl.BlockSpec((d, d), lambda *_: (0, 0), pipeline_mode=pl.Buffered(1)),
            ],
            out_specs=pl.BlockSpec((None, tq, d), cur),
            scratch_shapes=[pltpu.VMEM((WINDOW + tq, LANES), _BF)] * 4 + [pltpu.VMEM((tq, d), _BF)],
        ),
        out_shape=jax.ShapeDtypeStruct((b, s, d), _F32),
        compiler_params=pltpu.CompilerParams(dimension_semantics=("arbitrary", "arbitrary"),
                                             vmem_limit_bytes=VMEM_LIMIT),
        name="attn",
    )(sinks[0].astype(_F32), q, k, k, v, v, h1, w_o[0].astype(_BF))

    t = b * s
    rw = jnp.pad(router_w[0].astype(_F32), ((0, 0), (0, ROUTER_PAD - N_EXPERTS)))
    rw_hi = rw.astype(_BF)
    rw_lo = (rw - rw_hi.astype(_F32)).astype(_BF)
    rb = jnp.pad(router_b[0].astype(_F32), (0, ROUTER_PAD - N_EXPERTS)).reshape(1, ROUTER_PAD)

    bs, te = ROUTE_TILE, EXPERT_TILE
    nblk = t // bs
    cap = t
    smem = pl.BlockSpec(memory_space=pltpu.SMEM)
    tok = lambda w: pl.BlockSpec((bs, w), lambda ki, *_: (ki, 0))
    xs, meta, seg, totals = pl.pallas_call(
        _route_kernel,
        grid=(nblk,),
        in_specs=[tok(d), _const((1, d)), _const((d, 2 * ROUTER_PAD)), _const((1, ROUTER_PAD))],
        out_specs=[pl.BlockSpec(memory_space=pl.ANY), tok(ROUTER_PAD), smem, smem],
        out_shape=[jax.ShapeDtypeStruct((N_EXPERTS * cap, SORTED_WIDTH), _BF),
                   jax.ShapeDtypeStruct((t, ROUTER_PAD), _F32),
                   jax.ShapeDtypeStruct((nblk * 2 * N_EXPERTS,), jnp.int32),
                   jax.ShapeDtypeStruct((N_EXPERTS,), jnp.int32)],
        scratch_shapes=[pltpu.VMEM((2, SEG_ROWS, SORTED_WIDTH), _BF), pltpu.VMEM((EXPERT_TILE, SORTED_WIDTH), _BF),
                        pltpu.SMEM((N_EXPERTS,), jnp.int32), pltpu.SemaphoreType.DMA((2,))],
        compiler_params=pltpu.CompilerParams(dimension_semantics=("arbitrary",), vmem_limit_bytes=VMEM_LIMIT),
        name="route",
    )(h2.reshape(t, d), _row(ffn_norm[1]), jnp.concatenate([rw_hi, rw_lo], axis=1), rb)

    n_steps = (2 * t + nblk * N_EXPERTS * (SEG_ALIGN - 1)) // te + N_EXPERTS
    tiles_e = (totals + te - 1) // te
    tile_end = jnp.cumsum(tiles_e)
    n_tiles = tile_end[-1]
    step = jnp.minimum(jnp.arange(n_steps, dtype=jnp.int32), n_tiles - 1)
    tile_expert = jnp.sum(step[:, None] >= tile_end[None, :], axis=1).astype(jnp.int32)
    tile_block = (tile_expert * (cap // te) + step - (tile_end - tiles_e)[tile_expert]).astype(jnp.int32)

    xrow = lambda w: pl.BlockSpec((te, w), lambda i, te_r, tb_r, nt_r: (tb_r[i], 0))
    wexp = lambda r, c, cb: pl.BlockSpec((None, None, r, c), lambda i, te_r, tb_r, nt_r: (0, te_r[i], 0, cb))
    ys = pl.pallas_call(
        _expert_kernel,
        grid_spec=pltpu.PrefetchScalarGridSpec(
            num_scalar_prefetch=3,
            grid=(n_steps,),
            in_specs=[xrow(SORTED_WIDTH), wexp(d, D_FF_EXPERT, 0), wexp(d, D_FF_EXPERT, 1), wexp(D_FF_EXPERT, d, 0)],
            out_specs=xrow(d),
            scratch_shapes=[pltpu.VMEM((d, D_FF_EXPERT), _BF)] * 2 + [pltpu.VMEM((D_FF_EXPERT, d), _BF)],
        ),
        out_shape=jax.ShapeDtypeStruct((N_EXPERTS * cap, d), _BF),
        compiler_params=pltpu.CompilerParams(dimension_semantics=("arbitrary",), vmem_limit_bytes=VMEM_LIMIT),
        name="experts",
    )(tile_expert, tile_block, n_tiles.reshape(1).astype(jnp.int32), xs, we_gu, we_gu, we_down)

    out = pl.pallas_call(
        _combine_kernel,
        grid_spec=pltpu.PrefetchScalarGridSpec(
            num_scalar_prefetch=1,
            grid=(nblk,),
            in_specs=[tok(d),
                      pl.BlockSpec((None, None, bs, PLE_DIM), lambda ki, *_: (1, ki // (s // bs), ki % (s // bs), 0)),
                      tok(ROUTER_PAD), pl.BlockSpec(memory_space=pl.ANY),
                      _const((1, d)), _const((d, d)), _const((PLE_DIM, d))],
            out_specs=tok(d),
            scratch_shapes=[pltpu.VMEM((2, SEG_ROWS, d), _BF), pltpu.SemaphoreType.DMA((2,))],
        ),
        out_shape=jax.ShapeDtypeStruct((t, d), _F32),
        compiler_params=pltpu.CompilerParams(dimension_semantics=("arbitrary",), vmem_limit_bytes=VMEM_LIMIT),
        name="combine",
    )(seg, h2.reshape(t, d), p, meta, ys, _row(ple_gate_norm[1]),
      ple_gate_w[1].astype(_BF), ple_w[1].astype(_BF))
    return out.reshape(b, s, d)
```

```python
import functools

import jax
import jax.numpy as jnp
from jax import lax
from jax.experimental import pallas as pl
from jax.experimental.pallas import tpu as pltpu

D_MODEL = 1024
PLE_DIM = 256
POOL_WINDOWS = (2, 4, 8, 16)
POOL_GROUP = D_MODEL // len(POOL_WINDOWS)
POOL_HALO = 32
POOL_ROWS = 256
HEAD_DIM = 64
N_HEADS = 16
N_KV_HEADS = 4
KV_DIM = N_KV_HEADS * HEAD_DIM
WINDOW = 128
ROPE_THETA = 10000.0
D_FF = 2816
FF_CHUNK = 256
N_EXPERTS = 8
D_FF_EXPERT = 1024
EPS = 1e-6
LANES = 128
ROUTER_PAD = LANES

SEQ_TILE = 512
ATTN_TILE = 1024
ROUTE_TILE = 512
EXPERT_TILE = 1024
SEG_ALIGN = 16
SEG_SIZES = tuple(ROUTE_TILE >> i for i in range((ROUTE_TILE // SEG_ALIGN).bit_length()))
FILL_SIZES = tuple((EXPERT_TILE // 2) >> i for i in range((EXPERT_TILE // 2 // SEG_ALIGN).bit_length()))
SEG_ROWS = 2 * ROUTE_TILE + N_EXPERTS * SEG_ALIGN
GATE_LANES = LANES
SORTED_WIDTH = D_MODEL + GATE_LANES
VMEM_LIMIT = 60 * 1024 * 1024

_BF = jnp.bfloat16
_F32 = jnp.float32


def _rms(x, g):
    return x * lax.rsqrt(jnp.mean(x * x, axis=-1, keepdims=True) + EPS) * g


def _sigmoid(x):
    return 0.5 * jnp.tanh(0.5 * x) + 0.5


def _dot(a, b):
    return jnp.dot(a, b, preferred_element_type=_F32)


def _head_meansq(x, hmat):
    x2 = (x * x).astype(_BF)
    width = hmat.shape[0]
    return jnp.concatenate([_dot(x2[:, c:c + width], hmat) for c in range(0, x.shape[1], width)], axis=1)


def _rope(x, cos, sin_signed):
    n = x.shape[1]
    reps = n // LANES
    lane = lax.broadcasted_iota(jnp.int32, (1, n), 1)
    first_half = (lane % HEAD_DIM) < (HEAD_DIM // 2)
    partner = jnp.where(first_half, pltpu.roll(x, n - HEAD_DIM // 2, 1), pltpu.roll(x, HEAD_DIM // 2, 1))
    c = jnp.concatenate([cos] * reps, axis=1)
    s = jnp.concatenate([sin_signed] * reps, axis=1)
    return x * c + partner * s


def _layer0_kernel(x_ref, p_ref, cos_ref, sin_ref, pool_norm, pool_w, pool_scale, ffn_norm, wgu, wd,
                   gate_norm, gate_w, ple_w, attn_norm, wq, q_gain, kv_norm, wkv, k_gain, hmat,
                   h_out, q_out, k_out, v_out, ubuf, sum_a, sum_b, h1_cur, uf_cur, h1_next, uf_next, h2_buf,
                   act_scr, *, tiles_per_seq):
    step = pl.program_id(0)
    si = jnp.minimum(step, pl.num_programs(0) - 3) % tiles_per_seq
    ts = x_ref.shape[0]
    halo, end = POOL_HALO, POOL_HALO + ts
    n_chunks = D_FF // FF_CHUNK
    assert ts % POOL_ROWS == 0 and ts // POOL_ROWS <= n_chunks

    @pl.when(step == 0)
    def _():
        h1_cur[...] = jnp.zeros_like(h1_cur)
        uf_cur[...] = jnp.zeros_like(uf_cur)
        h2_buf[...] = jnp.zeros_like(h2_buf)

    @pl.when(si == 0)
    def _():
        ubuf[0:halo, :] = jnp.zeros((halo, D_MODEL), _F32)

    def first_stage_rows(r0, n):
        a, b = halo + r0, halo + r0 + n
        xr = x_ref[r0:r0 + n, :]
        u = _rms(xr, pool_norm[...])
        ubuf[a:b, :] = u
        g1, g2, g3 = POOL_GROUP, 2 * POOL_GROUP, 3 * POOL_GROUP
        m = n + 24
        s2 = ubuf[a - 24:b, :] + ubuf[a - 25:b - 1, :]
        sum_a[0:m, g1:] = s2[:, g1:]
        s4 = sum_a[8:m, g1:] + sum_a[6:m - 2, g1:]
        sum_b[8:m, g2:] = s4[:, g1:]
        s8 = sum_b[16:m, g2:] + sum_b[12:m - 4, g2:]
        sum_a[16:m, g3:] = s8[:, g1:]
        s16 = sum_a[24:m, g3:] + sum_a[16:m - 8, g3:]
        window_sums = (s2[24:, :g1], s4[16:, :g1], s8[8:, :g1], s16)
        pos = (si * ts + r0 + 1 + lax.broadcasted_iota(jnp.int32, (n, 1), 0)).astype(_F32)
        mixed = []
        for g, win in enumerate(POOL_WINDOWS):
            ug = u[:, g * POOL_GROUP:(g + 1) * POOL_GROUP]
            d = window_sums[g] / jnp.minimum(pos, float(win)) - ug
            mixed.append(_dot(d.astype(_BF), pool_w[g]))
        h_mixed = xr + jnp.concatenate(mixed, axis=1) * pool_scale[...]
        h1_next[r0:r0 + n, :] = h_mixed
        uf_next[r0:r0 + n, :] = _rms(h_mixed, ffn_norm[...]).astype(_BF)

    t = {}

    def embed_gate_matmul():
        t["h"] = h2_buf[...]
        t["gate_pre"] = _dot(_rms(t["h"], gate_norm[...]).astype(_BF), gate_w[...])

    def embed_and_unit_norm():
        h = t["h"] + _sigmoid(t["gate_pre"]) * _dot(p_ref[...].astype(_BF), ple_w[...])
        h_out[...] = h
        t["unit"] = h * lax.rsqrt(jnp.mean(h * h, axis=-1, keepdims=True) + EPS)

    def q_matmul():
        t["q"] = _dot((t["unit"] * attn_norm[...]).astype(_BF), wq[...])

    def q_norm_rope():
        q = t["q"] * lax.rsqrt(_head_meansq(t["q"], hmat[...]) + EPS) * q_gain[...]
        q_out[...] = _rope(q, cos_ref[...], sin_ref[...]).astype(_BF)

    def kv_matmul():
        t["kv"] = _dot((t["unit"] * kv_norm[...]).astype(_BF), wkv[...])

    def k_norm_rope():
        k = t["kv"][:, :KV_DIM]
        k = k * lax.rsqrt(_head_meansq(k, hmat[...]) + EPS) * k_gain[...]
        k_out[...] = _rope(k, cos_ref[...], sin_ref[...]).astype(_BF)
        v_out[...] = t["kv"][:, KV_DIM:].astype(_BF)

    third_stage = (embed_gate_matmul, embed_and_unit_norm, q_matmul, q_norm_rope, kv_matmul, k_norm_rope)
    assert len(third_stage) <= n_chunks

    uf = uf_cur[...]
    for c in range(n_chunks):
        c0 = c * FF_CHUNK
        a = _dot(uf, wgu[:, c0:c0 + FF_CHUNK])
        g = _dot(uf, wgu[:, D_FF + c0:D_FF + c0 + FF_CHUNK])
        act_scr[:, c0:c0 + FF_CHUNK] = (a * _sigmoid(a) * g).astype(_BF)
        if c < len(third_stage):
            third_stage[c]()
        if c * POOL_ROWS < ts:
            first_stage_rows(c * POOL_ROWS, POOL_ROWS)
    acc = _dot(act_scr[...], wd[...])

    h2_buf[...] = h1_cur[...] + acc
    ubuf[0:halo, :] = ubuf[ts:end, :]
    h1_cur[...] = h1_next[...]
    uf_cur[...] = uf_next[...]


def _attn_kernel(sinks_ref, q_ref, kp_ref, kc_ref, vp_ref, vc_ref, h_ref, wo_ref, out_ref,
                 klo, khi, vlo, vhi, o_scr):
    i = pl.program_id(1)
    tq = q_ref.shape[0]
    nblk = tq // WINDOW
    lane = lax.broadcasted_iota(jnp.int32, (1, LANES), 1)
    low = lane < HEAD_DIM

    qi = lax.broadcasted_iota(jnp.int32, (2 * WINDOW, 2 * WINDOW), 0) % WINDOW
    kj = lax.broadcasted_iota(jnp.int32, (2 * WINDOW, 2 * WINDOW), 1)
    in_window = (kj > qi) & (kj <= qi + WINDOW)
    upper_rows = lax.broadcasted_iota(jnp.int32, (2 * WINDOW, 1), 0) >= WINDOW

    for g in range(N_KV_HEADS):
        slab = slice((g // 2) * LANES, (g // 2 + 1) * LANES)
        for src_p, src_c, lo_ref, hi_ref in ((kp_ref, kc_ref, klo, khi), (vp_ref, vc_ref, vlo, vhi)):
            t = jnp.concatenate([src_p[:, slab], src_c[:, slab]], axis=0).astype(_F32)
            r = pltpu.roll(t, HEAD_DIM, 1)
            in_low, in_high = (t, r) if g % 2 == 0 else (r, t)
            lo_ref[...] = jnp.where(low, in_low, 0.0).astype(_BF)
            hi_ref[...] = jnp.where(low, 0.0, in_high).astype(_BF)

        for j in range(nblk):
            qrows = pl.ds(j * WINDOW, WINDOW)
            krows = pl.ds(j * WINDOW, 2 * WINDOW)
            qg = jnp.concatenate([q_ref[qrows, (2 * g) * LANES:(2 * g + 1) * LANES],
                                  q_ref[qrows, (2 * g + 1) * LANES:(2 * g + 2) * LANES]], axis=0)
            mask = in_window & (kj >= jnp.where(i > 0, 0, WINDOW)) if j == 0 else in_window
            o = jnp.zeros((2 * WINDOW, LANES), _F32)
            for half, (k_ref, v_ref) in enumerate(((klo, vlo), (khi, vhi))):
                sink = jnp.where(upper_rows, sinks_ref[4 * g + 2 + half], sinks_ref[4 * g + half])
                s = lax.dot_general(qg, k_ref[krows, :], (((1,), (1,)), ((), ())),
                                    preferred_element_type=_F32)
                s = jnp.where(mask, s, -jnp.inf)
                m = jnp.maximum(jnp.max(s, axis=1, keepdims=True), sink)
                pr = jnp.exp(s - m)
                denom = jnp.sum(pr, axis=1, keepdims=True) + jnp.exp(sink - m)
                o = o + _dot(pr.astype(_BF), v_ref[krows, :]) / denom
            o_scr[qrows, (2 * g) * LANES:(2 * g + 1) * LANES] = o[:WINDOW].astype(_BF)
            o_scr[qrows, (2 * g + 1) * LANES:(2 * g + 2) * LANES] = o[WINDOW:].astype(_BF)

    out_ref[...] = h_ref[...] + _dot(o_scr[...], wo_ref[...])


def _segment_copies(rows, src_ref, src_base, dst_ref, dst_base, sem, sizes=SEG_SIZES):
    out = []
    for sz in sizes:
        done = rows & (-2 * sz)
        src = src_ref.at[pl.ds(pl.multiple_of(src_base + done, SEG_ALIGN), sz)]
        dst = dst_ref.at[pl.ds(pl.multiple_of(dst_base + done, SEG_ALIGN), sz)]
        out.append(((rows & sz) != 0, pltpu.make_async_copy(src, dst, sem)))
    return out


def _run(copies, op):
    for cond, cp in copies:
        @pl.when(cond)
        def _(cp=cp):
            getattr(cp, op)()


def _block_copies(seg_ref, blk, hbm_ref, buf_ref, sem, to_hbm):
    cap = hbm_ref.shape[0] // N_EXPERTS
    copies = []
    off = jnp.int32(0)
    for e in range(N_EXPERTS):
        base = e * cap + seg_ref[blk * 2 * N_EXPERTS + e]
        rows = seg_ref[blk * 2 * N_EXPERTS + N_EXPERTS + e]
        if to_hbm:
            copies += _segment_copies(rows, buf_ref, off, hbm_ref, base, sem)
        else:
            copies += _segment_copies(rows, hbm_ref, base, buf_ref, off, sem)
        off = off + rows
    return copies


def _wait_block(seg_ref, blk, hbm_ref, buf_ref, sem, to_hbm):
    rows = seg_ref[blk * 2 * N_EXPERTS + N_EXPERTS]
    for e in range(1, N_EXPERTS):
        rows = rows + seg_ref[blk * 2 * N_EXPERTS + N_EXPERTS + e]
    rows = pl.multiple_of(rows, SEG_ALIGN)
    src, dst = (buf_ref, hbm_ref) if to_hbm else (hbm_ref, buf_ref)
    pltpu.make_async_copy(src.at[pl.ds(0, rows)], dst.at[pl.ds(0, rows)], sem).wait()


def _route_kernel(h_ref, ffn_norm, rw_both, rb, xs_hbm, meta_ref, seg_ref, tot_ref,
                  comp_scr, zero_scr, cum_ref, sem):
    k = pl.program_id(0)
    bs = h_ref.shape[0]
    cap = xs_hbm.shape[0] // N_EXPERTS
    lane = lax.broadcasted_iota(jnp.int32, (1, ROUTER_PAD), 1)

    @pl.when(k == 0)
    def _():
        for e in range(N_EXPERTS):
            cum_ref[e] = 0

    u = _rms(h_ref[...], ffn_norm[...])
    u_hi = u.astype(_BF)
    u_lo = (u - u_hi.astype(_F32)).astype(_BF)
    both = _dot(u_hi, rw_both[...])
    logits = both[:, :ROUTER_PAD] + _dot(u_lo, rw_both[:, :ROUTER_PAD]) + both[:, ROUTER_PAD:] + rb[...]
    logits = jnp.where(lane < N_EXPERTS, logits, -jnp.inf)
    m1 = jnp.max(logits, axis=1, keepdims=True)
    i1 = jnp.min(jnp.where(logits == m1, lane, ROUTER_PAD), axis=1, keepdims=True)
    rest = jnp.where(lane == i1, -jnp.inf, logits)
    m2 = jnp.max(rest, axis=1, keepdims=True)
    i2 = jnp.min(jnp.where(rest == m2, lane, ROUTER_PAD), axis=1, keepdims=True)
    t = jnp.exp(m2 - m1)
    w1 = 1.0 / (1.0 + t)
    w2 = t * w1

    a1 = lane == i1
    a2 = lane == i2
    assigned = jnp.where(a1 | a2, 1.0, 0.0)
    before = (lax.broadcasted_iota(jnp.int32, (bs, bs), 1) < lax.broadcasted_iota(jnp.int32, (bs, bs), 0))
    rank = _dot(jnp.where(before, 1.0, 0.0).astype(_BF), assigned.astype(_BF))
    count = jnp.sum(assigned, axis=0, keepdims=True).astype(jnp.int32)
    seg_off = jnp.zeros((1, ROUTER_PAD), _F32)
    off = jnp.int32(0)
    for e in range(N_EXPERTS):
        rows = ((count[0, e] + (SEG_ALIGN - 1)) // SEG_ALIGN) * SEG_ALIGN
        seg_off = jnp.where(lane == e, off.astype(_F32), seg_off)
        base = cum_ref[e]
        seg_ref[k * 2 * N_EXPERTS + e] = base
        seg_ref[k * 2 * N_EXPERTS + N_EXPERTS + e] = rows
        cum_ref[e] = base + rows
        off = off + rows
    dst1 = jnp.sum(jnp.where(a1, rank + seg_off, 0.0), axis=1, keepdims=True)
    dst2 = jnp.sum(jnp.where(a2, rank + seg_off, 0.0), axis=1, keepdims=True)
    meta = jnp.where(lane == 0, dst1, jnp.where(lane == 1, dst2, jnp.where(lane == 2, w1, jnp.where(lane == 3, w2, 0.0))))
    meta_ref[...] = meta

    by_token = meta.T
    row = lax.broadcasted_iota(jnp.int32, (SEG_ROWS, 1), 0).astype(_F32)
    from1 = row == by_token[0:1, :]
    from2 = row == by_token[1:2, :]
    onehot = jnp.where(from1 | from2, 1.0, 0.0).astype(_BF)
    gate = jnp.sum(jnp.where(from1, by_token[2:3, :], 0.0) + jnp.where(from2, by_token[3:4, :], 0.0),
                   axis=1, keepdims=True)
    hi = gate.astype(_BF).astype(_F32)
    mid = (gate - hi).astype(_BF).astype(_F32)
    gate_parts = jnp.where(lane == 0, hi, jnp.where(lane == 1, mid, jnp.where(lane == 2, gate - hi - mid, 0.0)))

    slot = k % 2
    comp = comp_scr.at[slot]
    comp[:, :D_MODEL] = _dot(onehot, u_hi).astype(_BF)
    comp[:, D_MODEL:] = gate_parts.astype(_BF)
    _run(_block_copies(seg_ref, k, xs_hbm, comp, sem.at[slot], to_hbm=True), "start")

    @pl.when(k > 0)
    def _():
        _wait_block(seg_ref, k - 1, xs_hbm, comp_scr.at[1 - slot], sem.at[1 - slot], to_hbm=True)

    @pl.when(k == pl.num_programs(0) - 1)
    def _():
        _wait_block(seg_ref, k, xs_hbm, comp, sem.at[slot], to_hbm=True)
        zero_scr[...] = jnp.zeros_like(zero_scr)
        fills = []
        for e in range(N_EXPERTS):
            total = cum_ref[e]
            tot_ref[e] = total
            fills += _segment_copies((-total) & (EXPERT_TILE - 1), zero_scr, 0, xs_hbm, e * cap + total,
                                     sem.at[slot], sizes=FILL_SIZES)
        _run(fills, "start")
        _run(fills, "wait")


def _expert_kernel(te_ref, tb_ref, nt_ref, x_ref, wa_f32, wg_f32, wd_f32, y_ref, wa, wg, wd):
    i = pl.program_id(0)

    @pl.when((i == 0) | (te_ref[i] != te_ref[jnp.maximum(i - 1, 0)]))
    def _():
        wa[...] = wa_f32[...].astype(_BF)
        wg[...] = wg_f32[...].astype(_BF)
        wd[...] = wd_f32[...].astype(_BF)

    @pl.when(i < nt_ref[0])
    def _():
        x = x_ref[:, :D_MODEL]
        gate = jnp.sum(x_ref[:, D_MODEL:].astype(_F32), axis=1, keepdims=True)
        a = _dot(x, wa[...])
        g = _dot(x, wg[...])
        act = (a * _sigmoid(a) * g).astype(_BF)
        y_ref[...] = (gate * _dot(act, wd[...])).astype(_BF)


def _combine_kernel(seg_ref, h_ref, p_ref, meta_ref, ys_hbm, gate_norm, gate_w, ple_w, out_ref, ycat, sem):
    k = pl.program_id(0)
    slot = k % 2
    fetch = lambda blk, s, op: _run(_block_copies(seg_ref, blk, ys_hbm, ycat.at[s], sem.at[s], to_hbm=False), op)

    @pl.when(k == 0)
    def _():
        ycat[...] = jnp.zeros_like(ycat)
        fetch(k, slot, "start")

    @pl.when(k + 1 < pl.num_programs(0))
    def _():
        fetch(k + 1, 1 - slot, "start")

    _wait_block(seg_ref, k, ys_hbm, ycat.at[slot], sem.at[slot], to_hbm=False)

    pos = lax.broadcasted_iota(jnp.int32, (1, SEG_ROWS), 1).astype(_F32)
    meta = meta_ref[...]
    onehot = jnp.where((pos == meta[:, 0:1]) | (pos == meta[:, 1:2]), 1.0, 0.0).astype(_BF)
    h = h_ref[...] + _dot(onehot, ycat[slot])
    gate = _sigmoid(_dot(_rms(h, gate_norm[...]).astype(_BF), gate_w[...]))
    out_ref[...] = h + gate * _dot(p_ref[...].astype(_BF), ple_w[...])


def _const(shape):
    return pl.BlockSpec(shape, lambda *_: (0,) * len(shape), pipeline_mode=pl.Buffered(1))


def _row(v):
    return v.reshape(1, -1).astype(_F32)


def kernel(x, p, pool_norm, pool_w, pool_scale, kv_norm, w_kv, k_norm, attn_norm, w_q, q_norm, sinks, w_o,
           ffn_norm, w_gu, w_down, router_w, router_b, we_gu, we_down, ple_gate_norm, ple_gate_w, ple_w):
    b, s, d = x.shape
    ts, tq = SEQ_TILE, ATTN_TILE

    inv = ROPE_THETA ** (-jnp.arange(0, HEAD_DIM, 2, dtype=_F32) / HEAD_DIM)
    ang = jnp.arange(s, dtype=_F32)[:, None] * inv[None, :]
    cos_t = jnp.tile(jnp.cos(ang), (1, LANES // (HEAD_DIM // 2)))
    sin_t = jnp.tile(jnp.concatenate([-jnp.sin(ang), jnp.sin(ang)], axis=1), (1, LANES // HEAD_DIM))
    hid = jnp.arange(KV_DIM) // HEAD_DIM
    hmat = ((hid[:, None] == hid[None, :]) * (1.0 / HEAD_DIM)).astype(_BF)

    q_gain = jnp.tile(q_norm[0].astype(_F32) * (HEAD_DIM ** -0.5), N_HEADS).reshape(1, d)
    k_gain = jnp.tile(k_norm.astype(_F32), N_KV_HEADS).reshape(1, KV_DIM)

    tps = s // ts
    n_tiles0 = b * tps
    first = lambda st: jnp.minimum(st, n_tiles0 - 1)
    third = lambda st: jnp.maximum(st - 2, 0)
    tile3 = lambda w: pl.BlockSpec((None, ts, w), lambda st: (third(st) // tps, third(st) % tps, 0))
    h1, q, k, v = pl.pallas_call(
        functools.partial(_layer0_kernel, tiles_per_seq=tps),
        grid=(n_tiles0 + 2,),
        in_specs=[
            pl.BlockSpec((None, ts, d), lambda st: (first(st) // tps, first(st) % tps, 0)),
            pl.BlockSpec((None, None, ts, PLE_DIM), lambda st: (0, third(st) // tps, third(st) % tps, 0)),
            pl.BlockSpec((ts, LANES), lambda st: (third(st) % tps, 0)),
            pl.BlockSpec((ts, LANES), lambda st: (third(st) % tps, 0)),
            _const((1, d)), _const((len(POOL_WINDOWS), POOL_GROUP, POOL_GROUP)), _const((1, d)),
            _const((1, d)), _const((d, 2 * D_FF)), _const((D_FF, d)),
            _const((1, d)), _const((d, d)), _const((PLE_DIM, d)),
            _const((1, d)), _const((d, d)), _const((1, d)),
            _const((1, d)), _const((d, 2 * KV_DIM)), _const((1, KV_DIM)), _const((KV_DIM, KV_DIM)),
        ],
        out_specs=[tile3(d), tile3(d), tile3(KV_DIM), tile3(KV_DIM)],
        out_shape=[jax.ShapeDtypeStruct((b, s, d), _F32), jax.ShapeDtypeStruct((b, s, d), _BF),
                   jax.ShapeDtypeStruct((b, s, KV_DIM), _BF), jax.ShapeDtypeStruct((b, s, KV_DIM), _BF)],
        scratch_shapes=[pltpu.VMEM((POOL_HALO + ts, d), _F32)] + [pltpu.VMEM((POOL_ROWS + 24, d), _F32)] * 2
                       + [pltpu.VMEM((ts, d), _F32), pltpu.VMEM((ts, d), _BF)] * 2 + [pltpu.VMEM((ts, d), _F32)]
                       + [pltpu.VMEM((ts, D_FF), _BF)],
        compiler_params=pltpu.CompilerParams(dimension_semantics=("arbitrary",), vmem_limit_bytes=VMEM_LIMIT),
        name="layer0",
    )(x, p, cos_t, sin_t, _row(pool_norm[0]), pool_w[0].astype(_BF), _row(pool_scale[0]),
      _row(ffn_norm[0]), w_gu[0].astype(_BF), w_down[0].astype(_BF), _row(ple_gate_norm[0]), ple_gate_w[0].astype(_BF), ple_w[0].astype(_BF),
      _row(attn_norm[0]), w_q[0].astype(_BF), q_gain, _row(kv_norm), w_kv.astype(_BF), k_gain, hmat)

    blocks_per_tile = tq // WINDOW
    prev = lambda bi, qi, *_: (bi, jnp.maximum(qi * blocks_per_tile - 1, 0), 0)
    cur = lambda bi, qi, *_: (bi, qi, 0)
    h2 = pl.pallas_call(
        _attn_kernel,
        grid_spec=pltpu.PrefetchScalarGridSpec(
            num_scalar_prefetch=1,
            grid=(b, s // tq),
            in_specs=[
                pl.BlockSpec((None, tq, d), cur),
                pl.BlockSpec((None, WINDOW, KV_DIM), prev), pl.BlockSpec((None, tq, KV_DIM), cur),
                pl.BlockSpec((None, WINDOW, KV_DIM), prev), pl.BlockSpec((None, tq, KV_DIM), cur),
                pl.BlockSpec((None, tq, d), cur),
                pl.BlockSpec((d, d), lambda *_: (0, 0), pipeline_mode=pl.Buffered(1)),
            ],
            out_specs=pl.BlockSpec((None, tq, d), cur),
            scratch_shapes=[pltpu.VMEM((WINDOW + tq, LANES), _BF)] * 4 + [pltpu.VMEM((tq, d), _BF)],
        ),
        out_shape=jax.ShapeDtypeStruct((b, s, d), _F32),
        compiler_params=pltpu.CompilerParams(dimension_semantics=("arbitrary", "arbitrary"),
                                             vmem_limit_bytes=VMEM_LIMIT),
        name="attn",
    )(sinks[0].astype(_F32), q, k, k, v, v, h1, w_o[0].astype(_BF))

    t = b * s
    rw = jnp.pad(router_w[0].astype(_F32), ((0, 0), (0, ROUTER_PAD - N_EXPERTS)))
    rw_hi = rw.astype(_BF)
    rw_lo = (rw - rw_hi.astype(_F32)).astype(_BF)
    rb = jnp.pad(router_b[0].astype(_F32), (0, ROUTER_PAD - N_EXPERTS)).reshape(1, ROUTER_PAD)

    bs, te = ROUTE_TILE, EXPERT_TILE
    nblk = t // bs
    cap = t
    smem = pl.BlockSpec(memory_space=pltpu.SMEM)
    tok = lambda w: pl.BlockSpec((bs, w), lambda ki, *_: (ki, 0))
    xs, meta, seg, totals = pl.pallas_call(
        _route_kernel,
        grid=(nblk,),
        in_specs=[tok(d), _const((1, d)), _const((d, 2 * ROUTER_PAD)), _const((1, ROUTER_PAD))],
        out_specs=[pl.BlockSpec(memory_space=pl.ANY), tok(ROUTER_PAD), smem, smem],
        out_shape=[jax.ShapeDtypeStruct((N_EXPERTS * cap, SORTED_WIDTH), _BF),
                   jax.ShapeDtypeStruct((t, ROUTER_PAD), _F32),
                   jax.ShapeDtypeStruct((nblk * 2 * N_EXPERTS,), jnp.int32),
                   jax.ShapeDtypeStruct((N_EXPERTS,), jnp.int32)],
        scratch_shapes=[pltpu.VMEM((2, SEG_ROWS, SORTED_WIDTH), _BF), pltpu.VMEM((EXPERT_TILE, SORTED_WIDTH), _BF),
                        pltpu.SMEM((N_EXPERTS,), jnp.int32), pltpu.SemaphoreType.DMA((2,))],
        compiler_params=pltpu.CompilerParams(dimension_semantics=("arbitrary",), vmem_limit_bytes=VMEM_LIMIT),
        name="route",
    )(h2.reshape(t, d), _row(ffn_norm[1]), jnp.concatenate([rw_hi, rw_lo], axis=1), rb)

    n_steps = (2 * t + nblk * N_EXPERTS * (SEG_ALIGN - 1)) // te + N_EXPERTS
    tiles_e = (totals + te - 1) // te
    tile_end = jnp.cumsum(tiles_e)
    n_tiles = tile_end[-1]
    step = jnp.minimum(jnp.arange(n_steps, dtype=jnp.int32), n_tiles - 1)
    tile_expert = jnp.sum(step[:, None] >= tile_end[None, :], axis=1).astype(jnp.int32)
    tile_block = (tile_expert * (cap // te) + step - (tile_end - tiles_e)[tile_expert]).astype(jnp.int32)

    xrow = lambda w: pl.BlockSpec((te, w), lambda i, te_r, tb_r, nt_r: (tb_r[i], 0))
    wexp = lambda r, c, cb: pl.BlockSpec((None, None, r, c), lambda i, te_r, tb_r, nt_r: (0, te_r[i], 0, cb))
    ys = pl.pallas_call(
        _expert_kernel,
        grid_spec=pltpu.PrefetchScalarGridSpec(
            num_scalar_prefetch=3,
            grid=(n_steps,),
            in_specs=[xrow(SORTED_WIDTH), wexp(d, D_FF_EXPERT, 0), wexp(d, D_FF_EXPERT, 1), wexp(D_FF_EXPERT, d, 0)],
            out_specs=xrow(d),
            scratch_shapes=[pltpu.VMEM((d, D_FF_EXPERT), _BF)] * 2 + [pltpu.VMEM((D_FF_EXPERT, d), _BF)],
        ),
        out_shape=jax.ShapeDtypeStruct((N_EXPERTS * cap, d), _BF),
        compiler_params=pltpu.CompilerParams(dimension_semantics=("arbitrary",), vmem_limit_bytes=VMEM_LIMIT),
        name="experts",
    )(tile_expert, tile_block, n_tiles.reshape(1).astype(jnp.int32), xs, we_gu, we_gu, we_down)

    out = pl.pallas_call(
        _combine_kernel,
        grid_spec=pltpu.PrefetchScalarGridSpec(
            num_scalar_prefetch=1,
            grid=(nblk,),
            in_specs=[tok(d),
                      pl.BlockSpec((None, None, bs, PLE_DIM), lambda ki, *_: (1, ki // (s // bs), ki % (s // bs), 0)),
                      tok(ROUTER_PAD), pl.BlockSpec(memory_space=pl.ANY),
                      _const((1, d)), _const((d, d)), _const((PLE_DIM, d))],
            out_specs=tok(d),
            scratch_shapes=[pltpu.VMEM((2, SEG_ROWS, d), _BF), pltpu.SemaphoreType.DMA((2,))],
        ),
        out_shape=jax.ShapeDtypeStruct((t, d), _F32),
        compiler_params=pltpu.CompilerParams(dimension_semantics=("arbitrary",), vmem_limit_bytes=VMEM_LIMIT),
        name="combine",
    )(seg, h2.reshape(t, d), p, meta, ys, _row(ple_gate_norm[1]),
      ple_gate_w[1].astype(_BF), ple_w[1].astype(_BF))
    return out.reshape(b, s, d)
```

```python
import functools

import jax
import jax.numpy as jnp
from jax import lax
from jax.experimental import pallas as pl
from jax.experimental.pallas import tpu as pltpu

D_MODEL = 1024
PLE_DIM = 256
POOL_WINDOWS = (2, 4, 8, 16)
POOL_GROUP = D_MODEL // len(POOL_WINDOWS)
POOL_HALO = 32
POOL_ROWS = 256
HEAD_DIM = 64
N_HEADS = 16
N_KV_HEADS = 4
KV_DIM = N_KV_HEADS * HEAD_DIM
WINDOW = 128
ROPE_THETA = 10000.0
D_FF = 2816
FF_CHUNK = 256
N_EXPERTS = 8
D_FF_EXPERT = 1024
EPS = 1e-6
LANES = 128
ROUTER_PAD = LANES

SEQ_TILE = 512
ATTN_TILE = 1024
ROUTE_TILE = 512
EXPERT_TILE = 1024
SEG_ALIGN = 16
SEG_SIZES = tuple(ROUTE_TILE >> i for i in range((ROUTE_TILE // SEG_ALIGN).bit_length()))
FILL_SIZES = tuple((EXPERT_TILE // 2) >> i for i in range((EXPERT_TILE // 2 // SEG_ALIGN).bit_length()))
SEG_ROWS = 2 * ROUTE_TILE + N_EXPERTS * SEG_ALIGN
GATE_LANES = LANES
SORTED_WIDTH = D_MODEL + GATE_LANES
VMEM_LIMIT = 60 * 1024 * 1024

_BF = jnp.bfloat16
_F32 = jnp.float32


def _rms(x, g):
    return x * lax.rsqrt(jnp.mean(x * x, axis=-1, keepdims=True) + EPS) * g


def _sigmoid(x):
    return 0.5 * jnp.tanh(0.5 * x) + 0.5


def _dot(a, b):
    return jnp.dot(a, b, preferred_element_type=_F32)


def _head_meansq(x, hmat):
    x2 = (x * x).astype(_BF)
    width = hmat.shape[0]
    return jnp.concatenate([_dot(x2[:, c:c + width], hmat) for c in range(0, x.shape[1], width)], axis=1)


def _rope(x, cos, sin_signed):
    n = x.shape[1]
    reps = n // LANES
    lane = lax.broadcasted_iota(jnp.int32, (1, n), 1)
    first_half = (lane % HEAD_DIM) < (HEAD_DIM // 2)
    partner = jnp.where(first_half, pltpu.roll(x, n - HEAD_DIM // 2, 1), pltpu.roll(x, HEAD_DIM // 2, 1))
    c = jnp.concatenate([cos] * reps, axis=1)
    s = jnp.concatenate([sin_signed] * reps, axis=1)
    return x * c + partner * s


def _layer0_kernel(x_ref, p_ref, cos_ref, sin_ref, pool_norm, pool_w, pool_scale, ffn_norm, wgu, wd,
                   gate_norm, gate_w, ple_w, attn_norm, wq, q_gain, kv_norm, wkv, k_gain, hmat,
                   h_out, q_out, k_out, v_out, ubuf, sum_a, sum_b, h1_cur, uf_cur, h1_next, uf_next, h2_buf,
                   act_scr, *, tiles_per_seq):
    step = pl.program_id(0)
    si = jnp.minimum(step, pl.num_programs(0) - 3) % tiles_per_seq
    ts = x_ref.shape[0]
    halo, end = POOL_HALO, POOL_HALO + ts
    n_chunks = D_FF // FF_CHUNK
    assert ts % POOL_ROWS == 0 and ts // POOL_ROWS <= n_chunks

    @pl.when(step == 0)
    def _():
        h1_cur[...] = jnp.zeros_like(h1_cur)
        uf_cur[...] = jnp.zeros_like(uf_cur)
        h2_buf[...] = jnp.zeros_like(h2_buf)

    @pl.when(si == 0)
    def _():
        ubuf[0:halo, :] = jnp.zeros((halo, D_MODEL), _F32)

    def first_stage_rows(r0, n):
        a, b = halo + r0, halo + r0 + n
        xr = x_ref[r0:r0 + n, :]
        u = _rms(xr, pool_norm[...])
        ubuf[a:b, :] = u
        g1, g2, g3 = POOL_GROUP, 2 * POOL_GROUP, 3 * POOL_GROUP
        m = n + 24
        s2 = ubuf[a - 24:b, :] + ubuf[a - 25:b - 1, :]
        sum_a[0:m, g1:] = s2[:, g1:]
        s4 = sum_a[8:m, g1:] + sum_a[6:m - 2, g1:]
        sum_b[8:m, g2:] = s4[:, g1:]
        s8 = sum_b[16:m, g2:] + sum_b[12:m - 4, g2:]
        sum_a[16:m, g3:] = s8[:, g1:]
        s16 = sum_a[24:m, g3:] + sum_a[16:m - 8, g3:]
        window_sums = (s2[24:, :g1], s4[16:, :g1], s8[8:, :g1], s16)
        pos = (si * ts + r0 + 1 + lax.broadcasted_iota(jnp.int32, (n, 1), 0)).astype(_F32)
        mixed = []
        for g, win in enumerate(POOL_WINDOWS):
            ug = u[:, g * POOL_GROUP:(g + 1) * POOL_GROUP]
            d = window_sums[g] / jnp.minimum(pos, float(win)) - ug
            mixed.append(_dot(d.astype(_BF), pool_w[g]))
        h_mixed = xr + jnp.concatenate(mixed, axis=1) * pool_scale[...]
        h1_next[r0:r0 + n, :] = h_mixed
        uf_next[r0:r0 + n, :] = _rms(h_mixed, ffn_norm[...]).astype(_BF)

    t = {}

    def embed_gate_matmul():
        t["h"] = h2_buf[...]
        t["gate_pre"] = _dot(_rms(t["h"], gate_norm[...]).astype(_BF), gate_w[...])

    def embed_and_unit_norm():
        h = t["h"] + _sigmoid(t["gate_pre"]) * _dot(p_ref[...].astype(_BF), ple_w[...])
        h_out[...] = h
        t["unit"] = h * lax.rsqrt(jnp.mean(h * h, axis=-1, keepdims=True) + EPS)

    def q_matmul():
        t["q"] = _dot((t["unit"] * attn_norm[...]).astype(_BF), wq[...])

    def q_norm_rope():
        q = t["q"] * lax.rsqrt(_head_meansq(t["q"], hmat[...]) + EPS) * q_gain[...]
        q_out[...] = _rope(q, cos_ref[...], sin_ref[...]).astype(_BF)

    def kv_matmul():
        t["kv"] = _dot((t["unit"] * kv_norm[...]).astype(_BF), wkv[...])

    def k_norm_rope():
        k = t["kv"][:, :KV_DIM]
        k = k * lax.rsqrt(_head_meansq(k, hmat[...]) + EPS) * k_gain[...]
        k_out[...] = _rope(k, cos_ref[...], sin_ref[...]).astype(_BF)
        v_out[...] = t["kv"][:, KV_DIM:].astype(_BF)

    third_stage = (embed_gate_matmul, embed_and_unit_norm, q_matmul, q_norm_rope, kv_matmul, k_norm_rope)
    assert len(third_stage) <= n_chunks

    uf = uf_cur[...]
    for c in range(n_chunks):
        c0 = c * FF_CHUNK
        a = _dot(uf, wgu[:, c0:c0 + FF_CHUNK])
        g = _dot(uf, wgu[:, D_FF + c0:D_FF + c0 + FF_CHUNK])
        act_scr[:, c0:c0 + FF_CHUNK] = (a * _sigmoid(a) * g).astype(_BF)
        if c < len(third_stage):
            third_stage[c]()
        if c * POOL_ROWS < ts:
            first_stage_rows(c * POOL_ROWS, POOL_ROWS)
    acc = _dot(act_scr[...], wd[...])

    h2_buf[...] = h1_cur[...] + acc
    ubuf[0:halo, :] = ubuf[ts:end, :]
    h1_cur[...] = h1_next[...]
    uf_cur[...] = uf_next[...]


def _attn_kernel(sinks_ref, q_ref, kp_ref, kc_ref, vp_ref, vc_ref, h_ref, wo_ref, out_ref,
                 klo, khi, vlo, vhi, o_scr):
    i = pl.program_id(1)
    tq = q_ref.shape[0]
    nblk = tq // WINDOW
    lane = lax.broadcasted_iota(jnp.int32, (1, LANES), 1)
    low = lane < HEAD_DIM

    qi = lax.broadcasted_iota(jnp.int32, (2 * WINDOW, 2 * WINDOW), 0) % WINDOW
    kj = lax.broadcasted_iota(jnp.int32, (2 * WINDOW, 2 * WINDOW), 1)
    in_window = (kj > qi) & (kj <= qi + WINDOW)
    upper_rows = lax.broadcasted_iota(jnp.int32, (2 * WINDOW, 1), 0) >= WINDOW

    for g in range(N_KV_HEADS):
        slab = slice((g // 2) * LANES, (g // 2 + 1) * LANES)
        for src_p, src_c, lo_ref, hi_ref in ((kp_ref, kc_ref, klo, khi), (vp_ref, vc_ref, vlo, vhi)):
            t = jnp.concatenate([src_p[:, slab], src_c[:, slab]], axis=0).astype(_F32)
            r = pltpu.roll(t, HEAD_DIM, 1)
            in_low, in_high = (t, r) if g % 2 == 0 else (r, t)
            lo_ref[...] = jnp.where(low, in_low, 0.0).astype(_BF)
            hi_ref[...] = jnp.where(low, 0.0, in_high).astype(_BF)

        for j in range(nblk):
            qrows = pl.ds(j * WINDOW, WINDOW)
            krows = pl.ds(j * WINDOW, 2 * WINDOW)
            qg = jnp.concatenate([q_ref[qrows, (2 * g) * LANES:(2 * g + 1) * LANES],
                                  q_ref[qrows, (2 * g + 1) * LANES:(2 * g + 2) * LANES]], axis=0)
            mask = in_window & (kj >= jnp.where(i > 0, 0, WINDOW)) if j == 0 else in_window
            o = jnp.zeros((2 * WINDOW, LANES), _F32)
            for half, (k_ref, v_ref) in enumerate(((klo, vlo), (khi, vhi))):
                sink = jnp.where(upper_rows, sinks_ref[4 * g + 2 + half], sinks_ref[4 * g + half])
                s = lax.dot_general(qg, k_ref[krows, :], (((1,), (1,)), ((), ())),
                                    preferred_element_type=_F32)
                s = jnp.where(mask, s, -jnp.inf)
                m = jnp.maximum(jnp.max(s, axis=1, keepdims=True), sink)
                pr = jnp.exp(s - m)
                denom = jnp.sum(pr, axis=1, keepdims=True) + jnp.exp(sink - m)
                o = o + _dot(pr.astype(_BF), v_ref[krows, :]) / denom
            o_scr[qrows, (2 * g) * LANES:(2 * g + 1) * LANES] = o[:WINDOW].astype(_BF)
            o_scr[qrows, (2 * g + 1) * LANES:(2 * g + 2) * LANES] = o[WINDOW:].astype(_BF)

    out_ref[...] = h_ref[...] + _dot(o_scr[...], wo_ref[...])


def _segment_copies(rows, src_ref, src_base, dst_ref, dst_base, sem, sizes=SEG_SIZES):
    out = []
    for sz in sizes:
        done = rows & (-2 * sz)
        src = src_ref.at[pl.ds(pl.multiple_of(src_base + done, SEG_ALIGN), sz)]
        dst = dst_ref.at[pl.ds(pl.multiple_of(dst_base + done, SEG_ALIGN), sz)]
        out.append(((rows & sz) != 0, pltpu.make_async_copy(src, dst, sem)))
    return out


def _run(copies, op):
    for cond, cp in copies:
        @pl.when(cond)
        def _(cp=cp):
            getattr(cp, op)()


def _block_copies(seg_ref, blk, hbm_ref, buf_ref, sem, to_hbm):
    cap = hbm_ref.shape[0] // N_EXPERTS
    copies = []
    off = jnp.int32(0)
    for e in range(N_EXPERTS):
        base = e * cap + seg_ref[blk * 2 * N_EXPERTS + e]
        rows = seg_ref[blk * 2 * N_EXPERTS + N_EXPERTS + e]
        if to_hbm:
            copies += _segment_copies(rows, buf_ref, off, hbm_ref, base, sem)
        else:
            copies += _segment_copies(rows, hbm_ref, base, buf_ref, off, sem)
        off = off + rows
    return copies


def _wait_block(seg_ref, blk, hbm_ref, buf_ref, sem, to_hbm):
    rows = seg_ref[blk * 2 * N_EXPERTS + N_EXPERTS]
    for e in range(1, N_EXPERTS):
        rows = rows + seg_ref[blk * 2 * N_EXPERTS + N_EXPERTS + e]
    rows = pl.multiple_of(rows, SEG_ALIGN)
    src, dst = (buf_ref, hbm_ref) if to_hbm else (hbm_ref, buf_ref)
    pltpu.make_async_copy(src.at[pl.ds(0, rows)], dst.at[pl.ds(0, rows)], sem).wait()


def _route_kernel(h_ref, ffn_norm, rw_both, rb, xs_hbm, meta_ref, seg_ref, tot_ref,
                  comp_scr, zero_scr, cum_ref, sem):
    k = pl.program_id(0)
    bs = h_ref.shape[0]
    cap = xs_hbm.shape[0] // N_EXPERTS
    lane = lax.broadcasted_iota(jnp.int32, (1, ROUTER_PAD), 1)

    @pl.when(k == 0)
    def _():
        for e in range(N_EXPERTS):
            cum_ref[e] = 0

    u = _rms(h_ref[...], ffn_norm[...])
    u_hi = u.astype(_BF)
    u_lo = (u - u_hi.astype(_F32)).astype(_BF)
    both = _dot(u_hi, rw_both[...])
    logits = both[:, :ROUTER_PAD] + _dot(u_lo, rw_both[:, :ROUTER_PAD]) + both[:, ROUTER_PAD:] + rb[...]
    logits = jnp.where(lane < N_EXPERTS, logits, -jnp.inf)
    m1 = jnp.max(logits, axis=1, keepdims=True)
    i1 = jnp.min(jnp.where(logits == m1, lane, ROUTER_PAD), axis=1, keepdims=True)
    rest = jnp.where(lane == i1, -jnp.inf, logits)
    m2 = jnp.max(rest, axis=1, keepdims=True)
    i2 = jnp.min(jnp.where(rest == m2, lane, ROUTER_PAD), axis=1, keepdims=True)
    t = jnp.exp(m2 - m1)
    w1 = 1.0 / (1.0 + t)
    w2 = t * w1

    a1 = lane == i1
    a2 = lane == i2
    assigned = jnp.where(a1 | a2, 1.0, 0.0)
    before = (lax.broadcasted_iota(jnp.int32, (bs, bs), 1) < lax.broadcasted_iota(jnp.int32, (bs, bs), 0))
    rank = _dot(jnp.where(before, 1.0, 0.0).astype(_BF), assigned.astype(_BF))
    count = jnp.sum(assigned, axis=0, keepdims=True).astype(jnp.int32)
    seg_off = jnp.zeros((1, ROUTER_PAD), _F32)
    off = jnp.int32(0)
    for e in range(N_EXPERTS):
        rows = ((count[0, e] + (SEG_ALIGN - 1)) // SEG_ALIGN) * SEG_ALIGN
        seg_off = jnp.where(lane == e, off.astype(_F32), seg_off)
        base = cum_ref[e]
        seg_ref[k * 2 * N_EXPERTS + e] = base
        seg_ref[k * 2 * N_EXPERTS + N_EXPERTS + e] = rows
        cum_ref[e] = base + rows
        off = off + rows
    dst1 = jnp.sum(jnp.where(a1, rank + seg_off, 0.0), axis=1, keepdims=True)
    dst2 = jnp.sum(jnp.where(a2, rank + seg_off, 0.0), axis=1, keepdims=True)
    meta = jnp.where(lane == 0, dst1, jnp.where(lane == 1, dst2, jnp.where(lane == 2, w1, jnp.where(lane == 3, w2, 0.0))))
    meta_ref[...] = meta

    by_token = meta.T
    row = lax.broadcasted_iota(jnp.int32, (SEG_ROWS, 1), 0).astype(_F32)
    from1 = row == by_token[0:1, :]
    from2 = row == by_token[1:2, :]
    onehot = jnp.where(from1 | from2, 1.0, 0.0).astype(_BF)
    gate = jnp.sum(jnp.where(from1, by_token[2:3, :], 0.0) + jnp.where(from2, by_token[3:4, :], 0.0),
                   axis=1, keepdims=True)
    hi = gate.astype(_BF).astype(_F32)
    mid = (gate - hi).astype(_BF).astype(_F32)
    gate_parts = jnp.where(lane == 0, hi, jnp.where(lane == 1, mid, jnp.where(lane == 2, gate - hi - mid, 0.0)))

    slot = k % 2
    comp = comp_scr.at[slot]
    comp[:, :D_MODEL] = _dot(onehot, u_hi).astype(_BF)
    comp[:, D_MODEL:] = gate_parts.astype(_BF)
    _run(_block_copies(seg_ref, k, xs_hbm, comp, sem.at[slot], to_hbm=True), "start")

    @pl.when(k > 0)
    def _():
        _wait_block(seg_ref, k - 1, xs_hbm, comp_scr.at[1 - slot], sem.at[1 - slot], to_hbm=True)

    @pl.when(k == pl.num_programs(0) - 1)
    def _():
        _wait_block(seg_ref, k, xs_hbm, comp, sem.at[slot], to_hbm=True)
        zero_scr[...] = jnp.zeros_like(zero_scr)
        fills = []
        for e in range(N_EXPERTS):
            total = cum_ref[e]
            tot_ref[e] = total
            fills += _segment_copies((-total) & (EXPERT_TILE - 1), zero_scr, 0, xs_hbm, e * cap + total,
                                     sem.at[slot], sizes=FILL_SIZES)
        _run(fills, "start")
        _run(fills, "wait")


def _expert_kernel(te_ref, tb_ref, nt_ref, x_ref, wa_f32, wg_f32, wd_f32, y_ref, wa, wg, wd):
    i = pl.program_id(0)

    @pl.when((i == 0) | (te_ref[i] != te_ref[jnp.maximum(i - 1, 0)]))
    def _():
        wa[...] = wa_f32[...].astype(_BF)
        wg[...] = wg_f32[...].astype(_BF)
        wd[...] = wd_f32[...].astype(_BF)

    @pl.when(i < nt_ref[0])
    def _():
        x = x_ref[:, :D_MODEL]
        gate = jnp.sum(x_ref[:, D_MODEL:].astype(_F32), axis=1, keepdims=True)
        a = _dot(x, wa[...])
        g = _dot(x, wg[...])
        act = (a * _sigmoid(a) * g).astype(_BF)
        y_ref[...] = (gate * _dot(act, wd[...])).astype(_BF)


def _combine_kernel(seg_ref, h_ref, p_ref, meta_ref, ys_hbm, gate_norm, gate_w, ple_w, out_ref, ycat, sem):
    k = pl.program_id(0)
    slot = k % 2
    fetch = lambda blk, s, op: _run(_block_copies(seg_ref, blk, ys_hbm, ycat.at[s], sem.at[s], to_hbm=False), op)

    @pl.when(k == 0)
    def _():
        ycat[...] = jnp.zeros_like(ycat)
        fetch(k, slot, "start")

    @pl.when(k + 1 < pl.num_programs(0))
    def _():
        fetch(k + 1, 1 - slot, "start")

    pos = lax.broadcasted_iota(jnp.int32, (1, SEG_ROWS), 1).astype(_F32)
    meta = meta_ref[...]
    onehot = jnp.where((pos == meta[:, 0:1]) | (pos == meta[:, 1:2]), 1.0, 0.0).astype(_BF)
    embed = _dot(p_ref[...].astype(_BF), ple_w[...])

    _wait_block(seg_ref, k, ys_hbm, ycat.at[slot], sem.at[slot], to_hbm=False)
    h = h_ref[...] + _dot(onehot, ycat[slot])
    gate = _sigmoid(_dot(_rms(h, gate_norm[...]).astype(_BF), gate_w[...]))
    out_ref[...] = h + gate * embed


def _const(shape):
    return pl.BlockSpec(shape, lambda *_: (0,) * len(shape), pipeline_mode=pl.Buffered(1))


def _row(v):
    return v.reshape(1, -1).astype(_F32)


def kernel(x, p, pool_norm, pool_w, pool_scale, kv_norm, w_kv, k_norm, attn_norm, w_q, q_norm, sinks, w_o,
           ffn_norm, w_gu, w_down, router_w, router_b, we_gu, we_down, ple_gate_norm, ple_gate_w, ple_w):
    b, s, d = x.shape
    ts, tq = SEQ_TILE, ATTN_TILE

    inv = ROPE_THETA ** (-jnp.arange(0, HEAD_DIM, 2, dtype=_F32) / HEAD_DIM)
    ang = jnp.arange(s, dtype=_F32)[:, None] * inv[None, :]
    cos_t = jnp.tile(jnp.cos(ang), (1, LANES // (HEAD_DIM // 2)))
    sin_t = jnp.tile(jnp.concatenate([-jnp.sin(ang), jnp.sin(ang)], axis=1), (1, LANES // HEAD_DIM))
    hid = jnp.arange(KV_DIM) // HEAD_DIM
    hmat = ((hid[:, None] == hid[None, :]) * (1.0 / HEAD_DIM)).astype(_BF)

    q_gain = jnp.tile(q_norm[0].astype(_F32) * (HEAD_DIM ** -0.5), N_HEADS).reshape(1, d)
    k_gain = jnp.tile(k_norm.astype(_F32), N_KV_HEADS).reshape(1, KV_DIM)

    tps = s // ts
    n_tiles0 = b * tps
    first = lambda st: jnp.minimum(st, n_tiles0 - 1)
    third = lambda st: jnp.maximum(st - 2, 0)
    tile3 = lambda w: pl.BlockSpec((None, ts, w), lambda st: (third(st) // tps, third(st) % tps, 0))
    h1, q, k, v = pl.pallas_call(
        functools.partial(_layer0_kernel, tiles_per_seq=tps),
        grid=(n_tiles0 + 2,),
        in_specs=[
            pl.BlockSpec((None, ts, d), lambda st: (first(st) // tps, first(st) % tps, 0)),
            pl.BlockSpec((None, None, ts, PLE_DIM), lambda st: (0, third(st) // tps, third(st) % tps, 0)),
            pl.BlockSpec((ts, LANES), lambda st: (third(st) % tps, 0)),
            pl.BlockSpec((ts, LANES), lambda st: (third(st) % tps, 0)),
            _const((1, d)), _const((len(POOL_WINDOWS), POOL_GROUP, POOL_GROUP)), _const((1, d)),
            _const((1, d)), _const((d, 2 * D_FF)), _const((D_FF, d)),
            _const((1, d)), _const((d, d)), _const((PLE_DIM, d)),
            _const((1, d)), _const((d, d)), _const((1, d)),
            _const((1, d)), _const((d, 2 * KV_DIM)), _const((1, KV_DIM)), _const((KV_DIM, KV_DIM)),
        ],
        out_specs=[tile3(d), tile3(d), tile3(KV_DIM), tile3(KV_DIM)],
        out_shape=[jax.ShapeDtypeStruct((b, s, d), _F32), jax.ShapeDtypeStruct((b, s, d), _BF),
                   jax.ShapeDtypeStruct((b, s, KV_DIM), _BF), jax.ShapeDtypeStruct((b, s, KV_DIM), _BF)],
        scratch_shapes=[pltpu.VMEM((POOL_HALO + ts, d), _F32)] + [pltpu.VMEM((POOL_ROWS + 24, d), _F32)] * 2
                       + [pltpu.VMEM((ts, d), _F32), pltpu.VMEM((ts, d), _BF)] * 2 + [pltpu.VMEM((ts, d), _F32)]
                       + [pltpu.VMEM((ts, D_FF), _BF)],
        compiler_params=pltpu.CompilerParams(dimension_semantics=("arbitrary",), vmem_limit_bytes=VMEM_LIMIT),
        name="layer0",
    )(x, p, cos_t, sin_t, _row(pool_norm[0]), pool_w[0].astype(_BF), _row(pool_scale[0]),
      _row(ffn_norm[0]), w_gu[0].astype(_BF), w_down[0].astype(_BF), _row(ple_gate_norm[0]), ple_gate_w[0].astype(_BF), ple_w[0].astype(_BF),
      _row(attn_norm[0]), w_q[0].astype(_BF), q_gain, _row(kv_norm), w_kv.astype(_BF), k_gain, hmat)

    blocks_per_tile = tq // WINDOW
    prev = lambda bi, qi, *_: (bi, jnp.maximum(qi * blocks_per_tile - 1, 0), 0)
    cur = lambda bi, qi, *_: (bi, qi, 0)
    h2 = pl.pallas_call(
        _attn_kernel,
        grid_spec=pltpu.PrefetchScalarGridSpec(
            num_scalar_prefetch=1,
            grid=(b, s // tq),
            in_specs=[
                pl.BlockSpec((None, tq, d), cur),
                pl.BlockSpec((None, WINDOW, KV_DIM), prev), pl.BlockSpec((None, tq, KV_DIM), cur),
                pl.BlockSpec((None, WINDOW, KV_DIM), prev), pl.BlockSpec((None, tq, KV_DIM), cur),
                pl.BlockSpec((None, tq, d), cur),
                pl.BlockSpec((d, d), lambda *_: (0, 0), pipeline_mode=pl.Buffered(1)),
            ],
            out_specs=pl.BlockSpec((None, tq, d), cur),
            scratch_shapes=[pltpu.VMEM((WINDOW + tq, LANES), _BF)] * 4 + [pltpu.VMEM((tq, d), _BF)],
        ),
        out_shape=jax.ShapeDtypeStruct((b, s, d), _F32),
        compiler_params=pltpu.CompilerParams(dimension_semantics=("arbitrary", "arbitrary"),
                                             vmem_limit_bytes=VMEM_LIMIT),
        name="attn",
    )(sinks[0].astype(_F32), q, k, k, v, v, h1, w_o[0].astype(_BF))

    t = b * s
    rw = jnp.pad(router_w[0].astype(_F32), ((0, 0), (0, ROUTER_PAD - N_EXPERTS)))
    rw_hi = rw.astype(_BF)
    rw_lo = (rw - rw_hi.astype(_F32)).astype(_BF)
    rb = jnp.pad(router_b[0].astype(_F32), (0, ROUTER_PAD - N_EXPERTS)).reshape(1, ROUTER_PAD)

    bs, te = ROUTE_TILE, EXPERT_TILE
    nblk = t // bs
    cap = t
    smem = pl.BlockSpec(memory_space=pltpu.SMEM)
    tok = lambda w: pl.BlockSpec((bs, w), lambda ki, *_: (ki, 0))
    xs, meta, seg, totals = pl.pallas_call(
        _route_kernel,
        grid=(nblk,),
        in_specs=[tok(d), _const((1, d)), _const((d, 2 * ROUTER_PAD)), _const((1, ROUTER_PAD))],
        out_specs=[pl.BlockSpec(memory_space=pl.ANY), tok(ROUTER_PAD), smem, smem],
        out_shape=[jax.ShapeDtypeStruct((N_EXPERTS * cap, SORTED_WIDTH), _BF),
                   jax.ShapeDtypeStruct((t, ROUTER_PAD), _F32),
                   jax.ShapeDtypeStruct((nblk * 2 * N_EXPERTS,), jnp.int32),
                   jax.ShapeDtypeStruct((N_EXPERTS,), jnp.int32)],
        scratch_shapes=[pltpu.VMEM((2, SEG_ROWS, SORTED_WIDTH), _BF), pltpu.VMEM((EXPERT_TILE, SORTED_WIDTH), _BF),
                        pltpu.SMEM((N_EXPERTS,), jnp.int32), pltpu.SemaphoreType.DMA((2,))],
        compiler_params=pltpu.CompilerParams(dimension_semantics=("arbitrary",), vmem_limit_bytes=VMEM_LIMIT),
        name="route",
    )(h2.reshape(t, d), _row(ffn_norm[1]), jnp.concatenate([rw_hi, rw_lo], axis=1), rb)

    n_steps = (2 * t + nblk * N_EXPERTS * (SEG_ALIGN - 1)) // te + N_EXPERTS
    tiles_e = (totals + te - 1) // te
    tile_end = jnp.cumsum(tiles_e)
    n_tiles = tile_end[-1]
    step = jnp.minimum(jnp.arange(n_steps, dtype=jnp.int32), n_tiles - 1)
    tile_expert = jnp.sum(step[:, None] >= tile_end[None, :], axis=1).astype(jnp.int32)
    tile_block = (tile_expert * (cap // te) + step - (tile_end - tiles_e)[tile_expert]).astype(jnp.int32)

    xrow = lambda w: pl.BlockSpec((te, w), lambda i, te_r, tb_r, nt_r: (tb_r[i], 0))
    wexp = lambda r, c, cb: pl.BlockSpec((None, None, r, c), lambda i, te_r, tb_r, nt_r: (0, te_r[i], 0, cb))
    ys = pl.pallas_call(
        _expert_kernel,
        grid_spec=pltpu.PrefetchScalarGridSpec(
            num_scalar_prefetch=3,
            grid=(n_steps,),
            in_specs=[xrow(SORTED_WIDTH), wexp(d, D_FF_EXPERT, 0), wexp(d, D_FF_EXPERT, 1), wexp(D_FF_EXPERT, d, 0)],
            out_specs=xrow(d),
            scratch_shapes=[pltpu.VMEM((d, D_FF_EXPERT), _BF)] * 2 + [pltpu.VMEM((D_FF_EXPERT, d), _BF)],
        ),
        out_shape=jax.ShapeDtypeStruct((N_EXPERTS * cap, d), _BF),
        compiler_params=pltpu.CompilerParams(dimension_semantics=("arbitrary",), vmem_limit_bytes=VMEM_LIMIT),
        name="experts",
    )(tile_expert, tile_block, n_tiles.reshape(1).astype(jnp.int32), xs, we_gu, we_gu, we_down)

    out = pl.pallas_call(
        _combine_kernel,
        grid_spec=pltpu.PrefetchScalarGridSpec(
            num_scalar_prefetch=1,
            grid=(nblk,),
            in_specs=[tok(d),
                      pl.BlockSpec((None, None, bs, PLE_DIM), lambda ki, *_: (1, ki // (s // bs), ki % (s // bs), 0)),
                      tok(ROUTER_PAD), pl.BlockSpec(memory_space=pl.ANY),
                      _const((1, d)), _const((d, d)), _const((PLE_DIM, d))],
            out_specs=tok(d),
            scratch_shapes=[pltpu.VMEM((2, SEG_ROWS, d), _BF), pltpu.SemaphoreType.DMA((2,))],
        ),
        out_shape=jax.ShapeDtypeStruct((t, d), _F32),
        compiler_params=pltpu.CompilerParams(dimension_semantics=("arbitrary",), vmem_limit_bytes=VMEM_LIMIT),
        name="combine",
    )(seg, h2.reshape(t, d), p, meta, ys, _row(ple_gate_norm[1]),
      ple_gate_w[1].astype(_BF), ple_w[1].astype(_BF))
    return out.reshape(b, s, d)
```
